```python
import math
import jax, jax.numpy as jnp
from jax import lax
import numpy as np

D_MODEL = 2048
BATCH = 16
SEQ = 2048
DEPTH = 1

ATTN_HEADS = 16
ATTN_KV_HEADS = 4
ATTN_HEAD_DIM = 64
ATTN_GROUP = ATTN_HEADS // ATTN_KV_HEADS
WINDOW = 128
ATTN_BLOCK = WINDOW
ATTN_WIDTH = ATTN_HEADS * ATTN_HEAD_DIM
KV_WIDTH = ATTN_KV_HEADS * ATTN_HEAD_DIM
HGRN_HEADS = 8
HGRN_KEY_DIM = 128
HGRN_VALUE_DIM = 128
HGRN_WIDTH = HGRN_HEADS * HGRN_VALUE_DIM
HGRN_CHUNK = 64
REL_BUCKETS = 32
REL_MAX_DIST = 128
NORM_EPS = 1e-6
IN_WIDTHS = (ATTN_WIDTH, KV_WIDTH, KV_WIDTH, ATTN_WIDTH,
             HGRN_HEADS * HGRN_KEY_DIM, HGRN_HEADS * HGRN_KEY_DIM, HGRN_WIDTH, HGRN_WIDTH,
             D_MODEL, D_MODEL)
IN_PROJ_WIDTH = 10752

kernel_name = 'hybrid_swa_sinks_hgrn2_gated_merge'


def rms_norm(x, gain):
    xf = x.astype(jnp.float32)
    y = xf * lax.rsqrt(jnp.mean(xf * xf, axis=-1, keepdims=True) + NORM_EPS)
    return (y * gain.astype(jnp.float32)).astype(x.dtype)


def t5_bucket(dist):
    max_exact = REL_BUCKETS // 2
    d = jnp.maximum(dist, 0)
    df = jnp.maximum(d, 1).astype(jnp.float32)
    large = max_exact + (jnp.log(df / max_exact) / math.log(REL_MAX_DIST / max_exact)
                         * (REL_BUCKETS - max_exact)).astype(jnp.int32)
    large = jnp.minimum(large, REL_BUCKETS - 1)
    return jnp.where(d < max_exact, d, large)


def sliding_window_attention(q, k, v, sinks, rel_bias):
    B, S, _ = q.shape
    nb = S // ATTN_BLOCK
    qb = q.astype(jnp.float32).reshape(B, nb, ATTN_BLOCK, ATTN_KV_HEADS, ATTN_GROUP, ATTN_HEAD_DIM)

    def windows(t):
        t = t.astype(jnp.float32).reshape(B, S, ATTN_KV_HEADS, ATTN_HEAD_DIM)
        t = jnp.pad(t, ((0, 0), (ATTN_BLOCK, 0), (0, 0), (0, 0)))
        t = t.reshape(B, nb + 1, ATTN_BLOCK, ATTN_KV_HEADS, ATTN_HEAD_DIM)
        return jnp.concatenate([t[:, :-1], t[:, 1:]], axis=2)

    kw, vw = windows(k), windows(v)
    scores = jnp.einsum('bnqhgd,bnshd->bhgnqs', qb, kw) * (ATTN_HEAD_DIM ** -0.5)
    qi = jnp.arange(ATTN_BLOCK)[:, None]
    si = jnp.arange(2 * ATTN_BLOCK)[None, :]
    dist = qi + ATTN_BLOCK - si
    band = (dist >= 0) & (dist < WINDOW)
    key_pos = jnp.arange(nb)[:, None] * ATTN_BLOCK - ATTN_BLOCK + jnp.arange(2 * ATTN_BLOCK)[None, :]
    mask = band[None] & (key_pos >= 0)[:, None, :]
    bias = rel_bias[t5_bucket(dist)].astype(jnp.float32)
    bias = jnp.transpose(bias, (2, 0, 1)).reshape(ATTN_KV_HEADS, ATTN_GROUP, 1, ATTN_BLOCK, 2 * ATTN_BLOCK)
    scores = jnp.where(mask, scores + bias, -jnp.inf)
    sink = jnp.broadcast_to(sinks.astype(jnp.float32).reshape(ATTN_KV_HEADS, ATTN_GROUP, 1, 1, 1),
                            scores.shape[:-1] + (1,))
    probs = jax.nn.softmax(jnp.concatenate([scores, sink], axis=-1), axis=-1)[..., :-1]
    out = jnp.einsum('bhgnqs,bnshd->bnqhgd', probs, vw)
    return out.reshape(B, S, ATTN_WIDTH)


def hgrn2_recurrence(q, f_pre, i, lb):
    B, S, _ = q.shape
    nc = S // HGRN_CHUNK

    def chunks(t):
        return t.reshape(B, nc, HGRN_CHUNK, HGRN_HEADS, -1).transpose(1, 0, 3, 2, 4)

    lbf = lb.astype(jnp.float32)
    f = lbf + (1.0 - lbf) * jax.nn.sigmoid(f_pre.astype(jnp.float32))
    qc = chunks(jax.nn.silu(q.astype(jnp.float32)))
    kc = chunks(1.0 - f)
    vc = chunks(i.astype(jnp.float32))
    gc = jnp.cumsum(chunks(jnp.log(f)), axis=3)
    causal = jnp.tril(jnp.ones((HGRN_CHUNK, HGRN_CHUNK), dtype=bool))

    def step(state, inp):
        qt, kt, vt, gt = inp
        inter = jnp.einsum('bhcd,bhde->bhce', qt * jnp.exp(gt), state)
        diff = gt[:, :, :, None, :] - gt[:, :, None, :, :]
        decay = jnp.exp(jnp.where(causal[:, :, None], diff, -jnp.inf))
        attn = jnp.einsum('bhtd,bhsd,bhtsd->bhts', qt, kt, decay)
        intra = jnp.einsum('bhts,bhse->bhte', attn, vt)
        g_last = gt[:, :, -1]
        k_dec = kt * jnp.exp(g_last[:, :, None, :] - gt)
        new_state = jnp.exp(g_last)[..., None] * state + jnp.einsum('bhsd,bhse->bhde', k_dec, vt)
        return new_state, inter + intra

    s0 = jnp.zeros((B, HGRN_HEADS, HGRN_KEY_DIM, HGRN_VALUE_DIM), jnp.float32)
    _, o = lax.scan(step, s0, (qc, kc, vc, gc))
    return o.transpose(1, 0, 3, 2, 4).reshape(B, S, HGRN_HEADS, HGRN_VALUE_DIM)


def setup_inputs(seed: int = 0) -> dict:
    key = jax.random.key(seed)
    ks = jax.random.split(key, 12)
    f32 = jnp.float32
    return {
        'x': jax.random.normal(ks[0], (BATCH, SEQ, D_MODEL), f32),
        'norm_pre': 1.0 + 0.1 * jax.random.normal(ks[1], (DEPTH, D_MODEL), f32),
        'w_in': jax.random.normal(ks[2], (DEPTH, D_MODEL, IN_PROJ_WIDTH), f32) * D_MODEL ** -0.5,
        'rel_bias': 0.5 * jax.random.normal(ks[3], (REL_BUCKETS, ATTN_HEADS), f32),
        'attn_sinks': jax.random.normal(ks[4], (DEPTH, ATTN_HEADS), f32),
        'lb_logits': 0.5 * jax.random.normal(ks[5], (DEPTH + 1, HGRN_HEADS * HGRN_KEY_DIM), f32),
        'hgrn_norm': 1.0 + 0.1 * jax.random.normal(ks[6], (DEPTH, HGRN_HEADS, HGRN_VALUE_DIM), f32),
        'w_branch_attn': jax.random.normal(ks[7], (DEPTH, ATTN_WIDTH, D_MODEL), f32) * ATTN_WIDTH ** -0.5,
        'w_branch_hgrn': jax.random.normal(ks[8], (DEPTH, HGRN_WIDTH, D_MODEL), f32) * HGRN_WIDTH ** -0.5,
        'w_out': jax.random.normal(ks[9], (DEPTH, D_MODEL, D_MODEL), f32) * D_MODEL ** -0.5,
        'norm_post': 1.0 + 0.1 * jax.random.normal(ks[10], (DEPTH, D_MODEL), f32),
    }


def reference(x, norm_pre, w_in, rel_bias, attn_sinks, lb_logits, hgrn_norm,
              w_branch_attn, w_branch_hgrn, w_out, norm_post):
    split_points = [int(p) for p in np.cumsum(IN_WIDTHS)[:-1]]
    lower_bounds = jnp.cumsum(jax.nn.softmax(lb_logits.astype(jnp.float32), axis=0), axis=0)[:DEPTH]
    for layer in range(DEPTH):
        h = rms_norm(x, norm_pre[layer])
        proj = jnp.matmul(h, w_in[layer])
        aq, ak, av, ag, hq, hf, hi, hg, gate_a, gate_h = jnp.split(proj, split_points, axis=-1)
        ya = sliding_window_attention(aq, ak, av, attn_sinks[layer], rel_bias)
        ya = (ya * jax.nn.silu(ag.astype(jnp.float32))).astype(x.dtype)
        oh = rms_norm(hgrn2_recurrence(hq, hf, hi, lower_bounds[layer]), hgrn_norm[layer])
        yh = (oh.reshape(oh.shape[0], oh.shape[1], HGRN_WIDTH)
              * jax.nn.silu(hg.astype(jnp.float32))).astype(x.dtype)
        ua = jnp.matmul(ya, w_branch_attn[layer])
        uh = jnp.matmul(yh, w_branch_hgrn[layer])
        merged = jax.nn.sigmoid(gate_a) * ua + jax.nn.sigmoid(gate_h) * uh
        y = jnp.matmul(merged, w_out[layer])
        x = x + rms_norm(y, norm_post[layer]).astype(x.dtype)
    return x
```

```python
import functools
import math

import jax
import jax.numpy as jnp
from jax import lax
from jax.experimental import pallas as pl
from jax.experimental.pallas import tpu as pltpu

D_MODEL = 2048
ATTN_HEADS = 16
ATTN_KV_HEADS = 4
ATTN_HEAD_DIM = 64
ATTN_GROUP = ATTN_HEADS // ATTN_KV_HEADS
WINDOW = 128
ATTN_WIDTH = ATTN_HEADS * ATTN_HEAD_DIM
KV_WIDTH = ATTN_KV_HEADS * ATTN_HEAD_DIM
HGRN_HEADS = 8
HGRN_DIM = 128
HGRN_WIDTH = HGRN_HEADS * HGRN_DIM
CHUNK = 64
CHUNK_LEVELS = 6
REL_BUCKETS = 32
REL_MAX_DIST = 128
NORM_EPS = 1e-6
IN_PROJ_WIDTH = 10752

OFF_AQ = 0
OFF_AK = OFF_AQ + ATTN_WIDTH
OFF_AV = OFF_AK + KV_WIDTH
OFF_AG = OFF_AV + KV_WIDTH
OFF_HQ = OFF_AG + ATTN_WIDTH
OFF_HF = OFF_HQ + HGRN_WIDTH
OFF_HI = OFF_HF + HGRN_WIDTH
OFF_HG = OFF_HI + HGRN_WIDTH
OFF_GA = OFF_HG + HGRN_WIDTH
OFF_GH = OFF_GA + D_MODEL

LANES = 128
MASK_VALUE = -1e30

F32 = jnp.float32
BF16 = jnp.bfloat16

_NT = (((1,), (1,)), ((), ()))
_TN = (((0,), (0,)), ((), ()))


def _sigmoid(x):
    return 1.0 / (1.0 + jnp.exp(-x))


def _silu(x):
    return x * _sigmoid(x)


IN_TM = 1024
IN_TN = 1536
IN_NORM_ROWS = 256


def _inproj_kernel(x_ref, gain_ref, w_ref, o_ref, h_ref):
    @pl.when(pl.program_id(1) == 0)
    def _():
        gain = gain_ref[...]
        for r in range(0, IN_TM, IN_NORM_ROWS):
            xf = x_ref[r:r + IN_NORM_ROWS, :]
            ms = jnp.mean(xf * xf, axis=-1, keepdims=True)
            h_ref[r:r + IN_NORM_ROWS, :] = (xf * lax.rsqrt(ms + NORM_EPS) * gain).astype(BF16)

    o_ref[...] = jnp.dot(h_ref[...], w_ref[...], preferred_element_type=F32).astype(BF16)


def _inproj(x2d, gain, w_bf16):
    tokens = x2d.shape[0]
    return pl.pallas_call(
        _inproj_kernel,
        grid=(tokens // IN_TM, IN_PROJ_WIDTH // IN_TN),
        in_specs=[
            pl.BlockSpec((IN_TM, D_MODEL), lambda i, j: (i, 0)),
            pl.BlockSpec((1, D_MODEL), lambda i, j: (0, 0)),
            pl.BlockSpec((D_MODEL, IN_TN), lambda i, j: (0, j)),
        ],
        out_specs=pl.BlockSpec((IN_TM, IN_TN), lambda i, j: (i, j)),
        out_shape=jax.ShapeDtypeStruct((tokens, IN_PROJ_WIDTH), BF16),
        scratch_shapes=[pltpu.VMEM((IN_TM, D_MODEL), BF16)],
        compiler_params=pltpu.CompilerParams(
            dimension_semantics=("parallel", "arbitrary"),
            vmem_limit_bytes=56 * 1024 * 1024),
        name="inproj",
    )(x2d, gain, w_bf16)


def _t5_bucket(dist):
    max_exact = REL_BUCKETS // 2
    d = jnp.maximum(dist, 0)
    df = jnp.maximum(d, 1).astype(F32)
    large = max_exact + (jnp.log(df / max_exact) / math.log(REL_MAX_DIST / max_exact)
                         * (REL_BUCKETS - max_exact)).astype(jnp.int32)
    large = jnp.minimum(large, REL_BUCKETS - 1)
    return jnp.where(d < max_exact, d, large)


def _swap_halves(x):
    half = LANES // 2
    return jnp.concatenate([x[:, half:], x[:, :half]], axis=1)


def _attn_kernel(relb_ref, sink_ref, q_ref, kc_ref, vc_ref, kp_ref, vp_ref, ga_ref, gb_ref,
                 o_ref, bias_ref):
    blk = pl.program_id(1)

    @pl.when((pl.program_id(0) == 0) & (blk == 0))
    def _():
        qi = lax.broadcasted_iota(jnp.int32, (WINDOW, 2 * WINDOW), 0)
        si = lax.broadcasted_iota(jnp.int32, (WINDOW, 2 * WINDOW), 1)
        dist = qi + WINDOW - si
        band = (dist >= 0) & (dist < WINDOW)
        bucket = _t5_bucket(dist)
        for h in range(ATTN_HEADS):
            def pick(b, acc, h=h):
                return jnp.where(bucket == b, relb_ref[b, h], acc)
            tbl = lax.fori_loop(0, REL_BUCKETS, pick, jnp.zeros((WINDOW, 2 * WINDOW), F32))
            bias_ref[h] = jnp.where(band, tbl, MASK_VALUE)

    lane = lax.broadcasted_iota(jnp.int32, (1, LANES), 1)
    is_lo = lane < LANES // 2
    scale = ATTN_HEAD_DIM ** -0.5
    keep_lo = jnp.where(is_lo, 1.0, 0.0).astype(BF16)
    keep_hi = jnp.where(is_lo, 0.0, 1.0).astype(BF16)
    col = lax.broadcasted_iota(jnp.int32, (1, 2 * WINDOW), 1)
    prev_mask = jnp.where((blk == 0) & (col < WINDOW), MASK_VALUE, 0.0)
    row = lax.broadcasted_iota(jnp.int32, (2 * WINDOW, 1), 0)
    top = row < WINDOW

    for hk in range(ATTN_KV_HEADS):
        c, e = divmod(hk, 2)
        cs = slice(c * LANES, (c + 1) * LANES)
        k2 = jnp.concatenate([kp_ref[:, cs], kc_ref[:, cs]], axis=0)
        v2 = jnp.concatenate([vp_ref[:, cs], vc_ref[:, cs]], axis=0)
        k2s, v2s = _swap_halves(k2), _swap_halves(v2)
        keep_e, keep_o = (keep_lo, keep_hi) if e == 0 else (keep_hi, keep_lo)
        p0 = q_ref[:, hk * 2 * LANES:hk * 2 * LANES + LANES]
        p1 = q_ref[:, hk * 2 * LANES + LANES:(hk + 1) * 2 * LANES]
        q_same = jnp.concatenate([p0 * (keep_e * scale), p1 * (keep_e * scale)], axis=0)
        q_diff = jnp.concatenate([p0 * (keep_o * scale), p1 * (keep_o * scale)], axis=0)
        h_same = (4 * hk + e, 4 * hk + 2 + e)
        h_diff = (4 * hk + 1 - e, 4 * hk + 3 - e)

        def probs(qm, km, heads):
            s = lax.dot_general(qm, km, _NT, preferred_element_type=F32)
            bias = jnp.concatenate([bias_ref[heads[0]], bias_ref[heads[1]]], axis=0)
            s = s + bias + prev_mask
            sink = jnp.where(top, sink_ref[0, heads[0]], sink_ref[0, heads[1]])
            m = jnp.maximum(jnp.max(s, axis=-1, keepdims=True), sink)
            p = jnp.exp(s - m)
            denom = jnp.sum(p, axis=-1, keepdims=True) + jnp.exp(sink - m)
            return p.astype(BF16), 1.0 / denom

        p_same, r_same = probs(q_same, k2, h_same)
        p_diff, r_diff = probs(q_diff, k2s, h_diff)
        if e == 0:
            p_lo, p_hi, r_lo, r_hi = p_same, p_diff, r_same, r_diff
            v_lo, v_hi = v2 * keep_lo, v2s * keep_hi
        else:
            p_lo, p_hi, r_lo, r_hi = p_diff, p_same, r_diff, r_same
            v_lo, v_hi = v2s * keep_lo, v2 * keep_hi
        pv = jnp.dot(jnp.concatenate([p_lo, p_hi], axis=1),
                     jnp.concatenate([v_lo, v_hi], axis=0),
                     preferred_element_type=F32)
        pv = pv * jnp.where(is_lo, r_lo, r_hi)
        g_ref = ga_ref if hk < 2 else gb_ref
        for i in range(2):
            gs = slice((hk % 2) * 2 * LANES + i * LANES, (hk % 2) * 2 * LANES + (i + 1) * LANES)
            gate = _silu(g_ref[:, gs].astype(F32))
            o_ref[:, (2 * hk + i) * LANES:(2 * hk + i + 1) * LANES] = (
                pv[i * WINDOW:(i + 1) * WINDOW] * gate).astype(BF16)


def _attention(proj, rel_bias, sinks, batch, seq):
    nb = seq // WINDOW
    kcol = OFF_AK // KV_WIDTH
    vcol = OFF_AV // KV_WIDTH
    gcol = OFF_AG // (ATTN_WIDTH // 2)
    cur = lambda b, n: b * nb + n
    prev = lambda b, n: b * nb + jnp.maximum(n - 1, 0)
    smem = pl.BlockSpec(memory_space=pltpu.SMEM)
    return pl.pallas_call(
        _attn_kernel,
        grid=(batch, nb),
        in_specs=[
            smem, smem,
            pl.BlockSpec((WINDOW, ATTN_WIDTH), lambda b, n: (cur(b, n), 0)),
            pl.BlockSpec((WINDOW, KV_WIDTH), lambda b, n: (cur(b, n), kcol)),
            pl.BlockSpec((WINDOW, KV_WIDTH), lambda b, n: (cur(b, n), vcol)),
            pl.BlockSpec((WINDOW, KV_WIDTH), lambda b, n: (prev(b, n), kcol)),
            pl.BlockSpec((WINDOW, KV_WIDTH), lambda b, n: (prev(b, n), vcol)),
            pl.BlockSpec((WINDOW, ATTN_WIDTH // 2), lambda b, n: (cur(b, n), gcol)),
            pl.BlockSpec((WINDOW, ATTN_WIDTH // 2), lambda b, n: (cur(b, n), gcol + 1)),
        ],
        out_specs=pl.BlockSpec((WINDOW, ATTN_WIDTH), lambda b, n: (cur(b, n), 0)),
        out_shape=jax.ShapeDtypeStruct((batch * seq, ATTN_WIDTH), BF16),
        scratch_shapes=[pltpu.VMEM((ATTN_HEADS, WINDOW, 2 * WINDOW), F32)],
        compiler_params=pltpu.CompilerParams(dimension_semantics=("arbitrary", "arbitrary")),
        name="swa",
    )(rel_bias, sinks, proj, proj, proj, proj, proj, proj, proj)


def _rows(g, start, n):
    return jnp.broadcast_to(g[start:start + 1, :], (n, g.shape[1]))


def _level_reference(g, level, sub):
    half = 1 << level
    blk = 2 * half
    if blk >= 8:
        return jnp.concatenate([_rows(g, b * blk + half, blk) for b in range(CHUNK // blk)], axis=0)
    tiles = []
    for t in range(CHUNK // 8):
        acc = _rows(g, t * 8 + half, 8)
        for b in range(1, 8 // blk):
            acc = jnp.where(sub >= b * blk, _rows(g, t * 8 + b * blk + half, 8), acc)
        tiles.append(acc)
    return jnp.concatenate(tiles, axis=0)


def _hgrn_kernel(lb_ref, hn_ref, hq_ref, hf_ref, hi_ref, hg_ref, o_ref):
    seq = hq_ref.shape[0]
    head = pl.program_id(1)
    lg = lb_ref[...]
    ex = jnp.exp(lg - jnp.max(lg, axis=0, keepdims=True))
    lb = ex[0:1, :] / jnp.sum(ex, axis=0, keepdims=True)
    hn = hn_ref[pl.ds(head, 1), :]

    rowi = lax.broadcasted_iota(jnp.int32, (CHUNK, HGRN_DIM), 0)
    sub = lax.broadcasted_iota(jnp.int32, (8, HGRN_DIM), 0)
    ti = lax.broadcasted_iota(jnp.int32, (CHUNK, CHUNK), 0)
    si = lax.broadcasted_iota(jnp.int32, (CHUNK, CHUNK), 1)
    xor = ti ^ si
    lvl = jnp.full((CHUNK, CHUNK), -2, jnp.int32)
    for level in range(CHUNK_LEVELS):
        lvl = jnp.where(xor >= (1 << level), level, lvl)
    lvl = jnp.where(ti > si, lvl, jnp.where(ti == si, -1, -2))

    def chunk_step(c, state_t):
        r0 = pl.multiple_of(c * CHUNK, CHUNK)
        rows = pl.ds(r0, CHUNK)
        q = _silu(hq_ref[rows, :].astype(F32))
        f = lb + (1.0 - lb) * _sigmoid(hf_ref[rows, :].astype(F32))
        k = 1.0 - f
        v = hi_ref[rows, :]
        g = jnp.log(f)
        for shift in (1, 2, 4, 8, 16, 32):
            g = g + jnp.where(rowi >= shift, pltpu.roll(g, shift, 0), 0.0)
        g_last = g[CHUNK - 1:CHUNK, :]

        inter = lax.dot_general((q * jnp.exp(g)).astype(BF16), state_t.astype(BF16), _NT,
                                preferred_element_type=F32)

        attn = jnp.where(lvl == -1, jnp.sum(q * k, axis=-1, keepdims=True), 0.0)
        for level in range(CHUNK_LEVELS):
            ref_rows = _level_reference(g, level, sub)
            upper = (rowi & (1 << level)) != 0
            dec = jnp.exp(jnp.where(upper, g - ref_rows, ref_rows - g))
            a = lax.dot_general((q * dec).astype(BF16), (k * dec).astype(BF16), _NT,
                                preferred_element_type=F32)
            attn = jnp.where(lvl == level, a, attn)
        o = inter + jnp.dot(attn.astype(BF16), v, preferred_element_type=F32)

        k_dec = (k * jnp.exp(g_last - g)).astype(BF16)
        new_state_t = jnp.exp(g_last) * state_t + lax.dot_general(
            v, k_dec, _TN, preferred_element_type=F32)

        ms = jnp.mean(o * o, axis=-1, keepdims=True)
        y = o * lax.rsqrt(ms + NORM_EPS) * hn * _silu(hg_ref[rows, :].astype(F32))
        o_ref[rows, :] = y.astype(BF16)
        return new_state_t

    lax.fori_loop(0, seq // CHUNK, chunk_step, jnp.zeros((HGRN_DIM, HGRN_DIM), F32))


def _hgrn(proj, lb_logits, hgrn_norm, batch, seq):
    col = lambda off: (lambda b, h: (b, off // HGRN_DIM + h))
    blk = lambda off: pl.BlockSpec((seq, HGRN_DIM), col(off))
    return pl.pallas_call(
        _hgrn_kernel,
        grid=(batch, HGRN_HEADS),
        in_specs=[
            pl.BlockSpec((lb_logits.shape[0], HGRN_DIM), lambda b, h: (0, h)),
            pl.BlockSpec((HGRN_HEADS, HGRN_DIM), lambda b, h: (0, 0)),
            blk(OFF_HQ), blk(OFF_HF), blk(OFF_HI), blk(OFF_HG),
        ],
        out_specs=pl.BlockSpec((seq, HGRN_DIM), lambda b, h: (b, h)),
        out_shape=jax.ShapeDtypeStruct((batch * seq, HGRN_WIDTH), BF16),
        compiler_params=pltpu.CompilerParams(dimension_semantics=("parallel", "parallel")),
        name="hgrn2",
    )(lb_logits, hgrn_norm, proj, proj, proj, proj)


OUT_TM = 256
OUT_TN = 512
OUT_NCHUNK = D_MODEL // OUT_TN


def _out_kernel(*refs):
    ya_ref, yh_ref = refs[0], refs[1]
    ga_refs = refs[2:2 + OUT_NCHUNK]
    gh_refs = refs[2 + OUT_NCHUNK:2 + 2 * OUT_NCHUNK]
    x_ref, wa_ref, wh_ref, wo_ref, gain_ref, o_ref, m_ref = refs[2 + 2 * OUT_NCHUNK:]
    ya = ya_ref[...]
    yh = yh_ref[...]
    for j in range(OUT_NCHUNK):
        cs = slice(j * OUT_TN, (j + 1) * OUT_TN)
        ua = jnp.dot(ya, wa_ref[:, cs], preferred_element_type=F32)
        uh = jnp.dot(yh, wh_ref[:, cs], preferred_element_type=F32)
        merged = (_sigmoid(ga_refs[j][...].astype(F32)) * ua
                  + _sigmoid(gh_refs[j][...].astype(F32)) * uh)
        m_ref[:, cs] = merged.astype(BF16)
    y = jnp.dot(m_ref[...], wo_ref[...], preferred_element_type=F32)
    ms = jnp.mean(y * y, axis=-1, keepdims=True)
    o_ref[...] = x_ref[...] + y * lax.rsqrt(ms + NORM_EPS) * gain_ref[...]


def _merge_out(ya, yh, proj, x2d, wa, wh, wo, gain):
    tokens = x2d.shape[0]
    row = lambda i: (i, 0)
    const = lambda i: (0, 0)
    gate = lambda off, j: pl.BlockSpec((OUT_TM, OUT_TN), lambda i: (i, off // OUT_TN + j))
    resident = functools.partial(pl.BlockSpec, index_map=const, pipeline_mode=pl.Buffered(1))
    return pl.pallas_call(
        _out_kernel,
        grid=(tokens // OUT_TM,),
        in_specs=(
            [pl.BlockSpec((OUT_TM, ATTN_WIDTH), row), pl.BlockSpec((OUT_TM, HGRN_WIDTH), row)]
            + [gate(OFF_GA, j) for j in range(OUT_NCHUNK)]
            + [gate(OFF_GH, j) for j in range(OUT_NCHUNK)]
            + [pl.BlockSpec((OUT_TM, D_MODEL), row),
               resident((ATTN_WIDTH, D_MODEL)), resident((HGRN_WIDTH, D_MODEL)),
               resident((D_MODEL, D_MODEL)), pl.BlockSpec((1, D_MODEL), const)]),
        out_specs=pl.BlockSpec((OUT_TM, D_MODEL), row),
        out_shape=jax.ShapeDtypeStruct((tokens, D_MODEL), F32),
        scratch_shapes=[pltpu.VMEM((OUT_TM, D_MODEL), BF16)],
        compiler_params=pltpu.CompilerParams(
            dimension_semantics=("parallel",), vmem_limit_bytes=48 * 1024 * 1024),
        name="merge_out",
    )(ya, yh, *([proj] * (2 * OUT_NCHUNK)), x2d, wa, wh, wo, gain)


def kernel(x, norm_pre, w_in, rel_bias, attn_sinks, lb_logits, hgrn_norm, w_branch_attn,
           w_branch_hgrn, w_out, norm_post):
    batch, seq, d_model = x.shape
    depth = w_in.shape[0]
    assert depth == 1 and d_model == D_MODEL and seq % IN_TM == 0
    assert lb_logits.shape == (depth + 1, HGRN_WIDTH)
    x2d = x.reshape(batch * seq, d_model)
    layer = 0
    proj = _inproj(x2d, norm_pre[layer][None, :], w_in[layer].astype(BF16))
    ya = _attention(proj, rel_bias, attn_sinks[layer][None, :], batch, seq)
    yh = _hgrn(proj, lb_logits, hgrn_norm[layer], batch, seq)
    out = _merge_out(ya, yh, proj, x2d, w_branch_attn[layer].astype(BF16),
                     w_branch_hgrn[layer].astype(BF16), w_out[layer].astype(BF16),
                     norm_post[layer][None, :])
    return out.reshape(batch, seq, d_model)
```

```python
import functools
import math

import jax
import jax.numpy as jnp
from jax import lax
from jax.experimental import pallas as pl
from jax.experimental.pallas import tpu as pltpu

D_MODEL = 2048
ATTN_HEADS = 16
ATTN_KV_HEADS = 4
ATTN_HEAD_DIM = 64
ATTN_GROUP = ATTN_HEADS // ATTN_KV_HEADS
WINDOW = 128
ATTN_WIDTH = ATTN_HEADS * ATTN_HEAD_DIM
KV_WIDTH = ATTN_KV_HEADS * ATTN_HEAD_DIM
HGRN_HEADS = 8
HGRN_DIM = 128
HGRN_WIDTH = HGRN_HEADS * HGRN_DIM
CHUNK = 64
CHUNK_LEVELS = 6
HGRN_UNROLL = 4
REL_BUCKETS = 32
REL_MAX_DIST = 128
NORM_EPS = 1e-6
IN_PROJ_WIDTH = 10752

OFF_AQ = 0
OFF_AK = OFF_AQ + ATTN_WIDTH
OFF_AV = OFF_AK + KV_WIDTH
OFF_AG = OFF_AV + KV_WIDTH
OFF_HQ = OFF_AG + ATTN_WIDTH
OFF_HF = OFF_HQ + HGRN_WIDTH
OFF_HI = OFF_HF + HGRN_WIDTH
OFF_HG = OFF_HI + HGRN_WIDTH
OFF_GA = OFF_HG + HGRN_WIDTH
OFF_GH = OFF_GA + D_MODEL

LANES = 128
MASK_VALUE = -1e30

F32 = jnp.float32
BF16 = jnp.bfloat16

_NT = (((1,), (1,)), ((), ()))
_TN = (((0,), (0,)), ((), ()))


def _sigmoid(x):
    return 1.0 / (1.0 + jnp.exp(-x))


def _silu(x):
    return x * _sigmoid(x)


IN_TM = 1024
IN_TN = 1536
IN_NORM_ROWS = 256


def _inproj_kernel(x_ref, gain_ref, w_ref, o_ref, h_ref):
    @pl.when(pl.program_id(1) == 0)
    def _():
        gain = gain_ref[...]
        for r in range(0, IN_TM, IN_NORM_ROWS):
            xf = x_ref[r:r + IN_NORM_ROWS, :]
            ms = jnp.mean(xf * xf, axis=-1, keepdims=True)
            h_ref[r:r + IN_NORM_ROWS, :] = (xf * lax.rsqrt(ms + NORM_EPS) * gain).astype(BF16)

    o_ref[...] = jnp.dot(h_ref[...], w_ref[...], preferred_element_type=F32).astype(BF16)


def _inproj(x2d, gain, w_bf16):
    tokens = x2d.shape[0]
    return pl.pallas_call(
        _inproj_kernel,
        grid=(tokens // IN_TM, IN_PROJ_WIDTH // IN_TN),
        in_specs=[
            pl.BlockSpec((IN_TM, D_MODEL), lambda i, j: (i, 0)),
            pl.BlockSpec((1, D_MODEL), lambda i, j: (0, 0)),
            pl.BlockSpec((D_MODEL, IN_TN), lambda i, j: (0, j)),
        ],
        out_specs=pl.BlockSpec((IN_TM, IN_TN), lambda i, j: (i, j)),
        out_shape=jax.ShapeDtypeStruct((tokens, IN_PROJ_WIDTH), BF16),
        scratch_shapes=[pltpu.VMEM((IN_TM, D_MODEL), BF16)],
        compiler_params=pltpu.CompilerParams(
            dimension_semantics=("parallel", "arbitrary"),
            vmem_limit_bytes=56 * 1024 * 1024),
        name="inproj",
    )(x2d, gain, w_bf16)


def _t5_bucket(dist):
    max_exact = REL_BUCKETS // 2
    d = jnp.maximum(dist, 0)
    df = jnp.maximum(d, 1).astype(F32)
    large = max_exact + (jnp.log(df / max_exact) / math.log(REL_MAX_DIST / max_exact)
                         * (REL_BUCKETS - max_exact)).astype(jnp.int32)
    large = jnp.minimum(large, REL_BUCKETS - 1)
    return jnp.where(d < max_exact, d, large)


def _swap_halves(x):
    half = LANES // 2
    return jnp.concatenate([x[:, half:], x[:, :half]], axis=1)


def _attn_kernel(relb_ref, sink_ref, q_ref, kc_ref, vc_ref, kp_ref, vp_ref, ga_ref, gb_ref,
                 o_ref, bias_ref):
    blk = pl.program_id(1)

    @pl.when((pl.program_id(0) == 0) & (blk == 0))
    def _():
        qi = lax.broadcasted_iota(jnp.int32, (WINDOW, 2 * WINDOW), 0)
        si = lax.broadcasted_iota(jnp.int32, (WINDOW, 2 * WINDOW), 1)
        dist = qi + WINDOW - si
        band = (dist >= 0) & (dist < WINDOW)
        bucket = _t5_bucket(dist)
        for h in range(ATTN_HEADS):
            def pick(b, acc, h=h):
                return jnp.where(bucket == b, relb_ref[b, h], acc)
            tbl = lax.fori_loop(0, REL_BUCKETS, pick, jnp.zeros((WINDOW, 2 * WINDOW), F32))
            bias_ref[h] = jnp.where(band, tbl, MASK_VALUE)

    lane = lax.broadcasted_iota(jnp.int32, (1, LANES), 1)
    is_lo = lane < LANES // 2
    scale = ATTN_HEAD_DIM ** -0.5
    keep_lo = jnp.where(is_lo, 1.0, 0.0).astype(BF16)
    keep_hi = jnp.where(is_lo, 0.0, 1.0).astype(BF16)
    col = lax.broadcasted_iota(jnp.int32, (1, 2 * WINDOW), 1)
    prev_mask = jnp.where((blk == 0) & (col < WINDOW), MASK_VALUE, 0.0)
    row = lax.broadcasted_iota(jnp.int32, (2 * WINDOW, 1), 0)
    top = row < WINDOW

    for hk in range(ATTN_KV_HEADS):
        c, e = divmod(hk, 2)
        cs = slice(c * LANES, (c + 1) * LANES)
        k2 = jnp.concatenate([kp_ref[:, cs], kc_ref[:, cs]], axis=0)
        v2 = jnp.concatenate([vp_ref[:, cs], vc_ref[:, cs]], axis=0)
        k2s, v2s = _swap_halves(k2), _swap_halves(v2)
        keep_e, keep_o = (keep_lo, keep_hi) if e == 0 else (keep_hi, keep_lo)
        p0 = q_ref[:, hk * 2 * LANES:hk * 2 * LANES + LANES]
        p1 = q_ref[:, hk * 2 * LANES + LANES:(hk + 1) * 2 * LANES]
        q_same = jnp.concatenate([p0 * (keep_e * scale), p1 * (keep_e * scale)], axis=0)
        q_diff = jnp.concatenate([p0 * (keep_o * scale), p1 * (keep_o * scale)], axis=0)
        h_same = (4 * hk + e, 4 * hk + 2 + e)
        h_diff = (4 * hk + 1 - e, 4 * hk + 3 - e)

        def probs(qm, km, heads):
            s = lax.dot_general(qm, km, _NT, preferred_element_type=F32)
            bias = jnp.concatenate([bias_ref[heads[0]], bias_ref[heads[1]]], axis=0)
            s = s + bias + prev_mask
            sink = jnp.where(top, sink_ref[0, heads[0]], sink_ref[0, heads[1]])
            m = jnp.maximum(jnp.max(s, axis=-1, keepdims=True), sink)
            p = jnp.exp(s - m)
            denom = jnp.sum(p, axis=-1, keepdims=True) + jnp.exp(sink - m)
            return p.astype(BF16), 1.0 / denom

        p_same, r_same = probs(q_same, k2, h_same)
        p_diff, r_diff = probs(q_diff, k2s, h_diff)
        if e == 0:
            p_lo, p_hi, r_lo, r_hi = p_same, p_diff, r_same, r_diff
            v_lo, v_hi = v2 * keep_lo, v2s * keep_hi
        else:
            p_lo, p_hi, r_lo, r_hi = p_diff, p_same, r_diff, r_same
            v_lo, v_hi = v2s * keep_lo, v2 * keep_hi
        pv = jnp.dot(jnp.concatenate([p_lo, p_hi], axis=1),
                     jnp.concatenate([v_lo, v_hi], axis=0),
                     preferred_element_type=F32)
        pv = pv * jnp.where(is_lo, r_lo, r_hi)
        g_ref = ga_ref if hk < 2 else gb_ref
        for i in range(2):
            gs = slice((hk % 2) * 2 * LANES + i * LANES, (hk % 2) * 2 * LANES + (i + 1) * LANES)
            gate = _silu(g_ref[:, gs].astype(F32))
            o_ref[:, (2 * hk + i) * LANES:(2 * hk + i + 1) * LANES] = (
                pv[i * WINDOW:(i + 1) * WINDOW] * gate).astype(BF16)


def _attention(proj, rel_bias, sinks, batch, seq):
    nb = seq // WINDOW
    kcol = OFF_AK // KV_WIDTH
    vcol = OFF_AV // KV_WIDTH
    gcol = OFF_AG // (ATTN_WIDTH // 2)
    cur = lambda b, n: b * nb + n
    prev = lambda b, n: b * nb + jnp.maximum(n - 1, 0)
    smem = pl.BlockSpec(memory_space=pltpu.SMEM)
    return pl.pallas_call(
        _attn_kernel,
        grid=(batch, nb),
        in_specs=[
            smem, smem,
            pl.BlockSpec((WINDOW, ATTN_WIDTH), lambda b, n: (cur(b, n), 0)),
            pl.BlockSpec((WINDOW, KV_WIDTH), lambda b, n: (cur(b, n), kcol)),
            pl.BlockSpec((WINDOW, KV_WIDTH), lambda b, n: (cur(b, n), vcol)),
            pl.BlockSpec((WINDOW, KV_WIDTH), lambda b, n: (prev(b, n), kcol)),
            pl.BlockSpec((WINDOW, KV_WIDTH), lambda b, n: (prev(b, n), vcol)),
            pl.BlockSpec((WINDOW, ATTN_WIDTH // 2), lambda b, n: (cur(b, n), gcol)),
            pl.BlockSpec((WINDOW, ATTN_WIDTH // 2), lambda b, n: (cur(b, n), gcol + 1)),
        ],
        out_specs=pl.BlockSpec((WINDOW, ATTN_WIDTH), lambda b, n: (cur(b, n), 0)),
        out_shape=jax.ShapeDtypeStruct((batch * seq, ATTN_WIDTH), BF16),
        scratch_shapes=[pltpu.VMEM((ATTN_HEADS, WINDOW, 2 * WINDOW), F32)],
        compiler_params=pltpu.CompilerParams(dimension_semantics=("arbitrary", "arbitrary")),
        name="swa",
    )(rel_bias, sinks, proj, proj, proj, proj, proj, proj, proj)


SUBLANES = 8


def _level_reference(g_ref, level, sub):
    half = 1 << level
    blk = 2 * half

    def row_tile(r):
        return g_ref[pl.ds(r, SUBLANES, stride=0), :]

    if blk >= SUBLANES:
        return jnp.concatenate(
            [row_tile(b * blk + half) for b in range(CHUNK // blk) for _ in range(blk // SUBLANES)],
            axis=0)
    tiles = []
    for t in range(CHUNK // SUBLANES):
        acc = row_tile(t * SUBLANES + half)
        for b in range(1, SUBLANES // blk):
            acc = jnp.where(sub >= b * blk, row_tile(t * SUBLANES + b * blk + half), acc)
        tiles.append(acc)
    return jnp.concatenate(tiles, axis=0)


def _neg_abs(x):
    bits = lax.bitcast_convert_type(x, jnp.uint32) | jnp.uint32(0x80000000)
    return lax.bitcast_convert_type(bits, F32)


def _hgrn_kernel(lb_ref, hn_ref, hq_ref, hf_ref, hi_ref, hg_ref, o_ref,
                 g_ref, qg_ref, attn_ref, upd_ref, keep_ref, state_ref):
    seq = hq_ref.shape[0]
    head = pl.program_id(1)
    lg = lb_ref[...]
    ex = jnp.exp(lg - jnp.max(lg, axis=0, keepdims=True))
    lb = ex[0:1, :] / jnp.sum(ex, axis=0, keepdims=True)
    hn = hn_ref[pl.ds(head, 1), :]

    rowi = lax.broadcasted_iota(jnp.int32, (CHUNK, HGRN_DIM), 0)
    sub = lax.broadcasted_iota(jnp.int32, (SUBLANES, HGRN_DIM), 0)
    ti = lax.broadcasted_iota(jnp.int32, (CHUNK, CHUNK), 0)
    si = lax.broadcasted_iota(jnp.int32, (CHUNK, CHUNK), 1)
    xor = ti ^ si
    lvl = jnp.full((CHUNK, CHUNK), -2, jnp.int32)
    for level in range(CHUNK_LEVELS):
        lvl = jnp.where(xor >= (1 << level), level, lvl)
    lvl = jnp.where(ti > si, lvl, jnp.where(ti == si, -1, -2))

    def chunk_rows(c):
        return pl.ds(pl.multiple_of(c * CHUNK, CHUNK), CHUNK)

    def local_step(c, slot):
        rows = chunk_rows(c)
        q = _silu(hq_ref[rows, :].astype(F32))
        f = lb + (1.0 - lb) * _sigmoid(hf_ref[rows, :].astype(F32))
        k = 1.0 - f
        v = hi_ref[rows, :]
        g = jnp.log2(f)
        for shift in (1, 2, 4):
            g = g + jnp.where(rowi >= shift, pltpu.roll(g, shift, 0), 0.0)
        for shift in (8, 16, 32):
            g = g + jnp.concatenate(
                [jnp.zeros((shift, HGRN_DIM), F32), g[:CHUNK - shift]], axis=0)
        g_slot = g_ref.at[slot]
        g_slot[...] = g
        g_last = g_slot[pl.ds(CHUNK - 1, SUBLANES, stride=0), :][0:1]
        qb = q.astype(BF16)
        kb = k.astype(BF16)
        qg_ref[rows, :] = qb * jnp.exp2(g).astype(BF16)

        attn = jnp.where(lvl == -1, jnp.sum(q * k, axis=-1, keepdims=True), 0.0)
        for level in range(CHUNK_LEVELS):
            dec = jnp.exp2(_neg_abs(g - _level_reference(g_slot, level, sub))).astype(BF16)
            a = lax.dot_general(qb * dec, kb * dec, _NT, preferred_element_type=F32)
            attn = jnp.where(lvl == level, a, attn)
        attn_ref[c] = attn.astype(BF16)

        k_dec = kb * jnp.exp2(g_last - g).astype(BF16)
        upd_ref[c] = lax.dot_general(v, k_dec, _TN, preferred_element_type=F32)
        keep_ref[c] = jnp.broadcast_to(jnp.exp2(g_last), (SUBLANES, HGRN_DIM))

    def local_group(i, carry):
        for slot in range(HGRN_UNROLL):
            local_step(i * HGRN_UNROLL + slot, slot)
        return carry

    n_chunks = seq // CHUNK
    lax.fori_loop(0, n_chunks // HGRN_UNROLL, local_group, 0)

    def state_step(c, state_t):
        state_ref[c] = state_t.astype(BF16)
        return keep_ref[c][0:1] * state_t + upd_ref[c]

    lax.fori_loop(0, n_chunks, state_step, jnp.zeros((HGRN_DIM, HGRN_DIM), F32),
                  unroll=HGRN_UNROLL)

    def out_step(c, carry):
        rows = chunk_rows(c)
        o = lax.dot_general(qg_ref[rows, :], state_ref[c], _NT, preferred_element_type=F32)
        o = o + jnp.dot(attn_ref[c], hi_ref[rows, :], preferred_element_type=F32)
        ms = jnp.mean(o * o, axis=-1, keepdims=True)
        y = o * lax.rsqrt(ms + NORM_EPS) * hn * _silu(hg_ref[rows, :].astype(F32))
        o_ref[rows, :] = y.astype(BF16)
        return carry

    lax.fori_loop(0, n_chunks, out_step, 0, unroll=2 * HGRN_UNROLL)


def _hgrn(proj, lb_logits, hgrn_norm, batch, seq):
    col = lambda off: (lambda b, h: (b, off // HGRN_DIM + h))
    blk = lambda off: pl.BlockSpec((seq, HGRN_DIM), col(off))
    n_chunks = seq // CHUNK
    return pl.pallas_call(
        _hgrn_kernel,
        grid=(batch, HGRN_HEADS),
        in_specs=[
            pl.BlockSpec((lb_logits.shape[0], HGRN_DIM), lambda b, h: (0, h)),
            pl.BlockSpec((HGRN_HEADS, HGRN_DIM), lambda b, h: (0, 0)),
            blk(OFF_HQ), blk(OFF_HF), blk(OFF_HI), blk(OFF_HG),
        ],
        out_specs=pl.BlockSpec((seq, HGRN_DIM), lambda b, h: (b, h)),
        out_shape=jax.ShapeDtypeStruct((batch * seq, HGRN_WIDTH), BF16),
        scratch_shapes=[
            pltpu.VMEM((HGRN_UNROLL, CHUNK, HGRN_DIM), F32),
            pltpu.VMEM((seq, HGRN_DIM), BF16),
            pltpu.VMEM((n_chunks, CHUNK, CHUNK), BF16),
            pltpu.VMEM((n_chunks, HGRN_DIM, HGRN_DIM), F32),
            pltpu.VMEM((n_chunks, SUBLANES, HGRN_DIM), F32),
            pltpu.VMEM((n_chunks, HGRN_DIM, HGRN_DIM), BF16),
        ],
        compiler_params=pltpu.CompilerParams(dimension_semantics=("parallel", "parallel")),
        name="hgrn2",
    )(lb_logits, hgrn_norm, proj, proj, proj, proj)


OUT_TM = 256
OUT_TN = 512
OUT_NCHUNK = D_MODEL // OUT_TN


def _out_kernel(*refs):
    ya_ref, yh_ref = refs[0], refs[1]
    ga_refs = refs[2:2 + OUT_NCHUNK]
    gh_refs = refs[2 + OUT_NCHUNK:2 + 2 * OUT_NCHUNK]
    x_ref, wa_ref, wh_ref, wo_ref, gain_ref, o_ref, m_ref = refs[2 + 2 * OUT_NCHUNK:]
    ya = ya_ref[...]
    yh = yh_ref[...]
    for j in range(OUT_NCHUNK):
        cs = slice(j * OUT_TN, (j + 1) * OUT_TN)
        ua = jnp.dot(ya, wa_ref[:, cs], preferred_element_type=F32)
        uh = jnp.dot(yh, wh_ref[:, cs], preferred_element_type=F32)
        merged = (_sigmoid(ga_refs[j][...].astype(F32)) * ua
                  + _sigmoid(gh_refs[j][...].astype(F32)) * uh)
        m_ref[:, cs] = merged.astype(BF16)
    y = jnp.dot(m_ref[...], wo_ref[...], preferred_element_type=F32)
    ms = jnp.mean(y * y, axis=-1, keepdims=True)
    o_ref[...] = x_ref[...] + y * lax.rsqrt(ms + NORM_EPS) * gain_ref[...]


def _merge_out(ya, yh, proj, x2d, wa, wh, wo, gain):
    tokens = x2d.shape[0]
    row = lambda i: (i, 0)
    const = lambda i: (0, 0)
    gate = lambda off, j: pl.BlockSpec((OUT_TM, OUT_TN), lambda i: (i, off // OUT_TN + j))
    resident = functools.partial(pl.BlockSpec, index_map=const, pipeline_mode=pl.Buffered(1))
    return pl.pallas_call(
        _out_kernel,
        grid=(tokens // OUT_TM,),
        in_specs=(
            [pl.BlockSpec((OUT_TM, ATTN_WIDTH), row), pl.BlockSpec((OUT_TM, HGRN_WIDTH), row)]
            + [gate(OFF_GA, j) for j in range(OUT_NCHUNK)]
            + [gate(OFF_GH, j) for j in range(OUT_NCHUNK)]
            + [pl.BlockSpec((OUT_TM, D_MODEL), row),
               resident((ATTN_WIDTH, D_MODEL)), resident((HGRN_WIDTH, D_MODEL)),
               resident((D_MODEL, D_MODEL)), pl.BlockSpec((1, D_MODEL), const)]),
        out_specs=pl.BlockSpec((OUT_TM, D_MODEL), row),
        out_shape=jax.ShapeDtypeStruct((tokens, D_MODEL), F32),
        scratch_shapes=[pltpu.VMEM((OUT_TM, D_MODEL), BF16)],
        compiler_params=pltpu.CompilerParams(
            dimension_semantics=("parallel",), vmem_limit_bytes=48 * 1024 * 1024),
        name="merge_out",
    )(ya, yh, *([proj] * (2 * OUT_NCHUNK)), x2d, wa, wh, wo, gain)


def kernel(x, norm_pre, w_in, rel_bias, attn_sinks, lb_logits, hgrn_norm, w_branch_attn,
           w_branch_hgrn, w_out, norm_post):
    batch, seq, d_model = x.shape
    depth = w_in.shape[0]
    assert depth == 1 and d_model == D_MODEL and seq % IN_TM == 0
    assert lb_logits.shape == (depth + 1, HGRN_WIDTH)
    x2d = x.reshape(batch * seq, d_model)
    layer = 0
    proj = _inproj(x2d, norm_pre[layer][None, :], w_in[layer].astype(BF16))
    ya = _attention(proj, rel_bias, attn_sinks[layer][None, :], batch, seq)
    yh = _hgrn(proj, lb_logits, hgrn_norm[layer], batch, seq)
    out = _merge_out(ya, yh, proj, x2d, w_branch_attn[layer].astype(BF16),
                     w_branch_hgrn[layer].astype(BF16), w_out[layer].astype(BF16),
                     norm_post[layer][None, :])
    return out.reshape(batch, seq, d_model)
```

```python
import functools
import math

import jax
import jax.numpy as jnp
from jax import lax
from jax.experimental import pallas as pl
from jax.experimental.pallas import tpu as pltpu

D_MODEL = 2048
ATTN_HEADS = 16
ATTN_KV_HEADS = 4
ATTN_HEAD_DIM = 64
ATTN_GROUP = ATTN_HEADS // ATTN_KV_HEADS
WINDOW = 128
ATTN_WIDTH = ATTN_HEADS * ATTN_HEAD_DIM
KV_WIDTH = ATTN_KV_HEADS * ATTN_HEAD_DIM
HGRN_HEADS = 8
HGRN_DIM = 128
HGRN_WIDTH = HGRN_HEADS * HGRN_DIM
CHUNK = 64
CHUNK_LEVELS = 6
HGRN_UNROLL = 4
HGRN_HEADS_PER_STEP = 2
REL_BUCKETS = 32
REL_MAX_DIST = 128
NORM_EPS = 1e-6
IN_PROJ_WIDTH = 10752

OFF_AQ = 0
OFF_AK = OFF_AQ + ATTN_WIDTH
OFF_AV = OFF_AK + KV_WIDTH
OFF_AG = OFF_AV + KV_WIDTH
OFF_HQ = OFF_AG + ATTN_WIDTH
OFF_HF = OFF_HQ + HGRN_WIDTH
OFF_HI = OFF_HF + HGRN_WIDTH
OFF_HG = OFF_HI + HGRN_WIDTH
OFF_GA = OFF_HG + HGRN_WIDTH
OFF_GH = OFF_GA + D_MODEL

LANES = 128
MASK_VALUE = -1e30

F32 = jnp.float32
BF16 = jnp.bfloat16

_NT = (((1,), (1,)), ((), ()))
_TN = (((0,), (0,)), ((), ()))


def _sigmoid(x):
    return 1.0 / (1.0 + jnp.exp(-x))


def _silu(x):
    return x * _sigmoid(x)


IN_TM = 1024
IN_TN = 1536
IN_NORM_ROWS = 256


def _inproj_kernel(x_ref, gain_ref, w_ref, o_ref, h_ref):
    @pl.when(pl.program_id(1) == 0)
    def _():
        gain = gain_ref[...]
        for r in range(0, IN_TM, IN_NORM_ROWS):
            xf = x_ref[r:r + IN_NORM_ROWS, :]
            ms = jnp.mean(xf * xf, axis=-1, keepdims=True)
            h_ref[r:r + IN_NORM_ROWS, :] = (xf * lax.rsqrt(ms + NORM_EPS) * gain).astype(BF16)

    o_ref[...] = jnp.dot(h_ref[...], w_ref[...], preferred_element_type=F32).astype(BF16)


def _inproj(x2d, gain, w_bf16):
    tokens = x2d.shape[0]
    return pl.pallas_call(
        _inproj_kernel,
        grid=(tokens // IN_TM, IN_PROJ_WIDTH // IN_TN),
        in_specs=[
            pl.BlockSpec((IN_TM, D_MODEL), lambda i, j: (i, 0)),
            pl.BlockSpec((1, D_MODEL), lambda i, j: (0, 0)),
            pl.BlockSpec((D_MODEL, IN_TN), lambda i, j: (0, j)),
        ],
        out_specs=pl.BlockSpec((IN_TM, IN_TN), lambda i, j: (i, j)),
        out_shape=jax.ShapeDtypeStruct((tokens, IN_PROJ_WIDTH), BF16),
        scratch_shapes=[pltpu.VMEM((IN_TM, D_MODEL), BF16)],
        compiler_params=pltpu.CompilerParams(
            dimension_semantics=("parallel", "arbitrary"),
            vmem_limit_bytes=56 * 1024 * 1024),
        name="inproj",
    )(x2d, gain, w_bf16)


def _t5_bucket(dist):
    max_exact = REL_BUCKETS // 2
    d = jnp.maximum(dist, 0)
    df = jnp.maximum(d, 1).astype(F32)
    large = max_exact + (jnp.log(df / max_exact) / math.log(REL_MAX_DIST / max_exact)
                         * (REL_BUCKETS - max_exact)).astype(jnp.int32)
    large = jnp.minimum(large, REL_BUCKETS - 1)
    return jnp.where(d < max_exact, d, large)


def _swap_halves(x):
    half = LANES // 2
    return jnp.concatenate([x[:, half:], x[:, :half]], axis=1)


def _attn_kernel(relb_ref, sink_ref, q_ref, kc_ref, vc_ref, kp_ref, vp_ref, ga_ref, gb_ref,
                 o_ref, bias_ref, cur_ref):
    blk = pl.program_id(1)

    @pl.when((pl.program_id(0) == 0) & (blk == 0))
    def _():
        qi = lax.broadcasted_iota(jnp.int32, (WINDOW, WINDOW), 0)
        sj = lax.broadcasted_iota(jnp.int32, (WINDOW, WINDOW), 1)
        own = sj <= qi
        bucket = _t5_bucket(jnp.where(own, qi - sj, qi + WINDOW - sj))
        for h in range(ATTN_HEADS):
            def pick(b, acc, h=h):
                return jnp.where(bucket == b, relb_ref[b, h], acc)
            tbl = lax.fori_loop(0, REL_BUCKETS, pick, jnp.zeros((WINDOW, WINDOW), F32))
            bias_ref[0, h] = tbl
            bias_ref[1, h] = jnp.where(own, tbl, MASK_VALUE)
        own_bf = jnp.where(own, 1.0, 0.0).astype(BF16)
        cur_ref[...] = jnp.concatenate([own_bf, own_bf], axis=0)

    lane = lax.broadcasted_iota(jnp.int32, (1, LANES), 1)
    is_lo = lane < LANES // 2
    scale = ATTN_HEAD_DIM ** -0.5
    keep_lo = jnp.where(is_lo, 1.0, 0.0).astype(BF16)
    keep_hi = jnp.where(is_lo, 0.0, 1.0).astype(BF16)
    first = jnp.where(blk == 0, 1, 0)
    row = lax.broadcasted_iota(jnp.int32, (2 * WINDOW, 1), 0)
    top = row < WINDOW
    own2 = (lax.broadcasted_iota(jnp.int32, (2 * WINDOW, WINDOW), 1)
            <= (lax.broadcasted_iota(jnp.int32, (2 * WINDOW, WINDOW), 0) & (WINDOW - 1)))

    def scores(qm, km, heads):
        s2 = lax.dot_general(qm, km, _NT, preferred_element_type=F32)
        bias = jnp.concatenate([bias_ref[first, heads[0]], bias_ref[first, heads[1]]], axis=0)
        return jnp.where(own2, s2[:, WINDOW:], s2[:, :WINDOW]) + bias

    def probs(s, heads):
        sink = jnp.where(top, sink_ref[0, heads[0]], sink_ref[0, heads[1]])
        m = jnp.maximum(jnp.max(s, axis=-1, keepdims=True), sink)
        p = jnp.exp(s - m)
        denom = jnp.sum(p, axis=-1, keepdims=True) + jnp.exp(sink - m)
        pb = p.astype(BF16)
        p_own = pb * cur_ref[...]
        return jnp.concatenate([pb - p_own, p_own], axis=1), 1.0 / denom

    s_same, s_diff, vals = [], [], []
    for hk in range(ATTN_KV_HEADS):
        c, e = divmod(hk, 2)
        cs = slice(c * LANES, (c + 1) * LANES)
        k2 = jnp.concatenate([kp_ref[:, cs], kc_ref[:, cs]], axis=0)
        v2 = jnp.concatenate([vp_ref[:, cs], vc_ref[:, cs]], axis=0)
        k2s, v2s = _swap_halves(k2), _swap_halves(v2)
        keep_e, keep_o = (keep_lo, keep_hi) if e == 0 else (keep_hi, keep_lo)
        p0 = q_ref[:, hk * 2 * LANES:hk * 2 * LANES + LANES]
        p1 = q_ref[:, hk * 2 * LANES + LANES:(hk + 1) * 2 * LANES]
        q_same = jnp.concatenate([p0 * (keep_e * scale), p1 * (keep_e * scale)], axis=0)
        q_diff = jnp.concatenate([p0 * (keep_o * scale), p1 * (keep_o * scale)], axis=0)
        s_same.append(scores(q_same, k2, (4 * hk + e, 4 * hk + 2 + e)))
        s_diff.append(scores(q_diff, k2s, (4 * hk + 1 - e, 4 * hk + 3 - e)))
        vals.append((v2 * keep_lo, v2s * keep_hi) if e == 0 else (v2s * keep_lo, v2 * keep_hi))

    pr_same = [probs(s_same[hk], (4 * hk + hk % 2, 4 * hk + 2 + hk % 2))
               for hk in range(ATTN_KV_HEADS)]
    pr_diff = [probs(s_diff[hk], (4 * hk + 1 - hk % 2, 4 * hk + 3 - hk % 2))
               for hk in range(ATTN_KV_HEADS)]

    for hk in range(ATTN_KV_HEADS):
        (p_lo, r_lo), (p_hi, r_hi) = ((pr_same[hk], pr_diff[hk]) if hk % 2 == 0
                                      else (pr_diff[hk], pr_same[hk]))
        v_lo, v_hi = vals[hk]
        pv = jnp.dot(jnp.concatenate([p_lo, p_hi], axis=1),
                     jnp.concatenate([v_lo, v_hi], axis=0),
                     preferred_element_type=F32)
        pv = pv * jnp.where(is_lo, r_lo, r_hi)
        g_ref = ga_ref if hk < 2 else gb_ref
        for i in range(2):
            gs = slice((hk % 2) * 2 * LANES + i * LANES, (hk % 2) * 2 * LANES + (i + 1) * LANES)
            gate = _silu(g_ref[:, gs].astype(F32))
            o_ref[:, (2 * hk + i) * LANES:(2 * hk + i + 1) * LANES] = (
                pv[i * WINDOW:(i + 1) * WINDOW] * gate).astype(BF16)


def _attention(proj, rel_bias, sinks, batch, seq):
    nb = seq // WINDOW
    kcol = OFF_AK // KV_WIDTH
    vcol = OFF_AV // KV_WIDTH
    gcol = OFF_AG // (ATTN_WIDTH // 2)
    cur = lambda b, n: b * nb + n
    prev = lambda b, n: b * nb + jnp.maximum(n - 1, 0)
    smem = pl.BlockSpec(memory_space=pltpu.SMEM)
    return pl.pallas_call(
        _attn_kernel,
        grid=(batch, nb),
        in_specs=[
            smem, smem,
            pl.BlockSpec((WINDOW, ATTN_WIDTH), lambda b, n: (cur(b, n), 0)),
            pl.BlockSpec((WINDOW, KV_WIDTH), lambda b, n: (cur(b, n), kcol)),
            pl.BlockSpec((WINDOW, KV_WIDTH), lambda b, n: (cur(b, n), vcol)),
            pl.BlockSpec((WINDOW, KV_WIDTH), lambda b, n: (prev(b, n), kcol)),
            pl.BlockSpec((WINDOW, KV_WIDTH), lambda b, n: (prev(b, n), vcol)),
            pl.BlockSpec((WINDOW, ATTN_WIDTH // 2), lambda b, n: (cur(b, n), gcol)),
            pl.BlockSpec((WINDOW, ATTN_WIDTH // 2), lambda b, n: (cur(b, n), gcol + 1)),
        ],
        out_specs=pl.BlockSpec((WINDOW, ATTN_WIDTH), lambda b, n: (cur(b, n), 0)),
        out_shape=jax.ShapeDtypeStruct((batch * seq, ATTN_WIDTH), BF16),
        scratch_shapes=[
            pltpu.VMEM((2, ATTN_HEADS, WINDOW, WINDOW), F32),
            pltpu.VMEM((2 * WINDOW, WINDOW), BF16),
        ],
        compiler_params=pltpu.CompilerParams(dimension_semantics=("arbitrary", "arbitrary")),
        name="swa",
    )(rel_bias, sinks, proj, proj, proj, proj, proj, proj, proj)


SUBLANES = 8


def _level_reference(g_ref, level, sub):
    half = 1 << level
    blk = 2 * half

    def row_tile(r):
        return g_ref[pl.ds(r, SUBLANES, stride=0), :]

    if blk >= SUBLANES:
        return jnp.concatenate(
            [row_tile(b * blk + half) for b in range(CHUNK // blk) for _ in range(blk // SUBLANES)],
            axis=0)
    tiles = []
    for t in range(CHUNK // SUBLANES):
        acc = row_tile(t * SUBLANES + half)
        for b in range(1, SUBLANES // blk):
            acc = jnp.where(sub >= b * blk, row_tile(t * SUBLANES + b * blk + half), acc)
        tiles.append(acc)
    return jnp.concatenate(tiles, axis=0)


def _neg_abs(x):
    bits = lax.bitcast_convert_type(x, jnp.uint32) | jnp.uint32(0x80000000)
    return lax.bitcast_convert_type(bits, F32)


def _hgrn_kernel(lb_ref, hn_ref, hq_ref, hf_ref, hi_ref, hg_ref, o_ref, *scratch):
    for j in range(HGRN_HEADS_PER_STEP):
        cols = slice(j * HGRN_DIM, (j + 1) * HGRN_DIM)
        head = pl.program_id(1) * HGRN_HEADS_PER_STEP + j
        _hgrn_head(lb_ref.at[:, cols], hn_ref.at[pl.ds(head, 1), :], hq_ref.at[:, cols],
                   hf_ref.at[:, cols], hi_ref.at[:, cols], hg_ref.at[:, cols], o_ref.at[:, cols],
                   *scratch)


def _hgrn_head(lb_ref, hn_ref, hq_ref, hf_ref, hi_ref, hg_ref, o_ref,
               g_ref, qg_ref, attn_ref, upd_ref, keep_ref, state_ref):
    seq = hq_ref.shape[0]
    lg = lb_ref[...]
    ex = jnp.exp(lg - jnp.max(lg, axis=0, keepdims=True))
    lb = ex[0:1, :] / jnp.sum(ex, axis=0, keepdims=True)
    hn = hn_ref[...]

    rowi = lax.broadcasted_iota(jnp.int32, (CHUNK, HGRN_DIM), 0)
    sub = lax.broadcasted_iota(jnp.int32, (SUBLANES, HGRN_DIM), 0)
    ti = lax.broadcasted_iota(jnp.int32, (CHUNK, CHUNK), 0)
    si = lax.broadcasted_iota(jnp.int32, (CHUNK, CHUNK), 1)
    xor = ti ^ si
    lvl = jnp.full((CHUNK, CHUNK), -2, jnp.int32)
    for level in range(CHUNK_LEVELS):
        lvl = jnp.where(xor >= (1 << level), level, lvl)
    lvl = jnp.where(ti > si, lvl, jnp.where(ti == si, -1, -2))

    def chunk_rows(c):
        return pl.ds(pl.multiple_of(c * CHUNK, CHUNK), CHUNK)

    def local_step(c, slot):
        rows = chunk_rows(c)
        q = _silu(hq_ref[rows, :].astype(F32))
        f = lb + (1.0 - lb) * _sigmoid(hf_ref[rows, :].astype(F32))
        k = 1.0 - f
        v = hi_ref[rows, :]
        g = jnp.log2(f)
        for shift in (1, 2, 4):
            g = g + jnp.where(rowi >= shift, pltpu.roll(g, shift, 0), 0.0)
        for shift in (8, 16, 32):
            g = g + jnp.concatenate(
                [jnp.zeros((shift, HGRN_DIM), F32), g[:CHUNK - shift]], axis=0)
        g_slot = g_ref.at[slot]
        g_slot[...] = g
        g_last = g_slot[pl.ds(CHUNK - 1, SUBLANES, stride=0), :][0:1]
        qb = q.astype(BF16)
        kb = k.astype(BF16)
        qg_ref[rows, :] = qb * jnp.exp2(g).astype(BF16)

        attn = jnp.where(lvl == -1, jnp.sum(q * k, axis=-1, keepdims=True), 0.0)
        for level in range(CHUNK_LEVELS):
            dec = jnp.exp2(_neg_abs(g - _level_reference(g_slot, level, sub))).astype(BF16)
            a = lax.dot_general(qb * dec, kb * dec, _NT, preferred_element_type=F32)
            attn = jnp.where(lvl == level, a, attn)
        attn_ref[c] = attn.astype(BF16)

        k_dec = kb * jnp.exp2(g_last - g).astype(BF16)
        upd_ref[c] = lax.dot_general(v, k_dec, _TN, preferred_element_type=F32)
        keep_ref[c] = jnp.broadcast_to(jnp.exp2(g_last), (SUBLANES, HGRN_DIM))

    def local_group(i, carry):
        for slot in range(HGRN_UNROLL):
            local_step(i * HGRN_UNROLL + slot, slot)
        return carry

    n_chunks = seq // CHUNK
    lax.fori_loop(0, n_chunks // HGRN_UNROLL, local_group, 0)

    def state_step(c, state_t):
        state_ref[c] = state_t.astype(BF16)
        return keep_ref[c][0:1] * state_t + upd_ref[c]

    lax.fori_loop(0, n_chunks, state_step, jnp.zeros((HGRN_DIM, HGRN_DIM), F32),
                  unroll=HGRN_UNROLL)

    def out_step(c, carry):
        rows = chunk_rows(c)
        o = lax.dot_general(qg_ref[rows, :], state_ref[c], _NT, preferred_element_type=F32)
        o = o + jnp.dot(attn_ref[c], hi_ref[rows, :], preferred_element_type=F32)
        ms = jnp.mean(o * o, axis=-1, keepdims=True)
        y = o * lax.rsqrt(ms + NORM_EPS) * hn * _silu(hg_ref[rows, :].astype(F32))
        o_ref[rows, :] = y.astype(BF16)
        return carry

    lax.fori_loop(0, n_chunks, out_step, 0, unroll=2 * HGRN_UNROLL)


def _hgrn(proj, lb_logits, hgrn_norm, batch, seq):
    width = HGRN_HEADS_PER_STEP * HGRN_DIM
    col = lambda off: (lambda b, h: (b, off // width + h))
    blk = lambda off: pl.BlockSpec((seq, width), col(off))
    n_chunks = seq // CHUNK
    return pl.pallas_call(
        _hgrn_kernel,
        grid=(batch, HGRN_HEADS // HGRN_HEADS_PER_STEP),
        in_specs=[
            pl.BlockSpec((lb_logits.shape[0], width), lambda b, h: (0, h)),
            pl.BlockSpec((HGRN_HEADS, HGRN_DIM), lambda b, h: (0, 0)),
            blk(OFF_HQ), blk(OFF_HF), blk(OFF_HI), blk(OFF_HG),
        ],
        out_specs=pl.BlockSpec((seq, width), lambda b, h: (b, h)),
        out_shape=jax.ShapeDtypeStruct((batch * seq, HGRN_WIDTH), BF16),
        scratch_shapes=[
            pltpu.VMEM((HGRN_UNROLL, CHUNK, HGRN_DIM), F32),
            pltpu.VMEM((seq, HGRN_DIM), BF16),
            pltpu.VMEM((n_chunks, CHUNK, CHUNK), BF16),
            pltpu.VMEM((n_chunks, HGRN_DIM, HGRN_DIM), F32),
            pltpu.VMEM((n_chunks, SUBLANES, HGRN_DIM), F32),
            pltpu.VMEM((n_chunks, HGRN_DIM, HGRN_DIM), BF16),
        ],
        compiler_params=pltpu.CompilerParams(dimension_semantics=("parallel", "parallel")),
        name="hgrn2",
    )(lb_logits, hgrn_norm, proj, proj, proj, proj)


OUT_TM = 256
OUT_TN = 512
OUT_NCHUNK = D_MODEL // OUT_TN


def _out_kernel(*refs):
    ya_ref, yh_ref = refs[0], refs[1]
    ga_refs = refs[2:2 + OUT_NCHUNK]
    gh_refs = refs[2 + OUT_NCHUNK:2 + 2 * OUT_NCHUNK]
    x_ref, wa_ref, wh_ref, wo_ref, gain_ref, o_ref, m_ref = refs[2 + 2 * OUT_NCHUNK:]
    ya = ya_ref[...]
    yh = yh_ref[...]
    for j in range(OUT_NCHUNK):
        cs = slice(j * OUT_TN, (j + 1) * OUT_TN)
        ua = jnp.dot(ya, wa_ref[:, cs], preferred_element_type=F32)
        uh = jnp.dot(yh, wh_ref[:, cs], preferred_element_type=F32)
        merged = (_sigmoid(ga_refs[j][...].astype(F32)) * ua
                  + _sigmoid(gh_refs[j][...].astype(F32)) * uh)
        m_ref[:, cs] = merged.astype(BF16)
    y = jnp.dot(m_ref[...], wo_ref[...], preferred_element_type=F32)
    ms = jnp.mean(y * y, axis=-1, keepdims=True)
    o_ref[...] = x_ref[...] + y * lax.rsqrt(ms + NORM_EPS) * gain_ref[...]


def _merge_out(ya, yh, proj, x2d, wa, wh, wo, gain):
    tokens = x2d.shape[0]
    row = lambda i: (i, 0)
    const = lambda i: (0, 0)
    gate = lambda off, j: pl.BlockSpec((OUT_TM, OUT_TN), lambda i: (i, off // OUT_TN + j))
    resident = functools.partial(pl.BlockSpec, index_map=const, pipeline_mode=pl.Buffered(1))
    return pl.pallas_call(
        _out_kernel,
        grid=(tokens // OUT_TM,),
        in_specs=(
            [pl.BlockSpec((OUT_TM, ATTN_WIDTH), row), pl.BlockSpec((OUT_TM, HGRN_WIDTH), row)]
            + [gate(OFF_GA, j) for j in range(OUT_NCHUNK)]
            + [gate(OFF_GH, j) for j in range(OUT_NCHUNK)]
            + [pl.BlockSpec((OUT_TM, D_MODEL), row),
               resident((ATTN_WIDTH, D_MODEL)), resident((HGRN_WIDTH, D_MODEL)),
               resident((D_MODEL, D_MODEL)), pl.BlockSpec((1, D_MODEL), const)]),
        out_specs=pl.BlockSpec((OUT_TM, D_MODEL), row),
        out_shape=jax.ShapeDtypeStruct((tokens, D_MODEL), F32),
        scratch_shapes=[pltpu.VMEM((OUT_TM, D_MODEL), BF16)],
        compiler_params=pltpu.CompilerParams(
            dimension_semantics=("parallel",), vmem_limit_bytes=48 * 1024 * 1024),
        name="merge_out",
    )(ya, yh, *([proj] * (2 * OUT_NCHUNK)), x2d, wa, wh, wo, gain)


def kernel(x, norm_pre, w_in, rel_bias, attn_sinks, lb_logits, hgrn_norm, w_branch_attn,
           w_branch_hgrn, w_out, norm_post):
    batch, seq, d_model = x.shape
    depth = w_in.shape[0]
    assert depth == 1 and d_model == D_MODEL and seq % IN_TM == 0
    assert lb_logits.shape == (depth + 1, HGRN_WIDTH)
    x2d = x.reshape(batch * seq, d_model)
    layer = 0
    proj = _inproj(x2d, norm_pre[layer][None, :], w_in[layer].astype(BF16))
    ya = _attention(proj, rel_bias, attn_sinks[layer][None, :], batch, seq)
    yh = _hgrn(proj, lb_logits, hgrn_norm[layer], batch, seq)
    out = _merge_out(ya, yh, proj, x2d, w_branch_attn[layer].astype(BF16),
                     w_branch_hgrn[layer].astype(BF16), w_out[layer].astype(BF16),
                     norm_post[layer][None, :])
    return out.reshape(batch, seq, d_model)
```

```python
import functools
import math

import jax
import jax.numpy as jnp
from jax import lax
from jax.experimental import pallas as pl
from jax.experimental.pallas import tpu as pltpu

D_MODEL = 2048
ATTN_HEADS = 16
ATTN_KV_HEADS = 4
ATTN_HEAD_DIM = 64
ATTN_GROUP = ATTN_HEADS // ATTN_KV_HEADS
WINDOW = 128
ATTN_WIDTH = ATTN_HEADS * ATTN_HEAD_DIM
KV_WIDTH = ATTN_KV_HEADS * ATTN_HEAD_DIM
HGRN_HEADS = 8
HGRN_DIM = 128
HGRN_WIDTH = HGRN_HEADS * HGRN_DIM
CHUNK = 64
CHUNK_LEVELS = 6
HGRN_UNROLL = 4
HGRN_HEADS_PER_STEP = 2
REL_BUCKETS = 32
REL_MAX_DIST = 128
NORM_EPS = 1e-6
IN_PROJ_WIDTH = 10752

OFF_AQ = 0
OFF_AK = OFF_AQ + ATTN_WIDTH
OFF_AV = OFF_AK + KV_WIDTH
OFF_AG = OFF_AV + KV_WIDTH
OFF_HQ = OFF_AG + ATTN_WIDTH
OFF_HF = OFF_HQ + HGRN_WIDTH
OFF_HI = OFF_HF + HGRN_WIDTH
OFF_HG = OFF_HI + HGRN_WIDTH
OFF_GA = OFF_HG + HGRN_WIDTH
OFF_GH = OFF_GA + D_MODEL

LANES = 128
SUBLANES = 8
MASK_VALUE = -1e30

F32 = jnp.float32
BF16 = jnp.bfloat16

_NT = (((1,), (1,)), ((), ()))
_TN = (((0,), (0,)), ((), ()))


def _sigmoid(x):
    return 1.0 / (1.0 + jnp.exp(-x))


def _silu(x):
    return x * _sigmoid(x)


IN_TM = 1024
IN_TN = 1536
IN_NORM_ROWS = 256


def _inproj_kernel(x_ref, gain_ref, w_ref, o_ref, h_ref):
    @pl.when(pl.program_id(1) == 0)
    def _():
        gain = gain_ref[...]
        for r in range(0, IN_TM, IN_NORM_ROWS):
            xf = x_ref[r:r + IN_NORM_ROWS, :]
            ms = jnp.mean(xf * xf, axis=-1, keepdims=True)
            h_ref[r:r + IN_NORM_ROWS, :] = (xf * lax.rsqrt(ms + NORM_EPS) * gain).astype(BF16)

    o_ref[...] = jnp.dot(h_ref[...], w_ref[...], preferred_element_type=F32).astype(BF16)


def _inproj(x2d, gain, w_bf16):
    tokens = x2d.shape[0]
    return pl.pallas_call(
        _inproj_kernel,
        grid=(tokens // IN_TM, IN_PROJ_WIDTH // IN_TN),
        in_specs=[
            pl.BlockSpec((IN_TM, D_MODEL), lambda i, j: (i, 0)),
            pl.BlockSpec((1, D_MODEL), lambda i, j: (0, 0)),
            pl.BlockSpec((D_MODEL, IN_TN), lambda i, j: (0, j)),
        ],
        out_specs=pl.BlockSpec((IN_TM, IN_TN), lambda i, j: (i, j)),
        out_shape=jax.ShapeDtypeStruct((tokens, IN_PROJ_WIDTH), BF16),
        scratch_shapes=[pltpu.VMEM((IN_TM, D_MODEL), BF16)],
        compiler_params=pltpu.CompilerParams(
            dimension_semantics=("parallel", "arbitrary"),
            vmem_limit_bytes=56 * 1024 * 1024),
        name="inproj",
    )(x2d, gain, w_bf16)


def _level_reference(g_ref, level, sub):
    half = 1 << level
    blk = 2 * half

    def row_tile(r):
        return g_ref[pl.ds(r, SUBLANES, stride=0), :]

    if blk >= SUBLANES:
        return jnp.concatenate(
            [row_tile(b * blk + half) for b in range(CHUNK // blk) for _ in range(blk // SUBLANES)],
            axis=0)
    tiles = []
    for t in range(CHUNK // SUBLANES):
        acc = row_tile(t * SUBLANES + half)
        for b in range(1, SUBLANES // blk):
            acc = jnp.where(sub >= b * blk, row_tile(t * SUBLANES + b * blk + half), acc)
        tiles.append(acc)
    return jnp.concatenate(tiles, axis=0)


def _neg_abs(x):
    bits = lax.bitcast_convert_type(x, jnp.uint32) | jnp.uint32(0x80000000)
    return lax.bitcast_convert_type(bits, F32)


def _hgrn_kernel(lb_ref, hn_ref, hq_ref, hf_ref, hi_ref, hg_ref, o_ref, *scratch):
    for j in range(HGRN_HEADS_PER_STEP):
        cols = slice(j * HGRN_DIM, (j + 1) * HGRN_DIM)
        head = pl.program_id(1) * HGRN_HEADS_PER_STEP + j
        _hgrn_head(lb_ref.at[:, cols], hn_ref.at[pl.ds(head, 1), :], hq_ref.at[:, cols],
                   hf_ref.at[:, cols], hi_ref.at[:, cols], hg_ref.at[:, cols], o_ref.at[:, cols],
                   *scratch)


def _hgrn_head(lb_ref, hn_ref, hq_ref, hf_ref, hi_ref, hg_ref, o_ref,
               g_ref, qg_ref, attn_ref, upd_ref, keep_ref, state_ref):
    seq = hq_ref.shape[0]
    lg = lb_ref[...]
    ex = jnp.exp(lg - jnp.max(lg, axis=0, keepdims=True))
    lb = ex[0:1, :] / jnp.sum(ex, axis=0, keepdims=True)
    hn = hn_ref[...]

    rowi = lax.broadcasted_iota(jnp.int32, (CHUNK, HGRN_DIM), 0)
    sub = lax.broadcasted_iota(jnp.int32, (SUBLANES, HGRN_DIM), 0)
    ti = lax.broadcasted_iota(jnp.int32, (CHUNK, CHUNK), 0)
    si = lax.broadcasted_iota(jnp.int32, (CHUNK, CHUNK), 1)
    xor = ti ^ si
    lvl = jnp.full((CHUNK, CHUNK), -2, jnp.int32)
    for level in range(CHUNK_LEVELS):
        lvl = jnp.where(xor >= (1 << level), level, lvl)
    lvl = jnp.where(ti > si, lvl, jnp.where(ti == si, -1, -2))

    def chunk_rows(c):
        return pl.ds(pl.multiple_of(c * CHUNK, CHUNK), CHUNK)

    def local_step(c, slot):
        rows = chunk_rows(c)
        q = _silu(hq_ref[rows, :].astype(F32))
        f = lb + (1.0 - lb) * _sigmoid(hf_ref[rows, :].astype(F32))
        k = 1.0 - f
        v = hi_ref[rows, :]
        g = jnp.log2(f)
        for shift in (1, 2, 4):
            g = g + jnp.where(rowi >= shift, pltpu.roll(g, shift, 0), 0.0)
        for shift in (8, 16, 32):
            g = g + jnp.concatenate(
                [jnp.zeros((shift, HGRN_DIM), F32), g[:CHUNK - shift]], axis=0)
        g_slot = g_ref.at[slot]
        g_slot[...] = g
        g_last = g_slot[pl.ds(CHUNK - 1, SUBLANES, stride=0), :][0:1]
        qb = q.astype(BF16)
        kb = k.astype(BF16)
        qg_ref[rows, :] = qb * jnp.exp2(g).astype(BF16)

        attn = jnp.where(lvl == -1, jnp.sum(q * k, axis=-1, keepdims=True), 0.0)
        for level in range(CHUNK_LEVELS):
            dec = jnp.exp2(_neg_abs(g - _level_reference(g_slot, level, sub))).astype(BF16)
            a = lax.dot_general(qb * dec, kb * dec, _NT, preferred_element_type=F32)
            attn = jnp.where(lvl == level, a, attn)
        attn_ref[c] = attn.astype(BF16)

        k_dec = kb * jnp.exp2(g_last - g).astype(BF16)
        upd_ref[c] = lax.dot_general(v, k_dec, _TN, preferred_element_type=F32)
        keep_ref[c] = jnp.broadcast_to(jnp.exp2(g_last), (SUBLANES, HGRN_DIM))

    def local_group(i, carry):
        for slot in range(HGRN_UNROLL):
            local_step(i * HGRN_UNROLL + slot, slot)
        return carry

    n_chunks = seq // CHUNK
    lax.fori_loop(0, n_chunks // HGRN_UNROLL, local_group, 0)

    def state_step(c, state_t):
        state_ref[c] = state_t.astype(BF16)
        return keep_ref[c][0:1] * state_t + upd_ref[c]

    lax.fori_loop(0, n_chunks, state_step, jnp.zeros((HGRN_DIM, HGRN_DIM), F32),
                  unroll=HGRN_UNROLL)

    def out_step(c, carry):
        rows = chunk_rows(c)
        o = lax.dot_general(qg_ref[rows, :], state_ref[c], _NT, preferred_element_type=F32)
        o = o + jnp.dot(attn_ref[c], hi_ref[rows, :], preferred_element_type=F32)
        ms = jnp.mean(o * o, axis=-1, keepdims=True)
        y = o * lax.rsqrt(ms + NORM_EPS) * hn * _silu(hg_ref[rows, :].astype(F32))
        o_ref[rows, :] = y.astype(BF16)
        return carry

    lax.fori_loop(0, n_chunks, out_step, 0, unroll=2 * HGRN_UNROLL)


def _hgrn(proj, lb_logits, hgrn_norm, batch, seq):
    width = HGRN_HEADS_PER_STEP * HGRN_DIM
    col = lambda off: (lambda b, h: (b, off // width + h))
    blk = lambda off: pl.BlockSpec((seq, width), col(off))
    n_chunks = seq // CHUNK
    return pl.pallas_call(
        _hgrn_kernel,
        grid=(batch, HGRN_HEADS // HGRN_HEADS_PER_STEP),
        in_specs=[
            pl.BlockSpec((lb_logits.shape[0], width), lambda b, h: (0, h)),
            pl.BlockSpec((HGRN_HEADS, HGRN_DIM), lambda b, h: (0, 0)),
            blk(OFF_HQ), blk(OFF_HF), blk(OFF_HI), blk(OFF_HG),
        ],
        out_specs=pl.BlockSpec((seq, width), lambda b, h: (b, h)),
        out_shape=jax.ShapeDtypeStruct((batch * seq, HGRN_WIDTH), BF16),
        scratch_shapes=[
            pltpu.VMEM((HGRN_UNROLL, CHUNK, HGRN_DIM), F32),
            pltpu.VMEM((seq, HGRN_DIM), BF16),
            pltpu.VMEM((n_chunks, CHUNK, CHUNK), BF16),
            pltpu.VMEM((n_chunks, HGRN_DIM, HGRN_DIM), F32),
            pltpu.VMEM((n_chunks, SUBLANES, HGRN_DIM), F32),
            pltpu.VMEM((n_chunks, HGRN_DIM, HGRN_DIM), BF16),
        ],
        compiler_params=pltpu.CompilerParams(dimension_semantics=("parallel", "parallel")),
        name="hgrn2",
    )(lb_logits, hgrn_norm, proj, proj, proj, proj)


SM_BLOCKS = 2
SM_TM = SM_BLOCKS * WINDOW
OUT_TN = 512
OUT_NCHUNK = D_MODEL // OUT_TN


def _t5_bucket(dist):
    max_exact = REL_BUCKETS // 2
    d = jnp.maximum(dist, 0)
    df = jnp.maximum(d, 1).astype(F32)
    large = max_exact + (jnp.log(df / max_exact) / math.log(REL_MAX_DIST / max_exact)
                         * (REL_BUCKETS - max_exact)).astype(jnp.int32)
    large = jnp.minimum(large, REL_BUCKETS - 1)
    return jnp.where(d < max_exact, d, large)


def _swap_halves(x):
    half = LANES // 2
    return jnp.concatenate([x[:, half:], x[:, :half]], axis=1)


def _swa_tables(relb_ref, bias_ref, cur_ref):
    qi = lax.broadcasted_iota(jnp.int32, (WINDOW, WINDOW), 0)
    sj = lax.broadcasted_iota(jnp.int32, (WINDOW, WINDOW), 1)
    own = sj <= qi
    bucket = _t5_bucket(jnp.where(own, qi - sj, qi + WINDOW - sj))
    for h in range(ATTN_HEADS):
        def pick(b, acc, h=h):
            return jnp.where(bucket == b, relb_ref[b, h], acc)
        tbl = lax.fori_loop(0, REL_BUCKETS, pick, jnp.zeros((WINDOW, WINDOW), F32))
        bias_ref[0, h] = tbl
        bias_ref[1, h] = jnp.where(own, tbl, MASK_VALUE)
    own_bf = jnp.where(own, 1.0, 0.0).astype(BF16)
    cur_ref[...] = jnp.concatenate([own_bf, own_bf], axis=0)


def _swa_merge_kernel(blocks_per_seq, n_tiles, relb_ref, sink_ref, q_ref, kc_ref, vc_ref, kp_ref,
                      vp_ref, ga_ref, gb_ref, yh_ref, *rest):
    gate_a_refs = rest[:OUT_NCHUNK]
    gate_h_refs = rest[OUT_NCHUNK:2 * OUT_NCHUNK]
    (x_ref, wa_ref, wh_ref, wo_ref, gain_ref, o_ref,
     bias_ref, cur_ref, ya_ref, m_ref) = rest[2 * OUT_NCHUNK:]
    step = pl.program_id(0)

    @pl.when(step == 0)
    def _():
        _swa_tables(relb_ref, bias_ref, cur_ref)
        ya_ref[...] = jnp.zeros_like(ya_ref)

    lane = lax.broadcasted_iota(jnp.int32, (1, LANES), 1)
    is_lo = lane < LANES // 2
    scale = ATTN_HEAD_DIM ** -0.5
    keep_lo = jnp.where(is_lo, 1.0, 0.0).astype(BF16)
    keep_hi = jnp.where(is_lo, 0.0, 1.0).astype(BF16)
    row = lax.broadcasted_iota(jnp.int32, (2 * WINDOW, 1), 0)
    top = row < WINDOW
    own2 = (lax.broadcasted_iota(jnp.int32, (2 * WINDOW, WINDOW), 1)
            <= (lax.broadcasted_iota(jnp.int32, (2 * WINDOW, WINDOW), 0) & (WINDOW - 1)))
    tile = jnp.minimum(step, n_tiles - 1)
    tile_starts_seq = jnp.where(tile % (blocks_per_seq // SM_BLOCKS) == 0, 1, 0)

    def scores(qm, km, first, heads):
        s2 = lax.dot_general(qm, km, _NT, preferred_element_type=F32)
        bias = jnp.concatenate([bias_ref[first, heads[0]], bias_ref[first, heads[1]]], axis=0)
        return jnp.where(own2, s2[:, WINDOW:], s2[:, :WINDOW]) + bias

    def probs(s, heads):
        sink = jnp.where(top, sink_ref[0, heads[0]], sink_ref[0, heads[1]])
        m = jnp.maximum(jnp.max(s, axis=-1, keepdims=True), sink)
        p = jnp.exp(s - m)
        denom = jnp.sum(p, axis=-1, keepdims=True) + jnp.exp(sink - m)
        pb = p.astype(BF16)
        p_own = pb * cur_ref[...]
        return jnp.concatenate([pb - p_own, p_own], axis=1), 1.0 / denom

    def heads_of(hk):
        e = hk % 2
        return (4 * hk + e, 4 * hk + 2 + e), (4 * hk + 1 - e, 4 * hk + 3 - e)

    s_same, s_diff, vals = [], [], []
    for blk in range(SM_BLOCKS):
        rows = slice(blk * WINDOW, (blk + 1) * WINDOW)
        first = tile_starts_seq if blk == 0 else 0
        for hk in range(ATTN_KV_HEADS):
            c, e = divmod(hk, 2)
            cs = slice(c * LANES, (c + 1) * LANES)
            if blk == 0:
                k_prev, v_prev = kp_ref[:, cs], vp_ref[:, cs]
            else:
                prev_rows = slice((blk - 1) * WINDOW, blk * WINDOW)
                k_prev, v_prev = kc_ref[prev_rows, cs], vc_ref[prev_rows, cs]
            k2 = jnp.concatenate([k_prev, kc_ref[rows, cs]], axis=0)
            v2 = jnp.concatenate([v_prev, vc_ref[rows, cs]], axis=0)
            k2s, v2s = _swap_halves(k2), _swap_halves(v2)
            keep_e, keep_o = (keep_lo, keep_hi) if e == 0 else (keep_hi, keep_lo)
            p0 = q_ref[rows, hk * 2 * LANES:hk * 2 * LANES + LANES]
            p1 = q_ref[rows, hk * 2 * LANES + LANES:(hk + 1) * 2 * LANES]
            q_same = jnp.concatenate([p0 * (keep_e * scale), p1 * (keep_e * scale)], axis=0)
            q_diff = jnp.concatenate([p0 * (keep_o * scale), p1 * (keep_o * scale)], axis=0)
            h_same, h_diff = heads_of(hk)
            s_same.append(scores(q_same, k2, first, h_same))
            s_diff.append(scores(q_diff, k2s, first, h_diff))
            vals.append((v2 * keep_lo, v2s * keep_hi) if e == 0 else (v2s * keep_lo, v2 * keep_hi))

    ya_prev = ya_ref[...]
    yh = yh_ref[...]
    for j in range(OUT_NCHUNK):
        cs = slice(j * OUT_TN, (j + 1) * OUT_TN)
        ua = jnp.dot(ya_prev, wa_ref[:, cs], preferred_element_type=F32)
        uh = jnp.dot(yh, wh_ref[:, cs], preferred_element_type=F32)
        merged = (_sigmoid(gate_a_refs[j][...].astype(F32)) * ua
                  + _sigmoid(gate_h_refs[j][...].astype(F32)) * uh)
        m_ref[:, cs] = merged.astype(BF16)

    pr_same = [probs(s_same[i], heads_of(i % ATTN_KV_HEADS)[0]) for i in range(len(s_same))]
    pr_diff = [probs(s_diff[i], heads_of(i % ATTN_KV_HEADS)[1]) for i in range(len(s_diff))]
    for blk in range(SM_BLOCKS):
        rows = slice(blk * WINDOW, (blk + 1) * WINDOW)
        for hk in range(ATTN_KV_HEADS):
            i = blk * ATTN_KV_HEADS + hk
            (p_lo, r_lo), (p_hi, r_hi) = ((pr_same[i], pr_diff[i]) if hk % 2 == 0
                                          else (pr_diff[i], pr_same[i]))
            v_lo, v_hi = vals[i]
            pv = jnp.dot(jnp.concatenate([p_lo, p_hi], axis=1),
                         jnp.concatenate([v_lo, v_hi], axis=0),
                         preferred_element_type=F32)
            pv = pv * jnp.where(is_lo, r_lo, r_hi)
            g_ref = ga_ref if hk < 2 else gb_ref
            for pair in range(2):
                gs = slice((hk % 2) * 2 * LANES + pair * LANES,
                           (hk % 2) * 2 * LANES + (pair + 1) * LANES)
                gate = _silu(g_ref[rows, gs].astype(F32))
                ya_ref[rows, (2 * hk + pair) * LANES:(2 * hk + pair + 1) * LANES] = (
                    pv[pair * WINDOW:(pair + 1) * WINDOW] * gate).astype(BF16)

    y = jnp.dot(m_ref[...], wo_ref[...], preferred_element_type=F32)
    ms = jnp.mean(y * y, axis=-1, keepdims=True)
    o_ref[...] = x_ref[...] + y * lax.rsqrt(ms + NORM_EPS) * gain_ref[...]


def _swa_merge(proj, yh, x2d, rel_bias, sinks, wa, wh, wo, gain, seq):
    tokens = x2d.shape[0]
    n_tiles = tokens // SM_TM
    blocks_per_seq = seq // WINDOW
    attn_tile = lambda t: jnp.minimum(t, n_tiles - 1)
    merge_tile = lambda t: jnp.maximum(t - 1, 0)
    prev_block = lambda t: jnp.maximum(attn_tile(t) * SM_BLOCKS - 1, 0)
    kcol, vcol = OFF_AK // KV_WIDTH, OFF_AV // KV_WIDTH
    gcol = OFF_AG // (ATTN_WIDTH // 2)
    const = lambda t: (0, 0)
    smem = pl.BlockSpec(memory_space=pltpu.SMEM)
    gate = lambda off, j: pl.BlockSpec((SM_TM, OUT_TN), lambda t: (merge_tile(t), off // OUT_TN + j))
    resident = functools.partial(pl.BlockSpec, index_map=const, pipeline_mode=pl.Buffered(1))
    return pl.pallas_call(
        functools.partial(_swa_merge_kernel, blocks_per_seq, n_tiles),
        grid=(n_tiles + 1,),
        in_specs=(
            [smem, smem,
             pl.BlockSpec((SM_TM, ATTN_WIDTH), lambda t: (attn_tile(t), 0)),
             pl.BlockSpec((SM_TM, KV_WIDTH), lambda t: (attn_tile(t), kcol)),
             pl.BlockSpec((SM_TM, KV_WIDTH), lambda t: (attn_tile(t), vcol)),
             pl.BlockSpec((WINDOW, KV_WIDTH), lambda t: (prev_block(t), kcol)),
             pl.BlockSpec((WINDOW, KV_WIDTH), lambda t: (prev_block(t), vcol)),
             pl.BlockSpec((SM_TM, ATTN_WIDTH // 2), lambda t: (attn_tile(t), gcol)),
             pl.BlockSpec((SM_TM, ATTN_WIDTH // 2), lambda t: (attn_tile(t), gcol + 1)),
             pl.BlockSpec((SM_TM, HGRN_WIDTH), lambda t: (merge_tile(t), 0))]
            + [gate(OFF_GA, j) for j in range(OUT_NCHUNK)]
            + [gate(OFF_GH, j) for j in range(OUT_NCHUNK)]
            + [pl.BlockSpec((SM_TM, D_MODEL), lambda t: (merge_tile(t), 0)),
               resident((ATTN_WIDTH, D_MODEL)), resident((HGRN_WIDTH, D_MODEL)),
               resident((D_MODEL, D_MODEL)), pl.BlockSpec((1, D_MODEL), const)]),
        out_specs=pl.BlockSpec((SM_TM, D_MODEL), lambda t: (merge_tile(t), 0)),
        out_shape=jax.ShapeDtypeStruct((tokens, D_MODEL), F32),
        scratch_shapes=[
            pltpu.VMEM((2, ATTN_HEADS, WINDOW, WINDOW), F32),
            pltpu.VMEM((2 * WINDOW, WINDOW), BF16),
            pltpu.VMEM((SM_TM, ATTN_WIDTH), BF16),
            pltpu.VMEM((SM_TM, D_MODEL), BF16),
        ],
        compiler_params=pltpu.CompilerParams(
            dimension_semantics=("arbitrary",), vmem_limit_bytes=52 * 1024 * 1024),
        name="swa_merge",
    )(rel_bias, sinks, proj, proj, proj, proj, proj, proj, proj, yh,
      *([proj] * (2 * OUT_NCHUNK)), x2d, wa, wh, wo, gain)


def kernel(x, norm_pre, w_in, rel_bias, attn_sinks, lb_logits, hgrn_norm, w_branch_attn,
           w_branch_hgrn, w_out, norm_post):
    batch, seq, d_model = x.shape
    depth = w_in.shape[0]
    assert depth == 1 and d_model == D_MODEL and seq % IN_TM == 0
    assert lb_logits.shape == (depth + 1, HGRN_WIDTH)
    x2d = x.reshape(batch * seq, d_model)
    layer = 0
    proj = _inproj(x2d, norm_pre[layer][None, :], w_in[layer].astype(BF16))
    yh = _hgrn(proj, lb_logits, hgrn_norm[layer], batch, seq)
    out = _swa_merge(proj, yh, x2d, rel_bias, attn_sinks[layer][None, :],
                     w_branch_attn[layer].astype(BF16), w_branch_hgrn[layer].astype(BF16),
                     w_out[layer].astype(BF16), norm_post[layer][None, :], seq)
    return out.reshape(batch, seq, d_model)
```

```python
import functools
import math

import jax
import jax.numpy as jnp
from jax import lax
from jax.experimental import pallas as pl
from jax.experimental.pallas import tpu as pltpu

D_MODEL = 2048
ATTN_HEADS = 16
ATTN_KV_HEADS = 4
ATTN_HEAD_DIM = 64
ATTN_GROUP = ATTN_HEADS // ATTN_KV_HEADS
WINDOW = 128
ATTN_WIDTH = ATTN_HEADS * ATTN_HEAD_DIM
KV_WIDTH = ATTN_KV_HEADS * ATTN_HEAD_DIM
HGRN_HEADS = 8
HGRN_DIM = 128
HGRN_WIDTH = HGRN_HEADS * HGRN_DIM
CHUNK = 64
CHUNK_LEVELS = 6
HGRN_UNROLL = 4
HGRN_SLOTS = 6
HGRN_HEADS_PER_STEP = 2
N_SEG = 4
REL_BUCKETS = 32
REL_MAX_DIST = 128
NORM_EPS = 1e-6
IN_PROJ_WIDTH = 10752

OFF_AQ = 0
OFF_AK = OFF_AQ + ATTN_WIDTH
OFF_AV = OFF_AK + KV_WIDTH
OFF_AG = OFF_AV + KV_WIDTH
OFF_HQ = OFF_AG + ATTN_WIDTH
OFF_HF = OFF_HQ + HGRN_WIDTH
OFF_HI = OFF_HF + HGRN_WIDTH
OFF_HG = OFF_HI + HGRN_WIDTH
OFF_GA = OFF_HG + HGRN_WIDTH
OFF_GH = OFF_GA + D_MODEL

LANES = 128
SUBLANES = 8
MASK_VALUE = -1e30

F32 = jnp.float32
BF16 = jnp.bfloat16

_NT = (((1,), (1,)), ((), ()))
_TN = (((0,), (0,)), ((), ()))


def _sigmoid(x):
    return 1.0 / (1.0 + jnp.exp(-x))


def _silu(x):
    return x * _sigmoid(x)


IN_TM = 1024
IN_TN = 1536
IN_NORM_ROWS = 256
IN_PIECE = 256
IN_PIECES = IN_TN // IN_PIECE


def _prenorm(x_ref, gain_ref, h_ref):
    gain = gain_ref[...]
    for r in range(0, IN_TM, IN_NORM_ROWS):
        xf = x_ref[r:r + IN_NORM_ROWS, :]
        ms = jnp.mean(xf * xf, axis=-1, keepdims=True)
        h_ref[r:r + IN_NORM_ROWS, :] = (xf * lax.rsqrt(ms + NORM_EPS) * gain).astype(BF16)


def _inproj_kernel(x_ref, gain_ref, w_ref, o_ref, h_ref):
    @pl.when(pl.program_id(1) == 0)
    def _():
        _prenorm(x_ref, gain_ref, h_ref)

    o_ref[...] = jnp.dot(h_ref[...], w_ref[...], preferred_element_type=F32).astype(BF16)


def _inproj_specs(seg_row_tiles, seg):
    return [
        pl.BlockSpec((IN_TM, D_MODEL), lambda i, j: (seg * seg_row_tiles + i, 0)),
        pl.BlockSpec((1, D_MODEL), lambda i, j: (0, 0)),
        pl.BlockSpec((D_MODEL, IN_TN), lambda i, j: (0, j)),
    ]


def _inproj(x2d, gain, w_bf16, seg, seg_tokens):
    seg_row_tiles = seg_tokens // IN_TM
    return pl.pallas_call(
        _inproj_kernel,
        grid=(seg_row_tiles, IN_PROJ_WIDTH // IN_TN),
        in_specs=_inproj_specs(seg_row_tiles, seg),
        out_specs=pl.BlockSpec((IN_TM, IN_TN), lambda i, j: (i, j)),
        out_shape=jax.ShapeDtypeStruct((seg_tokens, IN_PROJ_WIDTH), BF16),
        scratch_shapes=[pltpu.VMEM((IN_TM, D_MODEL), BF16)],
        compiler_params=pltpu.CompilerParams(
            dimension_semantics=("parallel", "arbitrary"),
            vmem_limit_bytes=56 * 1024 * 1024),
        name="inproj",
    )(x2d, gain, w_bf16)


def _level_reference(g_ref, level, sub):
    half = 1 << level
    blk = 2 * half

    def row_tile(r):
        return g_ref[pl.ds(r, SUBLANES, stride=0), :]

    if blk >= SUBLANES:
        return jnp.concatenate(
            [row_tile(b * blk + half) for b in range(CHUNK // blk) for _ in range(blk // SUBLANES)],
            axis=0)
    tiles = []
    for t in range(CHUNK // SUBLANES):
        acc = row_tile(t * SUBLANES + half)
        for b in range(1, SUBLANES // blk):
            acc = jnp.where(sub >= b * blk, row_tile(t * SUBLANES + b * blk + half), acc)
        tiles.append(acc)
    return jnp.concatenate(tiles, axis=0)


def _neg_abs(x):
    bits = lax.bitcast_convert_type(x, jnp.uint32) | jnp.uint32(0x80000000)
    return lax.bitcast_convert_type(bits, F32)


def _hgrn_kernel(lb_ref, hn_ref, hq_ref, hf_ref, hi_ref, hg_ref, o_ref, *scratch):
    for j in range(HGRN_HEADS_PER_STEP):
        cols = slice(j * HGRN_DIM, (j + 1) * HGRN_DIM)
        head = pl.program_id(1) * HGRN_HEADS_PER_STEP + j
        _hgrn_head(lb_ref.at[:, cols], hn_ref.at[pl.ds(head, 1), :], hq_ref.at[:, cols],
                   hf_ref.at[:, cols], hi_ref.at[:, cols], hg_ref.at[:, cols], o_ref.at[:, cols],
                   *scratch)


def _hgrn_head(lb_ref, hn_ref, hq_ref, hf_ref, hi_ref, hg_ref, o_ref,
               g_ref, qg_ref, attn_ref, upd_ref, keep_ref, state_ref, piece_fn=None, n_pieces=0):
    seq = hq_ref.shape[0]
    lg = lb_ref[...]
    ex = jnp.exp(lg - jnp.max(lg, axis=0, keepdims=True))
    lb = ex[0:1, :] / jnp.sum(ex, axis=0, keepdims=True)
    hn = hn_ref[...]

    rowi = lax.broadcasted_iota(jnp.int32, (CHUNK, HGRN_DIM), 0)
    sub = lax.broadcasted_iota(jnp.int32, (SUBLANES, HGRN_DIM), 0)
    ti = lax.broadcasted_iota(jnp.int32, (CHUNK, CHUNK), 0)
    si = lax.broadcasted_iota(jnp.int32, (CHUNK, CHUNK), 1)
    xor = ti ^ si
    lvl = jnp.full((CHUNK, CHUNK), -2, jnp.int32)
    for level in range(CHUNK_LEVELS):
        lvl = jnp.where(xor >= (1 << level), level, lvl)
    lvl = jnp.where(ti > si, lvl, jnp.where(ti == si, -1, -2))

    def chunk_rows(c):
        start = c * CHUNK
        return pl.ds(start if isinstance(c, int) else pl.multiple_of(start, CHUNK), CHUNK)

    def local_step(c, slot):
        rows = chunk_rows(c)
        q = _silu(hq_ref[rows, :].astype(F32))
        f = lb + (1.0 - lb) * _sigmoid(hf_ref[rows, :].astype(F32))
        k = 1.0 - f
        v = hi_ref[rows, :]
        g = jnp.log2(f)
        for shift in (1, 2, 4):
            g = g + jnp.where(rowi >= shift, pltpu.roll(g, shift, 0), 0.0)
        for shift in (8, 16, 32):
            g = g + jnp.concatenate(
                [jnp.zeros((shift, HGRN_DIM), F32), g[:CHUNK - shift]], axis=0)
        g_slot = g_ref.at[slot]
        g_slot[...] = g
        g_last = g_slot[pl.ds(CHUNK - 1, SUBLANES, stride=0), :][0:1]
        qb = q.astype(BF16)
        kb = k.astype(BF16)
        qg_ref[rows, :] = qb * jnp.exp2(g).astype(BF16)

        attn = jnp.where(lvl == -1, jnp.sum(q * k, axis=-1, keepdims=True), 0.0)
        for level in range(CHUNK_LEVELS):
            dec = jnp.exp2(_neg_abs(g - _level_reference(g_slot, level, sub))).astype(BF16)
            a = lax.dot_general(qb * dec, kb * dec, _NT, preferred_element_type=F32)
            attn = jnp.where(lvl == level, a, attn)
        attn_ref[c] = attn.astype(BF16)

        k_dec = kb * jnp.exp2(g_last - g).astype(BF16)
        upd_ref[c] = lax.dot_general(v, k_dec, _TN, preferred_element_type=F32)
        keep_ref[c] = jnp.broadcast_to(jnp.exp2(g_last), (SUBLANES, HGRN_DIM))

    def local_group(i, carry):
        for slot in range(HGRN_UNROLL):
            local_step(i * HGRN_UNROLL + slot, slot)
        return carry

    n_chunks = seq // CHUNK
    if n_pieces:
        bounds = [(p * n_chunks) // n_pieces for p in range(n_pieces + 1)]
        for p in range(n_pieces):
            piece_fn(p)
            for c in range(bounds[p], bounds[p + 1]):
                local_step(c, c - bounds[p])
    else:
        lax.fori_loop(0, n_chunks // HGRN_UNROLL, local_group, 0)

    def state_step(c, state_t):
        state_ref[c] = state_t.astype(BF16)
        return keep_ref[c][0:1] * state_t + upd_ref[c]

    lax.fori_loop(0, n_chunks, state_step, jnp.zeros((HGRN_DIM, HGRN_DIM), F32),
                  unroll=HGRN_UNROLL)

    def out_step(c, carry):
        rows = chunk_rows(c)
        o = lax.dot_general(qg_ref[rows, :], state_ref[c], _NT, preferred_element_type=F32)
        o = o + jnp.dot(attn_ref[c], hi_ref[rows, :], preferred_element_type=F32)
        ms = jnp.mean(o * o, axis=-1, keepdims=True)
        y = o * lax.rsqrt(ms + NORM_EPS) * hn * _silu(hg_ref[rows, :].astype(F32))
        o_ref[rows, :] = y.astype(BF16)
        return carry

    lax.fori_loop(0, n_chunks, out_step, 0, unroll=2 * HGRN_UNROLL)


def _hgrn_scratch(seq):
    n_chunks = seq // CHUNK
    return [
        pltpu.VMEM((HGRN_SLOTS, CHUNK, HGRN_DIM), F32),
        pltpu.VMEM((seq, HGRN_DIM), BF16),
        pltpu.VMEM((n_chunks, CHUNK, CHUNK), BF16),
        pltpu.VMEM((n_chunks, HGRN_DIM, HGRN_DIM), F32),
        pltpu.VMEM((n_chunks, SUBLANES, HGRN_DIM), F32),
        pltpu.VMEM((n_chunks, HGRN_DIM, HGRN_DIM), BF16),
    ]


def _hgrn(proj, lb_logits, hgrn_norm, batch, seq):
    width = HGRN_HEADS_PER_STEP * HGRN_DIM
    col = lambda off: (lambda b, h: (b, off // width + h))
    blk = lambda off: pl.BlockSpec((seq, width), col(off))
    return pl.pallas_call(
        _hgrn_kernel,
        grid=(batch, HGRN_HEADS // HGRN_HEADS_PER_STEP),
        in_specs=[
            pl.BlockSpec((lb_logits.shape[0], width), lambda b, h: (0, h)),
            pl.BlockSpec((HGRN_HEADS, HGRN_DIM), lambda b, h: (0, 0)),
            blk(OFF_HQ), blk(OFF_HF), blk(OFF_HI), blk(OFF_HG),
        ],
        out_specs=pl.BlockSpec((seq, width), lambda b, h: (b, h)),
        out_shape=jax.ShapeDtypeStruct((batch * seq, HGRN_WIDTH), BF16),
        scratch_shapes=_hgrn_scratch(seq),
        compiler_params=pltpu.CompilerParams(dimension_semantics=("parallel", "parallel")),
        name="hgrn2",
    )(lb_logits, hgrn_norm, proj, proj, proj, proj)


def _inproj_hgrn_kernel(heads_per_tile, x_ref, gain_ref, w_ref, lb_ref, hn_ref, hq_ref, hf_ref,
                        hi_ref, hg_ref, o_ref, yh_ref, h_ref, *hgrn_scratch):
    col_step = pl.program_id(1)

    @pl.when(col_step == 0)
    def _():
        _prenorm(x_ref, gain_ref, h_ref)

    def piece(p):
        cs = slice(p * IN_PIECE, (p + 1) * IN_PIECE)
        o_ref[:, cs] = jnp.dot(h_ref[...], w_ref[:, cs], preferred_element_type=F32).astype(BF16)

    @pl.when(col_step < heads_per_tile)
    def _():
        tiles_per_seq = HGRN_HEADS // heads_per_tile
        head = (pl.program_id(0) % tiles_per_seq) * heads_per_tile + col_step
        _hgrn_head(lb_ref, hn_ref.at[pl.ds(head, 1), :], hq_ref, hf_ref, hi_ref, hg_ref, yh_ref,
                   *hgrn_scratch, piece_fn=piece, n_pieces=IN_PIECES)

    @pl.when(col_step >= heads_per_tile)
    def _():
        o_ref[...] = jnp.dot(h_ref[...], w_ref[...], preferred_element_type=F32).astype(BF16)


def _inproj_hgrn(x2d, gain, w_bf16, seg, seg_tokens, proj_prev, lb_logits, hgrn_norm, seq):
    seg_row_tiles = seg_tokens // IN_TM
    tiles_per_seq = seq // IN_TM
    heads_per_tile = HGRN_HEADS // tiles_per_seq
    assert heads_per_tile <= IN_PROJ_WIDTH // IN_TN
    assert pl.cdiv(seq // CHUNK, IN_PIECES) <= HGRN_SLOTS

    def head_block(off):
        def index(i, j):
            head = (i % tiles_per_seq) * heads_per_tile + jnp.minimum(j, heads_per_tile - 1)
            return (i // tiles_per_seq, off // HGRN_DIM + head)
        return index

    blk = lambda off: pl.BlockSpec((seq, HGRN_DIM), head_block(off))
    return pl.pallas_call(
        functools.partial(_inproj_hgrn_kernel, heads_per_tile),
        grid=(seg_row_tiles, IN_PROJ_WIDTH // IN_TN),
        in_specs=_inproj_specs(seg_row_tiles, seg) + [
            pl.BlockSpec((lb_logits.shape[0], HGRN_DIM), lambda i, j: (0, head_block(0)(i, j)[1])),
            pl.BlockSpec((HGRN_HEADS, HGRN_DIM), lambda i, j: (0, 0)),
            blk(OFF_HQ), blk(OFF_HF), blk(OFF_HI), blk(OFF_HG),
        ],
        out_specs=[pl.BlockSpec((IN_TM, IN_TN), lambda i, j: (i, j)),
                   pl.BlockSpec((seq, HGRN_DIM), head_block(0))],
        out_shape=[jax.ShapeDtypeStruct((seg_tokens, IN_PROJ_WIDTH), BF16),
                   jax.ShapeDtypeStruct((proj_prev.shape[0], HGRN_WIDTH), BF16)],
        scratch_shapes=[pltpu.VMEM((IN_TM, D_MODEL), BF16)] + _hgrn_scratch(seq),
        compiler_params=pltpu.CompilerParams(
            dimension_semantics=("arbitrary", "arbitrary"),
            vmem_limit_bytes=58 * 1024 * 1024),
        name="inproj_hgrn2",
    )(x2d, gain, w_bf16, lb_logits, hgrn_norm, proj_prev, proj_prev, proj_prev, proj_prev)


SM_BLOCKS = 2
SM_TM = SM_BLOCKS * WINDOW
OUT_TN = 512
OUT_NCHUNK = D_MODEL // OUT_TN


def _t5_bucket(dist):
    max_exact = REL_BUCKETS // 2
    d = jnp.maximum(dist, 0)
    df = jnp.maximum(d, 1).astype(F32)
    large = max_exact + (jnp.log(df / max_exact) / math.log(REL_MAX_DIST / max_exact)
                         * (REL_BUCKETS - max_exact)).astype(jnp.int32)
    large = jnp.minimum(large, REL_BUCKETS - 1)
    return jnp.where(d < max_exact, d, large)


def _swap_halves(x):
    half = LANES // 2
    return jnp.concatenate([x[:, half:], x[:, :half]], axis=1)


def _swa_tables(relb_ref, bias_ref, cur_ref):
    qi = lax.broadcasted_iota(jnp.int32, (WINDOW, WINDOW), 0)
    sj = lax.broadcasted_iota(jnp.int32, (WINDOW, WINDOW), 1)
    own = sj <= qi
    bucket = _t5_bucket(jnp.where(own, qi - sj, qi + WINDOW - sj))
    for h in range(ATTN_HEADS):
        def pick(b, acc, h=h):
            return jnp.where(bucket == b, relb_ref[b, h], acc)
        tbl = lax.fori_loop(0, REL_BUCKETS, pick, jnp.zeros((WINDOW, WINDOW), F32))
        bias_ref[0, h] = tbl
        bias_ref[1, h] = jnp.where(own, tbl, MASK_VALUE)
    own_bf = jnp.where(own, 1.0, 0.0).astype(BF16)
    cur_ref[...] = jnp.concatenate([own_bf, own_bf], axis=0)


def _swa_merge_kernel(blocks_per_seq, n_tiles, relb_ref, sink_ref, q_ref, kc_ref, vc_ref, kp_ref,
                      vp_ref, ga_ref, gb_ref, yh_ref, *rest):
    gate_a_refs = rest[:OUT_NCHUNK]
    gate_h_refs = rest[OUT_NCHUNK:2 * OUT_NCHUNK]
    x_ref, wa_ref, wh_ref, wo_ref, gain_ref = rest[2 * OUT_NCHUNK:2 * OUT_NCHUNK + 5]
    o_ref, bias_ref, cur_ref, ya_ref, m_ref = rest[-5:]
    step = pl.program_id(0)

    @pl.when(step == 0)
    def _():
        _swa_tables(relb_ref, bias_ref, cur_ref)
        ya_ref[...] = jnp.zeros_like(ya_ref)

    lane = lax.broadcasted_iota(jnp.int32, (1, LANES), 1)
    is_lo = lane < LANES // 2
    scale = ATTN_HEAD_DIM ** -0.5
    keep_lo = jnp.where(is_lo, 1.0, 0.0).astype(BF16)
    keep_hi = jnp.where(is_lo, 0.0, 1.0).astype(BF16)
    row = lax.broadcasted_iota(jnp.int32, (2 * WINDOW, 1), 0)
    top = row < WINDOW
    own2 = (lax.broadcasted_iota(jnp.int32, (2 * WINDOW, WINDOW), 1)
            <= (lax.broadcasted_iota(jnp.int32, (2 * WINDOW, WINDOW), 0) & (WINDOW - 1)))
    tile = jnp.minimum(step, n_tiles - 1)
    tile_starts_seq = jnp.where(tile % (blocks_per_seq // SM_BLOCKS) == 0, 1, 0)

    def scores(qm, km, first, heads):
        s2 = lax.dot_general(qm, km, _NT, preferred_element_type=F32)
        bias = jnp.concatenate([bias_ref[first, heads[0]], bias_ref[first, heads[1]]], axis=0)
        return jnp.where(own2, s2[:, WINDOW:], s2[:, :WINDOW]) + bias

    def probs(s, heads):
        sink = jnp.where(top, sink_ref[0, heads[0]], sink_ref[0, heads[1]])
        m = jnp.maximum(jnp.max(s, axis=-1, keepdims=True), sink)
        p = jnp.exp(s - m)
        denom = jnp.sum(p, axis=-1, keepdims=True) + jnp.exp(sink - m)
        pb = p.astype(BF16)
        p_own = pb * cur_ref[...]
        return jnp.concatenate([pb - p_own, p_own], axis=1), 1.0 / denom

    def heads_of(hk):
        e = hk % 2
        return (4 * hk + e, 4 * hk + 2 + e), (4 * hk + 1 - e, 4 * hk + 3 - e)

    s_same, s_diff, vals = [], [], []
    for blk in range(SM_BLOCKS):
        rows = slice(blk * WINDOW, (blk + 1) * WINDOW)
        first = tile_starts_seq if blk == 0 else 0
        for hk in range(ATTN_KV_HEADS):
            c, e = divmod(hk, 2)
            cs = slice(c * LANES, (c + 1) * LANES)
            if blk == 0:
                k_prev, v_prev = kp_ref[:, cs], vp_ref[:, cs]
            else:
                prev_rows = slice((blk - 1) * WINDOW, blk * WINDOW)
                k_prev, v_prev = kc_ref[prev_rows, cs], vc_ref[prev_rows, cs]
            k2 = jnp.concatenate([k_prev, kc_ref[rows, cs]], axis=0)
            v2 = jnp.concatenate([v_prev, vc_ref[rows, cs]], axis=0)
            k2s, v2s = _swap_halves(k2), _swap_halves(v2)
            keep_e, keep_o = (keep_lo, keep_hi) if e == 0 else (keep_hi, keep_lo)
            p0 = q_ref[rows, hk * 2 * LANES:hk * 2 * LANES + LANES]
            p1 = q_ref[rows, hk * 2 * LANES + LANES:(hk + 1) * 2 * LANES]
            q_same = jnp.concatenate([p0 * (keep_e * scale), p1 * (keep_e * scale)], axis=0)
            q_diff = jnp.concatenate([p0 * (keep_o * scale), p1 * (keep_o * scale)], axis=0)
            h_same, h_diff = heads_of(hk)
            s_same.append(scores(q_same, k2, first, h_same))
            s_diff.append(scores(q_diff, k2s, first, h_diff))
            vals.append((v2 * keep_lo, v2s * keep_hi) if e == 0 else (v2s * keep_lo, v2 * keep_hi))

    ya_prev = ya_ref[...]
    yh = yh_ref[...]
    for j in range(OUT_NCHUNK):
        cs = slice(j * OUT_TN, (j + 1) * OUT_TN)
        ua = jnp.dot(ya_prev, wa_ref[:, cs], preferred_element_type=F32)
        uh = jnp.dot(yh, wh_ref[:, cs], preferred_element_type=F32)
        merged = (_sigmoid(gate_a_refs[j][...].astype(F32)) * ua
                  + _sigmoid(gate_h_refs[j][...].astype(F32)) * uh)
        m_ref[:, cs] = merged.astype(BF16)

    pr_same = [probs(s_same[i], heads_of(i % ATTN_KV_HEADS)[0]) for i in range(len(s_same))]
    pr_diff = [probs(s_diff[i], heads_of(i % ATTN_KV_HEADS)[1]) for i in range(len(s_diff))]
    for blk in range(SM_BLOCKS):
        rows = slice(blk * WINDOW, (blk + 1) * WINDOW)
        for hk in range(ATTN_KV_HEADS):
            i = blk * ATTN_KV_HEADS + hk
            (p_lo, r_lo), (p_hi, r_hi) = ((pr_same[i], pr_diff[i]) if hk % 2 == 0
                                          else (pr_diff[i], pr_same[i]))
            v_lo, v_hi = vals[i]
            pv = jnp.dot(jnp.concatenate([p_lo, p_hi], axis=1),
                         jnp.concatenate([v_lo, v_hi], axis=0),
                         preferred_element_type=F32)
            pv = pv * jnp.where(is_lo, r_lo, r_hi)
            g_ref = ga_ref if hk < 2 else gb_ref
            for pair in range(2):
                gs = slice((hk % 2) * 2 * LANES + pair * LANES,
                           (hk % 2) * 2 * LANES + (pair + 1) * LANES)
                gate = _silu(g_ref[rows, gs].astype(F32))
                ya_ref[rows, (2 * hk + pair) * LANES:(2 * hk + pair + 1) * LANES] = (
                    pv[pair * WINDOW:(pair + 1) * WINDOW] * gate).astype(BF16)

    y = jnp.dot(m_ref[...], wo_ref[...], preferred_element_type=F32)
    ms = jnp.mean(y * y, axis=-1, keepdims=True)
    o_ref[...] = x_ref[...] + y * lax.rsqrt(ms + NORM_EPS) * gain_ref[...]


def _swa_merge(proj, yh, x2d, out_prev, seg, rel_bias, sinks, wa, wh, wo, gain, seq):
    tokens = x2d.shape[0]
    n_tiles = proj.shape[0] // SM_TM
    blocks_per_seq = seq // WINDOW
    attn_tile = lambda t: jnp.minimum(t, n_tiles - 1)
    merge_tile = lambda t: jnp.maximum(t - 1, 0)
    global_tile = lambda t: seg * n_tiles + merge_tile(t)
    prev_block = lambda t: jnp.maximum(attn_tile(t) * SM_BLOCKS - 1, 0)
    kcol, vcol = OFF_AK // KV_WIDTH, OFF_AV // KV_WIDTH
    gcol = OFF_AG // (ATTN_WIDTH // 2)
    const = lambda t: (0, 0)
    smem = pl.BlockSpec(memory_space=pltpu.SMEM)
    gate = lambda off, j: pl.BlockSpec((SM_TM, OUT_TN), lambda t: (merge_tile(t), off // OUT_TN + j))
    resident = functools.partial(pl.BlockSpec, index_map=const, pipeline_mode=pl.Buffered(1))
    carried = [] if out_prev is None else [out_prev]
    n_inputs = 10 + 2 * OUT_NCHUNK + 5
    return pl.pallas_call(
        functools.partial(_swa_merge_kernel, blocks_per_seq, n_tiles),
        grid=(n_tiles + 1,),
        input_output_aliases={} if out_prev is None else {n_inputs: 0},
        in_specs=(
            [smem, smem,
             pl.BlockSpec((SM_TM, ATTN_WIDTH), lambda t: (attn_tile(t), 0)),
             pl.BlockSpec((SM_TM, KV_WIDTH), lambda t: (attn_tile(t), kcol)),
             pl.BlockSpec((SM_TM, KV_WIDTH), lambda t: (attn_tile(t), vcol)),
             pl.BlockSpec((WINDOW, KV_WIDTH), lambda t: (prev_block(t), kcol)),
             pl.BlockSpec((WINDOW, KV_WIDTH), lambda t: (prev_block(t), vcol)),
             pl.BlockSpec((SM_TM, ATTN_WIDTH // 2), lambda t: (attn_tile(t), gcol)),
             pl.BlockSpec((SM_TM, ATTN_WIDTH // 2), lambda t: (attn_tile(t), gcol + 1)),
             pl.BlockSpec((SM_TM, HGRN_WIDTH), lambda t: (merge_tile(t), 0))]
            + [gate(OFF_GA, j) for j in range(OUT_NCHUNK)]
            + [gate(OFF_GH, j) for j in range(OUT_NCHUNK)]
            + [pl.BlockSpec((SM_TM, D_MODEL), lambda t: (global_tile(t), 0)),
               resident((ATTN_WIDTH, D_MODEL)), resident((HGRN_WIDTH, D_MODEL)),
               resident((D_MODEL, D_MODEL)), pl.BlockSpec((1, D_MODEL), const)]
            + [pl.BlockSpec(memory_space=pl.ANY) for _ in carried]),
        out_specs=pl.BlockSpec((SM_TM, D_MODEL), lambda t: (global_tile(t), 0)),
        out_shape=jax.ShapeDtypeStruct((tokens, D_MODEL), F32),
        scratch_shapes=[
            pltpu.VMEM((2, ATTN_HEADS, WINDOW, WINDOW), F32),
            pltpu.VMEM((2 * WINDOW, WINDOW), BF16),
            pltpu.VMEM((SM_TM, ATTN_WIDTH), BF16),
            pltpu.VMEM((SM_TM, D_MODEL), BF16),
        ],
        compiler_params=pltpu.CompilerParams(
            dimension_semantics=("arbitrary",), vmem_limit_bytes=52 * 1024 * 1024),
        name="swa_merge",
    )(rel_bias, sinks, proj, proj, proj, proj, proj, proj, proj, yh,
      *([proj] * (2 * OUT_NCHUNK)), x2d, wa, wh, wo, gain, *carried)


def kernel(x, norm_pre, w_in, rel_bias, attn_sinks, lb_logits, hgrn_norm, w_branch_attn,
           w_branch_hgrn, w_out, norm_post):
    batch, seq, d_model = x.shape
    depth = w_in.shape[0]
    assert depth == 1 and d_model == D_MODEL and seq % IN_TM == 0
    assert lb_logits.shape == (depth + 1, HGRN_WIDTH)
    assert batch % N_SEG == 0
    x2d = x.reshape(batch * seq, d_model)
    layer = 0
    seg_batch = batch // N_SEG
    seg_tokens = seg_batch * seq
    gain_pre = norm_pre[layer][None, :]
    w_bf16 = w_in[layer].astype(BF16)
    projs = [_inproj(x2d, gain_pre, w_bf16, 0, seg_tokens)]
    yhs = []
    for seg in range(1, N_SEG):
        proj, yh = _inproj_hgrn(x2d, gain_pre, w_bf16, seg, seg_tokens, projs[-1], lb_logits,
                                hgrn_norm[layer], seq)
        projs.append(proj)
        yhs.append(yh)
    yhs.append(_hgrn(projs[-1], lb_logits, hgrn_norm[layer], seg_batch, seq))
    wa = w_branch_attn[layer].astype(BF16)
    wh = w_branch_hgrn[layer].astype(BF16)
    wo = w_out[layer].astype(BF16)
    out = None
    for seg in range(N_SEG):
        out = _swa_merge(projs[seg], yhs[seg], x2d, out, seg, rel_bias, attn_sinks[layer][None, :],
                         wa, wh, wo, norm_post[layer][None, :], seq)
    return out.reshape(batch, seq, d_model)
```

```python
import functools
import math

import jax
import jax.numpy as jnp
from jax import lax
from jax.experimental import pallas as pl
from jax.experimental.pallas import tpu as pltpu

D_MODEL = 2048
ATTN_HEADS = 16
ATTN_KV_HEADS = 4
ATTN_HEAD_DIM = 64
ATTN_GROUP = ATTN_HEADS // ATTN_KV_HEADS
WINDOW = 128
ATTN_WIDTH = ATTN_HEADS * ATTN_HEAD_DIM
KV_WIDTH = ATTN_KV_HEADS * ATTN_HEAD_DIM
HGRN_HEADS = 8
HGRN_DIM = 128
HGRN_WIDTH = HGRN_HEADS * HGRN_DIM
CHUNK = 64
CHUNK_LEVELS = 6
HGRN_UNROLL = 4
HGRN_SLOTS = 6
HGRN_HEADS_PER_STEP = 2
N_SEG = 8
REL_BUCKETS = 32
REL_MAX_DIST = 128
NORM_EPS = 1e-6
IN_PROJ_WIDTH = 10752

OFF_AQ = 0
OFF_AK = OFF_AQ + ATTN_WIDTH
OFF_AV = OFF_AK + KV_WIDTH
OFF_AG = OFF_AV + KV_WIDTH
OFF_HQ = OFF_AG + ATTN_WIDTH
OFF_HF = OFF_HQ + HGRN_WIDTH
OFF_HI = OFF_HF + HGRN_WIDTH
OFF_HG = OFF_HI + HGRN_WIDTH
OFF_GA = OFF_HG + HGRN_WIDTH
OFF_GH = OFF_GA + D_MODEL

LANES = 128
SUBLANES = 8
MASK_VALUE = -1e30

F32 = jnp.float32
BF16 = jnp.bfloat16

_NT = (((1,), (1,)), ((), ()))
_TN = (((0,), (0,)), ((), ()))


def _sigmoid(x):
    return 1.0 / (1.0 + jnp.exp(-x))


def _silu(x):
    return x * _sigmoid(x)


IN_TM = 1024
IN_TN = 1536
IN_NORM_ROWS = 256
IN_PIECE = 256
IN_PIECES = IN_TN // IN_PIECE


def _prenorm(x_ref, gain_ref, h_ref):
    gain = gain_ref[...]
    for r in range(0, IN_TM, IN_NORM_ROWS):
        xf = x_ref[r:r + IN_NORM_ROWS, :]
        ms = jnp.mean(xf * xf, axis=-1, keepdims=True)
        h_ref[r:r + IN_NORM_ROWS, :] = (xf * lax.rsqrt(ms + NORM_EPS) * gain).astype(BF16)


HG_TILE0 = OFF_HQ // IN_TN
HG_TILES = (OFF_GA - 1) // IN_TN - HG_TILE0 + 1
HG_COL0 = HG_TILE0 * IN_TN


def _copy_hgrn_tile(o_ref, hg_ref):
    col_step = pl.program_id(1)

    @pl.when((col_step >= HG_TILE0) & (col_step < HG_TILE0 + HG_TILES))
    def _():
        hg_ref[...] = o_ref[...]


def _inproj_kernel(x_ref, gain_ref, w_ref, o_ref, hg_ref, h_ref):
    @pl.when(pl.program_id(1) == 0)
    def _():
        _prenorm(x_ref, gain_ref, h_ref)

    o_ref[...] = jnp.dot(h_ref[...], w_ref[...], preferred_element_type=F32).astype(BF16)
    _copy_hgrn_tile(o_ref, hg_ref)


def _inproj_specs(seg_row_tiles, seg):
    return [
        pl.BlockSpec((IN_TM, D_MODEL), lambda i, j: (seg * seg_row_tiles + i, 0)),
        pl.BlockSpec((1, D_MODEL), lambda i, j: (0, 0)),
        pl.BlockSpec((D_MODEL, IN_TN), lambda i, j: (0, j)),
    ]


def _inproj_out_specs(seg_row_tiles, seg):
    hg_tile = lambda j: jnp.clip(j - HG_TILE0, 0, HG_TILES - 1)
    return [pl.BlockSpec((IN_TM, IN_TN), lambda i, j: (seg * seg_row_tiles + i, j)),
            pl.BlockSpec((IN_TM, IN_TN), lambda i, j: (i, hg_tile(j)))]


def _inproj_out_shapes(tokens, seg_tokens):
    return [jax.ShapeDtypeStruct((tokens, IN_PROJ_WIDTH), BF16),
            jax.ShapeDtypeStruct((seg_tokens, HG_TILES * IN_TN), BF16)]


def _inproj(x2d, gain, w_bf16, seg, seg_tokens):
    seg_row_tiles = seg_tokens // IN_TM
    return pl.pallas_call(
        _inproj_kernel,
        grid=(seg_row_tiles, IN_PROJ_WIDTH // IN_TN),
        in_specs=_inproj_specs(seg_row_tiles, seg),
        out_specs=_inproj_out_specs(seg_row_tiles, seg),
        out_shape=_inproj_out_shapes(x2d.shape[0], seg_tokens),
        scratch_shapes=[pltpu.VMEM((IN_TM, D_MODEL), BF16)],
        compiler_params=pltpu.CompilerParams(
            dimension_semantics=("parallel", "arbitrary"),
            vmem_limit_bytes=56 * 1024 * 1024),
        name="inproj",
    )(x2d, gain, w_bf16)


def _level_reference(g_ref, level, sub):
    half = 1 << level
    blk = 2 * half

    def row_tile(r):
        return g_ref[pl.ds(r, SUBLANES, stride=0), :]

    if blk >= SUBLANES:
        return jnp.concatenate(
            [row_tile(b * blk + half) for b in range(CHUNK // blk) for _ in range(blk // SUBLANES)],
            axis=0)
    tiles = []
    for t in range(CHUNK // SUBLANES):
        acc = row_tile(t * SUBLANES + half)
        for b in range(1, SUBLANES // blk):
            acc = jnp.where(sub >= b * blk, row_tile(t * SUBLANES + b * blk + half), acc)
        tiles.append(acc)
    return jnp.concatenate(tiles, axis=0)


def _neg_abs(x):
    bits = lax.bitcast_convert_type(x, jnp.uint32) | jnp.uint32(0x80000000)
    return lax.bitcast_convert_type(bits, F32)


def _hgrn_kernel(n_carried, lb_ref, hn_ref, hq_ref, hf_ref, hi_ref, hg_ref, *rest):
    o_ref, *scratch = rest[n_carried:]
    for j in range(HGRN_HEADS_PER_STEP):
        cols = slice(j * HGRN_DIM, (j + 1) * HGRN_DIM)
        head = pl.program_id(1) * HGRN_HEADS_PER_STEP + j
        _hgrn_head(lb_ref.at[:, cols], hn_ref.at[pl.ds(head, 1), :], hq_ref.at[:, cols],
                   hf_ref.at[:, cols], hi_ref.at[:, cols], hg_ref.at[:, cols], o_ref.at[:, cols],
                   *scratch)


def _hgrn_head(lb_ref, hn_ref, hq_ref, hf_ref, hi_ref, hg_ref, o_ref,
               g_ref, qg_ref, attn_ref, upd_ref, keep_ref, state_ref, piece_fn=None, n_pieces=0):
    seq = hq_ref.shape[0]
    lg = lb_ref[...]
    ex = jnp.exp(lg - jnp.max(lg, axis=0, keepdims=True))
    lb = ex[0:1, :] / jnp.sum(ex, axis=0, keepdims=True)
    hn = hn_ref[...]

    rowi = lax.broadcasted_iota(jnp.int32, (CHUNK, HGRN_DIM), 0)
    sub = lax.broadcasted_iota(jnp.int32, (SUBLANES, HGRN_DIM), 0)
    ti = lax.broadcasted_iota(jnp.int32, (CHUNK, CHUNK), 0)
    si = lax.broadcasted_iota(jnp.int32, (CHUNK, CHUNK), 1)
    xor = ti ^ si
    lvl = jnp.full((CHUNK, CHUNK), -2, jnp.int32)
    for level in range(CHUNK_LEVELS):
        lvl = jnp.where(xor >= (1 << level), level, lvl)
    lvl = jnp.where(ti > si, lvl, jnp.where(ti == si, -1, -2))

    def chunk_rows(c):
        start = c * CHUNK
        return pl.ds(start if isinstance(c, int) else pl.multiple_of(start, CHUNK), CHUNK)

    def local_step(c, slot):
        rows = chunk_rows(c)
        q = _silu(hq_ref[rows, :].astype(F32))
        f = lb + (1.0 - lb) * _sigmoid(hf_ref[rows, :].astype(F32))
        k = 1.0 - f
        v = hi_ref[rows, :]
        g = jnp.log2(f)
        for shift in (1, 2, 4):
            g = g + jnp.where(rowi >= shift, pltpu.roll(g, shift, 0), 0.0)
        for shift in (8, 16, 32):
            g = g + jnp.concatenate(
                [jnp.zeros((shift, HGRN_DIM), F32), g[:CHUNK - shift]], axis=0)
        g_slot = g_ref.at[slot]
        g_slot[...] = g
        g_last = g_slot[pl.ds(CHUNK - 1, SUBLANES, stride=0), :][0:1]
        qb = q.astype(BF16)
        kb = k.astype(BF16)
        qg_ref[rows, :] = qb * jnp.exp2(g).astype(BF16)

        attn = jnp.where(lvl == -1, jnp.sum(q * k, axis=-1, keepdims=True), 0.0)
        for level in range(CHUNK_LEVELS):
            dec = jnp.exp2(_neg_abs(g - _level_reference(g_slot, level, sub))).astype(BF16)
            a = lax.dot_general(qb * dec, kb * dec, _NT, preferred_element_type=F32)
            attn = jnp.where(lvl == level, a, attn)
        attn_ref[c] = attn.astype(BF16)

        k_dec = kb * jnp.exp2(g_last - g).astype(BF16)
        upd_ref[c] = lax.dot_general(v, k_dec, _TN, preferred_element_type=F32)
        keep_ref[c] = jnp.broadcast_to(jnp.exp2(g_last), (SUBLANES, HGRN_DIM))

    def local_group(i, carry):
        for slot in range(HGRN_UNROLL):
            local_step(i * HGRN_UNROLL + slot, slot)
        return carry

    n_chunks = seq // CHUNK
    if n_pieces:
        bounds = [(p * n_chunks) // n_pieces for p in range(n_pieces + 1)]
        for p in range(n_pieces):
            piece_fn(p)
            for c in range(bounds[p], bounds[p + 1]):
                local_step(c, c - bounds[p])
    else:
        lax.fori_loop(0, n_chunks // HGRN_UNROLL, local_group, 0)

    def state_step(c, state_t):
        state_ref[c] = state_t.astype(BF16)
        return keep_ref[c][0:1] * state_t + upd_ref[c]

    lax.fori_loop(0, n_chunks, state_step, jnp.zeros((HGRN_DIM, HGRN_DIM), F32),
                  unroll=HGRN_UNROLL)

    def out_step(c, carry):
        rows = chunk_rows(c)
        o = lax.dot_general(qg_ref[rows, :], state_ref[c], _NT, preferred_element_type=F32)
        o = o + jnp.dot(attn_ref[c], hi_ref[rows, :], preferred_element_type=F32)
        ms = jnp.mean(o * o, axis=-1, keepdims=True)
        y = o * lax.rsqrt(ms + NORM_EPS) * hn * _silu(hg_ref[rows, :].astype(F32))
        o_ref[rows, :] = y.astype(BF16)
        return carry

    lax.fori_loop(0, n_chunks, out_step, 0, unroll=2 * HGRN_UNROLL)


def _hgrn_scratch(seq):
    n_chunks = seq // CHUNK
    return [
        pltpu.VMEM((HGRN_SLOTS, CHUNK, HGRN_DIM), F32),
        pltpu.VMEM((seq, HGRN_DIM), BF16),
        pltpu.VMEM((n_chunks, CHUNK, CHUNK), BF16),
        pltpu.VMEM((n_chunks, HGRN_DIM, HGRN_DIM), F32),
        pltpu.VMEM((n_chunks, SUBLANES, HGRN_DIM), F32),
        pltpu.VMEM((n_chunks, HGRN_DIM, HGRN_DIM), BF16),
    ]


def _hgrn(hg_seg, lb_logits, hgrn_norm, yh_prev, seg, tokens, seq):
    width = HGRN_HEADS_PER_STEP * HGRN_DIM
    seg_batch = hg_seg.shape[0] // seq
    blk = lambda off: pl.BlockSpec((seq, width), lambda b, h: (b, (off - HG_COL0) // width + h))
    carried = [] if yh_prev is None else [yh_prev]
    return pl.pallas_call(
        functools.partial(_hgrn_kernel, len(carried)),
        grid=(seg_batch, HGRN_HEADS // HGRN_HEADS_PER_STEP),
        in_specs=[
            pl.BlockSpec((lb_logits.shape[0], width), lambda b, h: (0, h)),
            pl.BlockSpec((HGRN_HEADS, HGRN_DIM), lambda b, h: (0, 0)),
            blk(OFF_HQ), blk(OFF_HF), blk(OFF_HI), blk(OFF_HG),
        ] + [pl.BlockSpec(memory_space=pl.ANY) for _ in carried],
        out_specs=pl.BlockSpec((seq, width), lambda b, h: (seg * seg_batch + b, h)),
        out_shape=jax.ShapeDtypeStruct((tokens, HGRN_WIDTH), BF16),
        input_output_aliases={6: 0} if carried else {},
        scratch_shapes=_hgrn_scratch(seq),
        compiler_params=pltpu.CompilerParams(dimension_semantics=("parallel", "parallel")),
        name="hgrn2",
    )(lb_logits, hgrn_norm, hg_seg, hg_seg, hg_seg, hg_seg, *carried)


def _inproj_hgrn_kernel(heads_per_tile, n_carried, x_ref, gain_ref, w_ref, lb_ref, hn_ref, hq_ref,
                        hf_ref, hi_ref, hg_ref, *rest):
    o_ref, hg_out_ref, yh_ref, h_ref, *hgrn_scratch = rest[n_carried:]
    col_step = pl.program_id(1)

    @pl.when(col_step == 0)
    def _():
        _prenorm(x_ref, gain_ref, h_ref)

    def piece(p):
        cs = slice(p * IN_PIECE, (p + 1) * IN_PIECE)
        o_ref[:, cs] = jnp.dot(h_ref[...], w_ref[:, cs], preferred_element_type=F32).astype(BF16)

    @pl.when(col_step < heads_per_tile)
    def _():
        tiles_per_seq = HGRN_HEADS // heads_per_tile
        head = (pl.program_id(0) % tiles_per_seq) * heads_per_tile + col_step
        _hgrn_head(lb_ref, hn_ref.at[pl.ds(head, 1), :], hq_ref, hf_ref, hi_ref, hg_ref, yh_ref,
                   *hgrn_scratch, piece_fn=piece, n_pieces=IN_PIECES)

    @pl.when(col_step >= heads_per_tile)
    def _():
        o_ref[...] = jnp.dot(h_ref[...], w_ref[...], preferred_element_type=F32).astype(BF16)

    _copy_hgrn_tile(o_ref, hg_out_ref)


def _inproj_hgrn(x2d, gain, w_bf16, seg, hg_prev, proj_prev, yh_prev, lb_logits, hgrn_norm, seq):
    tokens = x2d.shape[0]
    seg_tokens = hg_prev.shape[0]
    seg_row_tiles = seg_tokens // IN_TM
    tiles_per_seq = seq // IN_TM
    seg_batch = seg_row_tiles // tiles_per_seq
    heads_per_tile = HGRN_HEADS // tiles_per_seq
    assert heads_per_tile <= IN_PROJ_WIDTH // IN_TN
    assert pl.cdiv(seq // CHUNK, IN_PIECES) <= HGRN_SLOTS

    def head_of(i, j):
        return (i % tiles_per_seq) * heads_per_tile + jnp.minimum(j, heads_per_tile - 1)

    def blk(off):
        return pl.BlockSpec((seq, HGRN_DIM), lambda i, j: (
            i // tiles_per_seq, (off - HG_COL0) // HGRN_DIM + head_of(i, j)))

    carried = [proj_prev] + ([] if yh_prev is None else [yh_prev])
    n_read = 9
    aliases = {n_read: 0} if yh_prev is None else {n_read: 0, n_read + 1: 2}
    return pl.pallas_call(
        functools.partial(_inproj_hgrn_kernel, heads_per_tile, len(carried)),
        grid=(seg_row_tiles, IN_PROJ_WIDTH // IN_TN),
        in_specs=_inproj_specs(seg_row_tiles, seg) + [
            pl.BlockSpec((lb_logits.shape[0], HGRN_DIM), lambda i, j: (0, head_of(i, j))),
            pl.BlockSpec((HGRN_HEADS, HGRN_DIM), lambda i, j: (0, 0)),
            blk(OFF_HQ), blk(OFF_HF), blk(OFF_HI), blk(OFF_HG),
        ] + [pl.BlockSpec(memory_space=pl.ANY) for _ in carried],
        out_specs=_inproj_out_specs(seg_row_tiles, seg) + [
            pl.BlockSpec((seq, HGRN_DIM), lambda i, j: (
                (seg - 1) * seg_batch + i // tiles_per_seq, head_of(i, j)))],
        out_shape=_inproj_out_shapes(tokens, seg_tokens) + [
            jax.ShapeDtypeStruct((tokens, HGRN_WIDTH), BF16)],
        input_output_aliases=aliases,
        scratch_shapes=[pltpu.VMEM((IN_TM, D_MODEL), BF16)] + _hgrn_scratch(seq),
        compiler_params=pltpu.CompilerParams(
            dimension_semantics=("arbitrary", "arbitrary"),
            vmem_limit_bytes=58 * 1024 * 1024),
        name="inproj_hgrn2",
    )(x2d, gain, w_bf16, lb_logits, hgrn_norm, hg_prev, hg_prev, hg_prev, hg_prev, *carried)


SM_BLOCKS = 2
SM_TM = SM_BLOCKS * WINDOW
OUT_TN = 512
OUT_NCHUNK = D_MODEL // OUT_TN


def _t5_bucket(dist):
    max_exact = REL_BUCKETS // 2
    d = jnp.maximum(dist, 0)
    df = jnp.maximum(d, 1).astype(F32)
    large = max_exact + (jnp.log(df / max_exact) / math.log(REL_MAX_DIST / max_exact)
                         * (REL_BUCKETS - max_exact)).astype(jnp.int32)
    large = jnp.minimum(large, REL_BUCKETS - 1)
    return jnp.where(d < max_exact, d, large)


def _swap_halves(x):
    half = LANES // 2
    return jnp.concatenate([x[:, half:], x[:, :half]], axis=1)


def _swa_tables(relb_ref, bias_ref, cur_ref):
    qi = lax.broadcasted_iota(jnp.int32, (WINDOW, WINDOW), 0)
    sj = lax.broadcasted_iota(jnp.int32, (WINDOW, WINDOW), 1)
    own = sj <= qi
    bucket = _t5_bucket(jnp.where(own, qi - sj, qi + WINDOW - sj))
    for h in range(ATTN_HEADS):
        def pick(b, acc, h=h):
            return jnp.where(bucket == b, relb_ref[b, h], acc)
        tbl = lax.fori_loop(0, REL_BUCKETS, pick, jnp.zeros((WINDOW, WINDOW), F32))
        bias_ref[0, h] = tbl
        bias_ref[1, h] = jnp.where(own, tbl, MASK_VALUE)
    own_bf = jnp.where(own, 1.0, 0.0).astype(BF16)
    cur_ref[...] = jnp.concatenate([own_bf, own_bf], axis=0)


def _swa_merge_kernel(blocks_per_seq, n_tiles, relb_ref, sink_ref, q_ref, kc_ref, vc_ref, kp_ref,
                      vp_ref, ga_ref, gb_ref, yh_ref, *rest):
    gate_a_refs = rest[:OUT_NCHUNK]
    gate_h_refs = rest[OUT_NCHUNK:2 * OUT_NCHUNK]
    (x_ref, wa_ref, wh_ref, wo_ref, gain_ref, o_ref,
     bias_ref, cur_ref, ya_ref, m_ref) = rest[2 * OUT_NCHUNK:]
    step = pl.program_id(0)

    @pl.when(step == 0)
    def _():
        _swa_tables(relb_ref, bias_ref, cur_ref)
        ya_ref[...] = jnp.zeros_like(ya_ref)

    lane = lax.broadcasted_iota(jnp.int32, (1, LANES), 1)
    is_lo = lane < LANES // 2
    scale = ATTN_HEAD_DIM ** -0.5
    keep_lo = jnp.where(is_lo, 1.0, 0.0).astype(BF16)
    keep_hi = jnp.where(is_lo, 0.0, 1.0).astype(BF16)
    row = lax.broadcasted_iota(jnp.int32, (2 * WINDOW, 1), 0)
    top = row < WINDOW
    own2 = (lax.broadcasted_iota(jnp.int32, (2 * WINDOW, WINDOW), 1)
            <= (lax.broadcasted_iota(jnp.int32, (2 * WINDOW, WINDOW), 0) & (WINDOW - 1)))
    tile = jnp.minimum(step, n_tiles - 1)
    tile_starts_seq = jnp.where(tile % (blocks_per_seq // SM_BLOCKS) == 0, 1, 0)

    def scores(qm, km, first, heads):
        s2 = lax.dot_general(qm, km, _NT, preferred_element_type=F32)
        bias = jnp.concatenate([bias_ref[first, heads[0]], bias_ref[first, heads[1]]], axis=0)
        return jnp.where(own2, s2[:, WINDOW:], s2[:, :WINDOW]) + bias

    def probs(s, heads):
        sink = jnp.where(top, sink_ref[0, heads[0]], sink_ref[0, heads[1]])
        m = jnp.maximum(jnp.max(s, axis=-1, keepdims=True), sink)
        p = jnp.exp(s - m)
        denom = jnp.sum(p, axis=-1, keepdims=True) + jnp.exp(sink - m)
        pb = p.astype(BF16)
        p_own = pb * cur_ref[...]
        return jnp.concatenate([pb - p_own, p_own], axis=1), 1.0 / denom

    def heads_of(hk):
        e = hk % 2
        return (4 * hk + e, 4 * hk + 2 + e), (4 * hk + 1 - e, 4 * hk + 3 - e)

    s_same, s_diff, vals = [], [], []
    for blk in range(SM_BLOCKS):
        rows = slice(blk * WINDOW, (blk + 1) * WINDOW)
        first = tile_starts_seq if blk == 0 else 0
        for hk in range(ATTN_KV_HEADS):
            c, e = divmod(hk, 2)
            cs = slice(c * LANES, (c + 1) * LANES)
            if blk == 0:
                k_prev, v_prev = kp_ref[:, cs], vp_ref[:, cs]
            else:
                prev_rows = slice((blk - 1) * WINDOW, blk * WINDOW)
                k_prev, v_prev = kc_ref[prev_rows, cs], vc_ref[prev_rows, cs]
            k2 = jnp.concatenate([k_prev, kc_ref[rows, cs]], axis=0)
            v2 = jnp.concatenate([v_prev, vc_ref[rows, cs]], axis=0)
            k2s, v2s = _swap_halves(k2), _swap_halves(v2)
            keep_e, keep_o = (keep_lo, keep_hi) if e == 0 else (keep_hi, keep_lo)
            p0 = q_ref[rows, hk * 2 * LANES:hk * 2 * LANES + LANES]
            p1 = q_ref[rows, hk * 2 * LANES + LANES:(hk + 1) * 2 * LANES]
            q_same = jnp.concatenate([p0 * (keep_e * scale), p1 * (keep_e * scale)], axis=0)
            q_diff = jnp.concatenate([p0 * (keep_o * scale), p1 * (keep_o * scale)], axis=0)
            h_same, h_diff = heads_of(hk)
            s_same.append(scores(q_same, k2, first, h_same))
            s_diff.append(scores(q_diff, k2s, first, h_diff))
            vals.append((v2 * keep_lo, v2s * keep_hi) if e == 0 else (v2s * keep_lo, v2 * keep_hi))

    ya_prev = ya_ref[...]
    yh = yh_ref[...]
    for j in range(OUT_NCHUNK):
        cs = slice(j * OUT_TN, (j + 1) * OUT_TN)
        ua = jnp.dot(ya_prev, wa_ref[:, cs], preferred_element_type=F32)
        uh = jnp.dot(yh, wh_ref[:, cs], preferred_element_type=F32)
        merged = (_sigmoid(gate_a_refs[j][...].astype(F32)) * ua
                  + _sigmoid(gate_h_refs[j][...].astype(F32)) * uh)
        m_ref[:, cs] = merged.astype(BF16)

    pr_same = [probs(s_same[i], heads_of(i % ATTN_KV_HEADS)[0]) for i in range(len(s_same))]
    pr_diff = [probs(s_diff[i], heads_of(i % ATTN_KV_HEADS)[1]) for i in range(len(s_diff))]
    for blk in range(SM_BLOCKS):
        rows = slice(blk * WINDOW, (blk + 1) * WINDOW)
        for hk in range(ATTN_KV_HEADS):
            i = blk * ATTN_KV_HEADS + hk
            (p_lo, r_lo), (p_hi, r_hi) = ((pr_same[i], pr_diff[i]) if hk % 2 == 0
                                          else (pr_diff[i], pr_same[i]))
            v_lo, v_hi = vals[i]
            pv = jnp.dot(jnp.concatenate([p_lo, p_hi], axis=1),
                         jnp.concatenate([v_lo, v_hi], axis=0),
                         preferred_element_type=F32)
            pv = pv * jnp.where(is_lo, r_lo, r_hi)
            g_ref = ga_ref if hk < 2 else gb_ref
            for pair in range(2):
                gs = slice((hk % 2) * 2 * LANES + pair * LANES,
                           (hk % 2) * 2 * LANES + (pair + 1) * LANES)
                gate = _silu(g_ref[rows, gs].astype(F32))
                ya_ref[rows, (2 * hk + pair) * LANES:(2 * hk + pair + 1) * LANES] = (
                    pv[pair * WINDOW:(pair + 1) * WINDOW] * gate).astype(BF16)

    y = jnp.dot(m_ref[...], wo_ref[...], preferred_element_type=F32)
    ms = jnp.mean(y * y, axis=-1, keepdims=True)
    o_ref[...] = x_ref[...] + y * lax.rsqrt(ms + NORM_EPS) * gain_ref[...]


def _swa_merge(proj, yh, x2d, rel_bias, sinks, wa, wh, wo, gain, seq):
    tokens = x2d.shape[0]
    n_tiles = tokens // SM_TM
    blocks_per_seq = seq // WINDOW
    attn_tile = lambda t: jnp.minimum(t, n_tiles - 1)
    merge_tile = lambda t: jnp.maximum(t - 1, 0)
    prev_block = lambda t: jnp.maximum(attn_tile(t) * SM_BLOCKS - 1, 0)
    kcol, vcol = OFF_AK // KV_WIDTH, OFF_AV // KV_WIDTH
    gcol = OFF_AG // (ATTN_WIDTH // 2)
    const = lambda t: (0, 0)
    smem = pl.BlockSpec(memory_space=pltpu.SMEM)
    gate = lambda off, j: pl.BlockSpec((SM_TM, OUT_TN), lambda t: (merge_tile(t), off // OUT_TN + j))
    resident = functools.partial(pl.BlockSpec, index_map=const, pipeline_mode=pl.Buffered(1))
    return pl.pallas_call(
        functools.partial(_swa_merge_kernel, blocks_per_seq, n_tiles),
        grid=(n_tiles + 1,),
        in_specs=(
            [smem, smem,
             pl.BlockSpec((SM_TM, ATTN_WIDTH), lambda t: (attn_tile(t), 0)),
             pl.BlockSpec((SM_TM, KV_WIDTH), lambda t: (attn_tile(t), kcol)),
             pl.BlockSpec((SM_TM, KV_WIDTH), lambda t: (attn_tile(t), vcol)),
             pl.BlockSpec((WINDOW, KV_WIDTH), lambda t: (prev_block(t), kcol)),
             pl.BlockSpec((WINDOW, KV_WIDTH), lambda t: (prev_block(t), vcol)),
             pl.BlockSpec((SM_TM, ATTN_WIDTH // 2), lambda t: (attn_tile(t), gcol)),
             pl.BlockSpec((SM_TM, ATTN_WIDTH // 2), lambda t: (attn_tile(t), gcol + 1)),
             pl.BlockSpec((SM_TM, HGRN_WIDTH), lambda t: (merge_tile(t), 0))]
            + [gate(OFF_GA, j) for j in range(OUT_NCHUNK)]
            + [gate(OFF_GH, j) for j in range(OUT_NCHUNK)]
            + [pl.BlockSpec((SM_TM, D_MODEL), lambda t: (merge_tile(t), 0)),
               resident((ATTN_WIDTH, D_MODEL)), resident((HGRN_WIDTH, D_MODEL)),
               resident((D_MODEL, D_MODEL)), pl.BlockSpec((1, D_MODEL), const)]),
        out_specs=pl.BlockSpec((SM_TM, D_MODEL), lambda t: (merge_tile(t), 0)),
        out_shape=jax.ShapeDtypeStruct((tokens, D_MODEL), F32),
        scratch_shapes=[
            pltpu.VMEM((2, ATTN_HEADS, WINDOW, WINDOW), F32),
            pltpu.VMEM((2 * WINDOW, WINDOW), BF16),
            pltpu.VMEM((SM_TM, ATTN_WIDTH), BF16),
            pltpu.VMEM((SM_TM, D_MODEL), BF16),
        ],
        compiler_params=pltpu.CompilerParams(
            dimension_semantics=("arbitrary",), vmem_limit_bytes=52 * 1024 * 1024),
        name="swa_merge",
    )(rel_bias, sinks, proj, proj, proj, proj, proj, proj, proj, yh,
      *([proj] * (2 * OUT_NCHUNK)), x2d, wa, wh, wo, gain)


def kernel(x, norm_pre, w_in, rel_bias, attn_sinks, lb_logits, hgrn_norm, w_branch_attn,
           w_branch_hgrn, w_out, norm_post):
    batch, seq, d_model = x.shape
    depth = w_in.shape[0]
    assert depth == 1 and d_model == D_MODEL and seq % IN_TM == 0
    assert lb_logits.shape == (depth + 1, HGRN_WIDTH)
    assert batch % N_SEG == 0
    x2d = x.reshape(batch * seq, d_model)
    layer = 0
    seg_batch = batch // N_SEG
    seg_tokens = seg_batch * seq
    gain_pre = norm_pre[layer][None, :]
    w_bf16 = w_in[layer].astype(BF16)
    proj, hg = _inproj(x2d, gain_pre, w_bf16, 0, seg_tokens)
    yh = None
    for seg in range(1, N_SEG):
        proj, hg, yh = _inproj_hgrn(x2d, gain_pre, w_bf16, seg, hg, proj, yh, lb_logits,
                                    hgrn_norm[layer], seq)
    yh = _hgrn(hg, lb_logits, hgrn_norm[layer], yh, N_SEG - 1, batch * seq, seq)
    out = _swa_merge(proj, yh, x2d, rel_bias, attn_sinks[layer][None, :],
                     w_branch_attn[layer].astype(BF16), w_branch_hgrn[layer].astype(BF16),
                     w_out[layer].astype(BF16), norm_post[layer][None, :], seq)
    return out.reshape(batch, seq, d_model)
```

```python
import functools
import math

import jax
import jax.numpy as jnp
from jax import lax
from jax.experimental import pallas as pl
from jax.experimental.pallas import tpu as pltpu

D_MODEL = 2048
ATTN_HEADS = 16
ATTN_KV_HEADS = 4
ATTN_HEAD_DIM = 64
ATTN_GROUP = ATTN_HEADS // ATTN_KV_HEADS
WINDOW = 128
ATTN_WIDTH = ATTN_HEADS * ATTN_HEAD_DIM
KV_WIDTH = ATTN_KV_HEADS * ATTN_HEAD_DIM
HGRN_HEADS = 8
HGRN_DIM = 128
HGRN_WIDTH = HGRN_HEADS * HGRN_DIM
CHUNK = 64
CHUNK_LEVELS = 6
HGRN_UNROLL = 4
HGRN_SLOTS = 6
HGRN_HEADS_PER_STEP = 2
N_SEG = 16
REL_BUCKETS = 32
REL_MAX_DIST = 128
NORM_EPS = 1e-6
IN_PROJ_WIDTH = 10752

OFF_AQ = 0
OFF_AK = OFF_AQ + ATTN_WIDTH
OFF_AV = OFF_AK + KV_WIDTH
OFF_AG = OFF_AV + KV_WIDTH
OFF_HQ = OFF_AG + ATTN_WIDTH
OFF_HF = OFF_HQ + HGRN_WIDTH
OFF_HI = OFF_HF + HGRN_WIDTH
OFF_HG = OFF_HI + HGRN_WIDTH
OFF_GA = OFF_HG + HGRN_WIDTH
OFF_GH = OFF_GA + D_MODEL

LANES = 128
SUBLANES = 8
MASK_VALUE = -1e30

F32 = jnp.float32
BF16 = jnp.bfloat16

_NT = (((1,), (1,)), ((), ()))
_TN = (((0,), (0,)), ((), ()))


def _sigmoid(x):
    return 1.0 / (1.0 + jnp.exp(-x))


def _silu(x):
    return x * _sigmoid(x)


IN_TM = 1024
IN_TN = 1536
IN_NORM_ROWS = 256
IN_PIECE = 256
IN_PIECES = IN_TN // IN_PIECE


def _prenorm(x_ref, gain_ref, h_ref):
    gain = gain_ref[...]
    for r in range(0, IN_TM, IN_NORM_ROWS):
        xf = x_ref[r:r + IN_NORM_ROWS, :]
        ms = jnp.mean(xf * xf, axis=-1, keepdims=True)
        h_ref[r:r + IN_NORM_ROWS, :] = (xf * lax.rsqrt(ms + NORM_EPS) * gain).astype(BF16)


HG_TILE0 = OFF_HQ // IN_TN
HG_TILES = (OFF_GA - 1) // IN_TN - HG_TILE0 + 1
HG_COL0 = HG_TILE0 * IN_TN


def _copy_hgrn_tile(o_ref, hg_ref):
    col_step = pl.program_id(1)

    @pl.when((col_step >= HG_TILE0) & (col_step < HG_TILE0 + HG_TILES))
    def _():
        hg_ref[...] = o_ref[...]


def _inproj_kernel(x_ref, gain_ref, w_ref, o_ref, hg_ref, h_ref):
    @pl.when(pl.program_id(1) == 0)
    def _():
        _prenorm(x_ref, gain_ref, h_ref)

    o_ref[...] = jnp.dot(h_ref[...], w_ref[...], preferred_element_type=F32).astype(BF16)
    _copy_hgrn_tile(o_ref, hg_ref)


def _inproj_specs(seg_row_tiles, seg):
    return [
        pl.BlockSpec((IN_TM, D_MODEL), lambda i, j: (seg * seg_row_tiles + i, 0)),
        pl.BlockSpec((1, D_MODEL), lambda i, j: (0, 0)),
        pl.BlockSpec((D_MODEL, IN_TN), lambda i, j: (0, j)),
    ]


def _inproj_out_specs(seg_row_tiles, seg):
    hg_tile = lambda j: jnp.clip(j - HG_TILE0, 0, HG_TILES - 1)
    return [pl.BlockSpec((IN_TM, IN_TN), lambda i, j: (seg * seg_row_tiles + i, j)),
            pl.BlockSpec((IN_TM, IN_TN), lambda i, j: (i, hg_tile(j)))]


def _inproj_out_shapes(tokens, seg_tokens):
    return [jax.ShapeDtypeStruct((tokens, IN_PROJ_WIDTH), BF16),
            jax.ShapeDtypeStruct((seg_tokens, HG_TILES * IN_TN), BF16)]


def _inproj(x2d, gain, w_bf16, seg, seg_tokens):
    seg_row_tiles = seg_tokens // IN_TM
    return pl.pallas_call(
        _inproj_kernel,
        grid=(seg_row_tiles, IN_PROJ_WIDTH // IN_TN),
        in_specs=_inproj_specs(seg_row_tiles, seg),
        out_specs=_inproj_out_specs(seg_row_tiles, seg),
        out_shape=_inproj_out_shapes(x2d.shape[0], seg_tokens),
        scratch_shapes=[pltpu.VMEM((IN_TM, D_MODEL), BF16)],
        compiler_params=pltpu.CompilerParams(
            dimension_semantics=("parallel", "arbitrary"),
            vmem_limit_bytes=56 * 1024 * 1024),
        name="inproj",
    )(x2d, gain, w_bf16)


def _level_reference(g_ref, level, sub):
    half = 1 << level
    blk = 2 * half

    def row_tile(r):
        return g_ref[pl.ds(r, SUBLANES, stride=0), :]

    if blk >= SUBLANES:
        return jnp.concatenate(
            [row_tile(b * blk + half) for b in range(CHUNK // blk) for _ in range(blk // SUBLANES)],
            axis=0)
    tiles = []
    for t in range(CHUNK // SUBLANES):
        acc = row_tile(t * SUBLANES + half)
        for b in range(1, SUBLANES // blk):
            acc = jnp.where(sub >= b * blk, row_tile(t * SUBLANES + b * blk + half), acc)
        tiles.append(acc)
    return jnp.concatenate(tiles, axis=0)


def _neg_abs(x):
    bits = lax.bitcast_convert_type(x, jnp.uint32) | jnp.uint32(0x80000000)
    return lax.bitcast_convert_type(bits, F32)


def _hgrn_kernel(n_carried, lb_ref, hn_ref, hq_ref, hf_ref, hi_ref, hg_ref, *rest):
    o_ref, *scratch = rest[n_carried:]
    for j in range(HGRN_HEADS_PER_STEP):
        cols = slice(j * HGRN_DIM, (j + 1) * HGRN_DIM)
        head = pl.program_id(1) * HGRN_HEADS_PER_STEP + j
        _hgrn_head(lb_ref.at[:, cols], hn_ref.at[pl.ds(head, 1), :], hq_ref.at[:, cols],
                   hf_ref.at[:, cols], hi_ref.at[:, cols], hg_ref.at[:, cols], o_ref.at[:, cols],
                   *scratch)


def _hgrn_head(lb_ref, hn_ref, hq_ref, hf_ref, hi_ref, hg_ref, o_ref,
               g_ref, qg_ref, attn_ref, upd_ref, keep_ref, state_ref, piece_fn=None, n_pieces=0):
    seq = hq_ref.shape[0]
    lg = lb_ref[...]
    ex = jnp.exp(lg - jnp.max(lg, axis=0, keepdims=True))
    lb = ex[0:1, :] / jnp.sum(ex, axis=0, keepdims=True)
    hn = hn_ref[...]

    rowi = lax.broadcasted_iota(jnp.int32, (CHUNK, HGRN_DIM), 0)
    sub = lax.broadcasted_iota(jnp.int32, (SUBLANES, HGRN_DIM), 0)
    ti = lax.broadcasted_iota(jnp.int32, (CHUNK, CHUNK), 0)
    si = lax.broadcasted_iota(jnp.int32, (CHUNK, CHUNK), 1)
    xor = ti ^ si
    lvl = jnp.full((CHUNK, CHUNK), -2, jnp.int32)
    for level in range(CHUNK_LEVELS):
        lvl = jnp.where(xor >= (1 << level), level, lvl)
    lvl = jnp.where(ti > si, lvl, jnp.where(ti == si, -1, -2))

    def chunk_rows(c):
        start = c * CHUNK
        return pl.ds(start if isinstance(c, int) else pl.multiple_of(start, CHUNK), CHUNK)

    def local_step(c, slot):
        rows = chunk_rows(c)
        q = _silu(hq_ref[rows, :].astype(F32))
        f = lb + (1.0 - lb) * _sigmoid(hf_ref[rows, :].astype(F32))
        k = 1.0 - f
        v = hi_ref[rows, :]
        g = jnp.log2(f)
        for shift in (1, 2, 4):
            g = g + jnp.where(rowi >= shift, pltpu.roll(g, shift, 0), 0.0)
        for shift in (8, 16, 32):
            g = g + jnp.concatenate(
                [jnp.zeros((shift, HGRN_DIM), F32), g[:CHUNK - shift]], axis=0)
        g_slot = g_ref.at[slot]
        g_slot[...] = g
        g_last = g_slot[pl.ds(CHUNK - 1, SUBLANES, stride=0), :][0:1]
        qb = q.astype(BF16)
        kb = k.astype(BF16)
        qg_ref[rows, :] = qb * jnp.exp2(g).astype(BF16)

        attn = jnp.where(lvl == -1, jnp.sum(q * k, axis=-1, keepdims=True), 0.0)
        for level in range(CHUNK_LEVELS):
            dec = jnp.exp2(_neg_abs(g - _level_reference(g_slot, level, sub))).astype(BF16)
            a = lax.dot_general(qb * dec, kb * dec, _NT, preferred_element_type=F32)
            attn = jnp.where(lvl == level, a, attn)
        attn_ref[c] = attn.astype(BF16)

        k_dec = kb * jnp.exp2(g_last - g).astype(BF16)
        upd_ref[c] = lax.dot_general(v, k_dec, _TN, preferred_element_type=F32)
        keep_ref[c] = jnp.broadcast_to(jnp.exp2(g_last), (SUBLANES, HGRN_DIM))

    def local_group(i, carry):
        for slot in range(HGRN_UNROLL):
            local_step(i * HGRN_UNROLL + slot, slot)
        return carry

    n_chunks = seq // CHUNK
    if n_pieces:
        bounds = [(p * n_chunks) // n_pieces for p in range(n_pieces + 1)]
        for p in range(n_pieces):
            piece_fn(p)
            for c in range(bounds[p], bounds[p + 1]):
                local_step(c, c - bounds[p])
    else:
        lax.fori_loop(0, n_chunks // HGRN_UNROLL, local_group, 0)

    def state_step(c, state_t):
        state_ref[c] = state_t.astype(BF16)
        return keep_ref[c][0:1] * state_t + upd_ref[c]

    lax.fori_loop(0, n_chunks, state_step, jnp.zeros((HGRN_DIM, HGRN_DIM), F32),
                  unroll=HGRN_UNROLL)

    def out_step(c, carry):
        rows = chunk_rows(c)
        o = lax.dot_general(qg_ref[rows, :], state_ref[c], _NT, preferred_element_type=F32)
        o = o + jnp.dot(attn_ref[c], hi_ref[rows, :], preferred_element_type=F32)
        ms = jnp.mean(o * o, axis=-1, keepdims=True)
        y = o * lax.rsqrt(ms + NORM_EPS) * hn * _silu(hg_ref[rows, :].astype(F32))
        o_ref[rows, :] = y.astype(BF16)
        return carry

    lax.fori_loop(0, n_chunks, out_step, 0, unroll=2 * HGRN_UNROLL)


def _hgrn_scratch(seq):
    n_chunks = seq // CHUNK
    return [
        pltpu.VMEM((HGRN_SLOTS, CHUNK, HGRN_DIM), F32),
        pltpu.VMEM((seq, HGRN_DIM), BF16),
        pltpu.VMEM((n_chunks, CHUNK, CHUNK), BF16),
        pltpu.VMEM((n_chunks, HGRN_DIM, HGRN_DIM), F32),
        pltpu.VMEM((n_chunks, SUBLANES, HGRN_DIM), F32),
        pltpu.VMEM((n_chunks, HGRN_DIM, HGRN_DIM), BF16),
    ]


def _hgrn(hg_seg, lb_logits, hgrn_norm, yh_prev, seg, tokens, seq):
    width = HGRN_HEADS_PER_STEP * HGRN_DIM
    seg_batch = hg_seg.shape[0] // seq
    blk = lambda off: pl.BlockSpec((seq, width), lambda b, h: (b, (off - HG_COL0) // width + h))
    carried = [] if yh_prev is None else [yh_prev]
    return pl.pallas_call(
        functools.partial(_hgrn_kernel, len(carried)),
        grid=(seg_batch, HGRN_HEADS // HGRN_HEADS_PER_STEP),
        in_specs=[
            pl.BlockSpec((lb_logits.shape[0], width), lambda b, h: (0, h)),
            pl.BlockSpec((HGRN_HEADS, HGRN_DIM), lambda b, h: (0, 0)),
            blk(OFF_HQ), blk(OFF_HF), blk(OFF_HI), blk(OFF_HG),
        ] + [pl.BlockSpec(memory_space=pl.ANY) for _ in carried],
        out_specs=pl.BlockSpec((seq, width), lambda b, h: (seg * seg_batch + b, h)),
        out_shape=jax.ShapeDtypeStruct((tokens, HGRN_WIDTH), BF16),
        input_output_aliases={6: 0} if carried else {},
        scratch_shapes=_hgrn_scratch(seq),
        compiler_params=pltpu.CompilerParams(dimension_semantics=("parallel", "parallel")),
        name="hgrn2",
    )(lb_logits, hgrn_norm, hg_seg, hg_seg, hg_seg, hg_seg, *carried)


def _inproj_hgrn_kernel(heads_per_tile, n_carried, x_ref, gain_ref, w_ref, lb_ref, hn_ref, hq_ref,
                        hf_ref, hi_ref, hg_ref, *rest):
    o_ref, hg_out_ref, yh_ref, h_ref, *hgrn_scratch = rest[n_carried:]
    col_step = pl.program_id(1)

    @pl.when(col_step == 0)
    def _():
        _prenorm(x_ref, gain_ref, h_ref)

    def piece(p):
        cs = slice(p * IN_PIECE, (p + 1) * IN_PIECE)
        tile = jnp.dot(h_ref[...], w_ref[:, cs], preferred_element_type=F32).astype(BF16)
        o_ref[:, cs] = tile
        hg_out_ref[:, cs] = tile

    @pl.when(col_step < heads_per_tile)
    def _():
        tiles_per_seq = HGRN_HEADS // heads_per_tile
        head = (pl.program_id(0) % tiles_per_seq) * heads_per_tile + col_step
        _hgrn_head(lb_ref, hn_ref.at[pl.ds(head, 1), :], hq_ref, hf_ref, hi_ref, hg_ref, yh_ref,
                   *hgrn_scratch, piece_fn=piece, n_pieces=IN_PIECES)

    @pl.when(col_step >= heads_per_tile)
    def _():
        o_ref[...] = jnp.dot(h_ref[...], w_ref[...], preferred_element_type=F32).astype(BF16)
        _copy_hgrn_tile(o_ref, hg_out_ref)


def _inproj_hgrn(x2d, gain, w_bf16, seg, hg_prev, proj_prev, yh_prev, lb_logits, hgrn_norm, seq):
    tokens = x2d.shape[0]
    seg_tokens = hg_prev.shape[0]
    seg_row_tiles = seg_tokens // IN_TM
    tiles_per_seq = seq // IN_TM
    seg_batch = seg_row_tiles // tiles_per_seq
    heads_per_tile = HGRN_HEADS // tiles_per_seq
    assert heads_per_tile <= HG_TILE0 + HG_TILES
    assert pl.cdiv(seq // CHUNK, IN_PIECES) <= HGRN_SLOTS

    def head_of(i, j):
        return (i % tiles_per_seq) * heads_per_tile + jnp.minimum(j, heads_per_tile - 1)

    def blk(off):
        return pl.BlockSpec((seq, HGRN_DIM), lambda i, j: (
            i // tiles_per_seq, (off - HG_COL0) // HGRN_DIM + head_of(i, j)))

    carried = [proj_prev] + ([] if yh_prev is None else [yh_prev])
    n_read = 9
    aliases = {n_read: 0} if yh_prev is None else {n_read: 0, n_read + 1: 2}
    return pl.pallas_call(
        functools.partial(_inproj_hgrn_kernel, heads_per_tile, len(carried)),
        grid=(seg_row_tiles, IN_PROJ_WIDTH // IN_TN),
        in_specs=_inproj_specs(seg_row_tiles, seg) + [
            pl.BlockSpec((lb_logits.shape[0], HGRN_DIM), lambda i, j: (0, head_of(i, j))),
            pl.BlockSpec((HGRN_HEADS, HGRN_DIM), lambda i, j: (0, 0)),
            blk(OFF_HQ), blk(OFF_HF), blk(OFF_HI), blk(OFF_HG),
        ] + [pl.BlockSpec(memory_space=pl.ANY) for _ in carried],
        out_specs=_inproj_out_specs(seg_row_tiles, seg) + [
            pl.BlockSpec((seq, HGRN_DIM), lambda i, j: (
                (seg - 1) * seg_batch + i // tiles_per_seq, head_of(i, j)))],
        out_shape=_inproj_out_shapes(tokens, seg_tokens) + [
            jax.ShapeDtypeStruct((tokens, HGRN_WIDTH), BF16)],
        input_output_aliases=aliases,
        scratch_shapes=[pltpu.VMEM((IN_TM, D_MODEL), BF16)] + _hgrn_scratch(seq),
        compiler_params=pltpu.CompilerParams(
            dimension_semantics=("arbitrary", "arbitrary"),
            vmem_limit_bytes=58 * 1024 * 1024),
        name="inproj_hgrn2",
    )(x2d, gain, w_bf16, lb_logits, hgrn_norm, hg_prev, hg_prev, hg_prev, hg_prev, *carried)


SM_BLOCKS = 2
SM_TM = SM_BLOCKS * WINDOW
OUT_TN = 512
OUT_NCHUNK = D_MODEL // OUT_TN


def _t5_bucket(dist):
    max_exact = REL_BUCKETS // 2
    d = jnp.maximum(dist, 0)
    df = jnp.maximum(d, 1).astype(F32)
    large = max_exact + (jnp.log(df / max_exact) / math.log(REL_MAX_DIST / max_exact)
                         * (REL_BUCKETS - max_exact)).astype(jnp.int32)
    large = jnp.minimum(large, REL_BUCKETS - 1)
    return jnp.where(d < max_exact, d, large)


def _swap_halves(x):
    half = LANES // 2
    return jnp.concatenate([x[:, half:], x[:, :half]], axis=1)


def _swa_tables(relb_ref, bias_ref, cur_ref):
    qi = lax.broadcasted_iota(jnp.int32, (WINDOW, WINDOW), 0)
    sj = lax.broadcasted_iota(jnp.int32, (WINDOW, WINDOW), 1)
    own = sj <= qi
    bucket = _t5_bucket(jnp.where(own, qi - sj, qi + WINDOW - sj))
    for h in range(ATTN_HEADS):
        def pick(b, acc, h=h):
            return jnp.where(bucket == b, relb_ref[b, h], acc)
        tbl = lax.fori_loop(0, REL_BUCKETS, pick, jnp.zeros((WINDOW, WINDOW), F32))
        bias_ref[0, h] = tbl
        bias_ref[1, h] = jnp.where(own, tbl, MASK_VALUE)
    own_bf = jnp.where(own, 1.0, 0.0).astype(BF16)
    cur_ref[...] = jnp.concatenate([own_bf, own_bf], axis=0)


def _swa_merge_kernel(blocks_per_seq, n_tiles, relb_ref, sink_ref, q_ref, kc_ref, vc_ref, kp_ref,
                      vp_ref, ga_ref, gb_ref, yh_ref, *rest):
    gate_a_refs = rest[:OUT_NCHUNK]
    gate_h_refs = rest[OUT_NCHUNK:2 * OUT_NCHUNK]
    (x_ref, wa_ref, wh_ref, wo_ref, gain_ref, o_ref,
     bias_ref, cur_ref, ya_ref, m_ref) = rest[2 * OUT_NCHUNK:]
    step = pl.program_id(0)

    @pl.when(step == 0)
    def _():
        _swa_tables(relb_ref, bias_ref, cur_ref)
        ya_ref[...] = jnp.zeros_like(ya_ref)

    lane = lax.broadcasted_iota(jnp.int32, (1, LANES), 1)
    is_lo = lane < LANES // 2
    scale = ATTN_HEAD_DIM ** -0.5
    keep_lo = jnp.where(is_lo, 1.0, 0.0).astype(BF16)
    keep_hi = jnp.where(is_lo, 0.0, 1.0).astype(BF16)
    row = lax.broadcasted_iota(jnp.int32, (2 * WINDOW, 1), 0)
    top = row < WINDOW
    own2 = (lax.broadcasted_iota(jnp.int32, (2 * WINDOW, WINDOW), 1)
            <= (lax.broadcasted_iota(jnp.int32, (2 * WINDOW, WINDOW), 0) & (WINDOW - 1)))
    tile = jnp.minimum(step, n_tiles - 1)
    tile_starts_seq = jnp.where(tile % (blocks_per_seq // SM_BLOCKS) == 0, 1, 0)

    def scores(qm, km, first, heads):
        s2 = lax.dot_general(qm, km, _NT, preferred_element_type=F32)
        bias = jnp.concatenate([bias_ref[first, heads[0]], bias_ref[first, heads[1]]], axis=0)
        return jnp.where(own2, s2[:, WINDOW:], s2[:, :WINDOW]) + bias

    def probs(s, heads):
        sink = jnp.where(top, sink_ref[0, heads[0]], sink_ref[0, heads[1]])
        m = jnp.maximum(jnp.max(s, axis=-1, keepdims=True), sink)
        p = jnp.exp(s - m)
        denom = jnp.sum(p, axis=-1, keepdims=True) + jnp.exp(sink - m)
        pb = p.astype(BF16)
        p_own = pb * cur_ref[...]
        return jnp.concatenate([pb - p_own, p_own], axis=1), 1.0 / denom

    def heads_of(hk):
        e = hk % 2
        return (4 * hk + e, 4 * hk + 2 + e), (4 * hk + 1 - e, 4 * hk + 3 - e)

    s_same, s_diff, vals = [], [], []
    for blk in range(SM_BLOCKS):
        rows = slice(blk * WINDOW, (blk + 1) * WINDOW)
        first = tile_starts_seq if blk == 0 else 0
        for hk in range(ATTN_KV_HEADS):
            c, e = divmod(hk, 2)
            cs = slice(c * LANES, (c + 1) * LANES)
            if blk == 0:
                k_prev, v_prev = kp_ref[:, cs], vp_ref[:, cs]
            else:
                prev_rows = slice((blk - 1) * WINDOW, blk * WINDOW)
                k_prev, v_prev = kc_ref[prev_rows, cs], vc_ref[prev_rows, cs]
            k2 = jnp.concatenate([k_prev, kc_ref[rows, cs]], axis=0)
            v2 = jnp.concatenate([v_prev, vc_ref[rows, cs]], axis=0)
            k2s, v2s = _swap_halves(k2), _swap_halves(v2)
            keep_e, keep_o = (keep_lo, keep_hi) if e == 0 else (keep_hi, keep_lo)
            p0 = q_ref[rows, hk * 2 * LANES:hk * 2 * LANES + LANES]
            p1 = q_ref[rows, hk * 2 * LANES + LANES:(hk + 1) * 2 * LANES]
            q_same = jnp.concatenate([p0 * (keep_e * scale), p1 * (keep_e * scale)], axis=0)
            q_diff = jnp.concatenate([p0 * (keep_o * scale), p1 * (keep_o * scale)], axis=0)
            h_same, h_diff = heads_of(hk)
            s_same.append(scores(q_same, k2, first, h_same))
            s_diff.append(scores(q_diff, k2s, first, h_diff))
            vals.append((v2 * keep_lo, v2s * keep_hi) if e == 0 else (v2s * keep_lo, v2 * keep_hi))

    ya_prev = ya_ref[...]
    yh = yh_ref[...]
    for j in range(OUT_NCHUNK):
        cs = slice(j * OUT_TN, (j + 1) * OUT_TN)
        ua = jnp.dot(ya_prev, wa_ref[:, cs], preferred_element_type=F32)
        uh = jnp.dot(yh, wh_ref[:, cs], preferred_element_type=F32)
        merged = (_sigmoid(gate_a_refs[j][...].astype(F32)) * ua
                  + _sigmoid(gate_h_refs[j][...].astype(F32)) * uh)
        m_ref[:, cs] = merged.astype(BF16)

    pr_same = [probs(s_same[i], heads_of(i % ATTN_KV_HEADS)[0]) for i in range(len(s_same))]
    pr_diff = [probs(s_diff[i], heads_of(i % ATTN_KV_HEADS)[1]) for i in range(len(s_diff))]
    for blk in range(SM_BLOCKS):
        rows = slice(blk * WINDOW, (blk + 1) * WINDOW)
        for hk in range(ATTN_KV_HEADS):
            i = blk * ATTN_KV_HEADS + hk
            (p_lo, r_lo), (p_hi, r_hi) = ((pr_same[i], pr_diff[i]) if hk % 2 == 0
                                          else (pr_diff[i], pr_same[i]))
            v_lo, v_hi = vals[i]
            pv = jnp.dot(jnp.concatenate([p_lo, p_hi], axis=1),
                         jnp.concatenate([v_lo, v_hi], axis=0),
                         preferred_element_type=F32)
            pv = pv * jnp.where(is_lo, r_lo, r_hi)
            g_ref = ga_ref if hk < 2 else gb_ref
            for pair in range(2):
                gs = slice((hk % 2) * 2 * LANES + pair * LANES,
                           (hk % 2) * 2 * LANES + (pair + 1) * LANES)
                gate = _silu(g_ref[rows, gs].astype(F32))
                ya_ref[rows, (2 * hk + pair) * LANES:(2 * hk + pair + 1) * LANES] = (
                    pv[pair * WINDOW:(pair + 1) * WINDOW] * gate).astype(BF16)

    y = jnp.dot(m_ref[...], wo_ref[...], preferred_element_type=F32)
    ms = jnp.mean(y * y, axis=-1, keepdims=True)
    o_ref[...] = x_ref[...] + y * lax.rsqrt(ms + NORM_EPS) * gain_ref[...]


def _swa_merge(proj, yh, x2d, rel_bias, sinks, wa, wh, wo, gain, seq):
    tokens = x2d.shape[0]
    n_tiles = tokens // SM_TM
    blocks_per_seq = seq // WINDOW
    attn_tile = lambda t: jnp.minimum(t, n_tiles - 1)
    merge_tile = lambda t: jnp.maximum(t - 1, 0)
    prev_block = lambda t: jnp.maximum(attn_tile(t) * SM_BLOCKS - 1, 0)
    kcol, vcol = OFF_AK // KV_WIDTH, OFF_AV // KV_WIDTH
    gcol = OFF_AG // (ATTN_WIDTH // 2)
    const = lambda t: (0, 0)
    smem = pl.BlockSpec(memory_space=pltpu.SMEM)
    gate = lambda off, j: pl.BlockSpec((SM_TM, OUT_TN), lambda t: (merge_tile(t), off // OUT_TN + j))
    resident = functools.partial(pl.BlockSpec, index_map=const, pipeline_mode=pl.Buffered(1))
    return pl.pallas_call(
        functools.partial(_swa_merge_kernel, blocks_per_seq, n_tiles),
        grid=(n_tiles + 1,),
        in_specs=(
            [smem, smem,
             pl.BlockSpec((SM_TM, ATTN_WIDTH), lambda t: (attn_tile(t), 0)),
             pl.BlockSpec((SM_TM, KV_WIDTH), lambda t: (attn_tile(t), kcol)),
             pl.BlockSpec((SM_TM, KV_WIDTH), lambda t: (attn_tile(t), vcol)),
             pl.BlockSpec((WINDOW, KV_WIDTH), lambda t: (prev_block(t), kcol)),
             pl.BlockSpec((WINDOW, KV_WIDTH), lambda t: (prev_block(t), vcol)),
             pl.BlockSpec((SM_TM, ATTN_WIDTH // 2), lambda t: (attn_tile(t), gcol)),
             pl.BlockSpec((SM_TM, ATTN_WIDTH // 2), lambda t: (attn_tile(t), gcol + 1)),
             pl.BlockSpec((SM_TM, HGRN_WIDTH), lambda t: (merge_tile(t), 0))]
            + [gate(OFF_GA, j) for j in range(OUT_NCHUNK)]
            + [gate(OFF_GH, j) for j in range(OUT_NCHUNK)]
            + [pl.BlockSpec((SM_TM, D_MODEL), lambda t: (merge_tile(t), 0)),
               resident((ATTN_WIDTH, D_MODEL)), resident((HGRN_WIDTH, D_MODEL)),
               resident((D_MODEL, D_MODEL)), pl.BlockSpec((1, D_MODEL), const)]),
        out_specs=pl.BlockSpec((SM_TM, D_MODEL), lambda t: (merge_tile(t), 0)),
        out_shape=jax.ShapeDtypeStruct((tokens, D_MODEL), F32),
        scratch_shapes=[
            pltpu.VMEM((2, ATTN_HEADS, WINDOW, WINDOW), F32),
            pltpu.VMEM((2 * WINDOW, WINDOW), BF16),
            pltpu.VMEM((SM_TM, ATTN_WIDTH), BF16),
            pltpu.VMEM((SM_TM, D_MODEL), BF16),
        ],
        compiler_params=pltpu.CompilerParams(
            dimension_semantics=("arbitrary",), vmem_limit_bytes=52 * 1024 * 1024),
        name="swa_merge",
    )(rel_bias, sinks, proj, proj, proj, proj, proj, proj, proj, yh,
      *([proj] * (2 * OUT_NCHUNK)), x2d, wa, wh, wo, gain)


def kernel(x, norm_pre, w_in, rel_bias, attn_sinks, lb_logits, hgrn_norm, w_branch_attn,
           w_branch_hgrn, w_out, norm_post):
    batch, seq, d_model = x.shape
    depth = w_in.shape[0]
    assert depth == 1 and d_model == D_MODEL and seq % IN_TM == 0
    assert lb_logits.shape == (depth + 1, HGRN_WIDTH)
    assert batch % N_SEG == 0
    x2d = x.reshape(batch * seq, d_model)
    layer = 0
    seg_batch = batch // N_SEG
    seg_tokens = seg_batch * seq
    gain_pre = norm_pre[layer][None, :]
    w_bf16 = w_in[layer].astype(BF16)
    proj, hg = _inproj(x2d, gain_pre, w_bf16, 0, seg_tokens)
    yh = None
    for seg in range(1, N_SEG):
        proj, hg, yh = _inproj_hgrn(x2d, gain_pre, w_bf16, seg, hg, proj, yh, lb_logits,
                                    hgrn_norm[layer], seq)
    yh = _hgrn(hg, lb_logits, hgrn_norm[layer], yh, N_SEG - 1, batch * seq, seq)
    out = _swa_merge(proj, yh, x2d, rel_bias, attn_sinks[layer][None, :],
                     w_branch_attn[layer].astype(BF16), w_branch_hgrn[layer].astype(BF16),
                     w_out[layer].astype(BF16), norm_post[layer][None, :], seq)
    return out.reshape(batch, seq, d_model)
```

```python
import functools
import math

import jax
import jax.numpy as jnp
from jax import lax
from jax.experimental import pallas as pl
from jax.experimental.pallas import tpu as pltpu

D_MODEL = 2048
ATTN_HEADS = 16
ATTN_KV_HEADS = 4
ATTN_HEAD_DIM = 64
ATTN_GROUP = ATTN_HEADS // ATTN_KV_HEADS
WINDOW = 128
ATTN_WIDTH = ATTN_HEADS * ATTN_HEAD_DIM
KV_WIDTH = ATTN_KV_HEADS * ATTN_HEAD_DIM
HGRN_HEADS = 8
HGRN_DIM = 128
HGRN_WIDTH = HGRN_HEADS * HGRN_DIM
CHUNK = 64
CHUNK_LEVELS = 6
HGRN_UNROLL = 4
HGRN_SLOTS = 6
HGRN_HEADS_PER_STEP = 2
SEG_BATCHES = (1, 1, 2, 2, 2, 2, 2, 2, 2)
REL_BUCKETS = 32
REL_MAX_DIST = 128
NORM_EPS = 1e-6
IN_PROJ_WIDTH = 10752

OFF_AQ = 0
OFF_AK = OFF_AQ + ATTN_WIDTH
OFF_AV = OFF_AK + KV_WIDTH
OFF_AG = OFF_AV + KV_WIDTH
OFF_HQ = OFF_AG + ATTN_WIDTH
OFF_HF = OFF_HQ + HGRN_WIDTH
OFF_HI = OFF_HF + HGRN_WIDTH
OFF_HG = OFF_HI + HGRN_WIDTH
OFF_GA = OFF_HG + HGRN_WIDTH
OFF_GH = OFF_GA + D_MODEL

LANES = 128
SUBLANES = 8
MASK_VALUE = -1e30

F32 = jnp.float32
BF16 = jnp.bfloat16

_NT = (((1,), (1,)), ((), ()))
_TN = (((0,), (0,)), ((), ()))


def _sigmoid(x):
    return 1.0 / (1.0 + jnp.exp(-x))


def _silu(x):
    return x * _sigmoid(x)


IN_TM = 1024
IN_TN = 1536
IN_NORM_ROWS = 256
IN_PIECE = 256
IN_PIECES = IN_TN // IN_PIECE


def _prenorm(x_ref, gain_ref, h_ref):
    gain = gain_ref[...]
    for r in range(0, IN_TM, IN_NORM_ROWS):
        xf = x_ref[r:r + IN_NORM_ROWS, :]
        ms = jnp.mean(xf * xf, axis=-1, keepdims=True)
        h_ref[r:r + IN_NORM_ROWS, :] = (xf * lax.rsqrt(ms + NORM_EPS) * gain).astype(BF16)


HG_TILE0 = OFF_HQ // IN_TN
HG_TILES = (OFF_GA - 1) // IN_TN - HG_TILE0 + 1
HG_COL0 = HG_TILE0 * IN_TN


def _copy_hgrn_tile(o_ref, hg_ref):
    col_step = pl.program_id(1)

    @pl.when((col_step >= HG_TILE0) & (col_step < HG_TILE0 + HG_TILES))
    def _():
        hg_ref[...] = o_ref[...]


def _inproj_kernel(x_ref, gain_ref, w_ref, o_ref, hg_ref, h_ref):
    @pl.when(pl.program_id(1) == 0)
    def _():
        _prenorm(x_ref, gain_ref, h_ref)

    o_ref[...] = jnp.dot(h_ref[...], w_ref[...], preferred_element_type=F32).astype(BF16)
    _copy_hgrn_tile(o_ref, hg_ref)


def _inproj_specs(row_tile0):
    return [
        pl.BlockSpec((IN_TM, D_MODEL), lambda i, j: (row_tile0 + i, 0)),
        pl.BlockSpec((1, D_MODEL), lambda i, j: (0, 0)),
        pl.BlockSpec((D_MODEL, IN_TN), lambda i, j: (0, j)),
    ]


def _inproj_out_specs(row_tile0):
    hg_tile = lambda j: jnp.clip(j - HG_TILE0, 0, HG_TILES - 1)
    return [pl.BlockSpec((IN_TM, IN_TN), lambda i, j: (row_tile0 + i, j)),
            pl.BlockSpec((IN_TM, IN_TN), lambda i, j: (i, hg_tile(j)))]


def _inproj_out_shapes(tokens, n_row_tiles):
    return [jax.ShapeDtypeStruct((tokens, IN_PROJ_WIDTH), BF16),
            jax.ShapeDtypeStruct((n_row_tiles * IN_TM, HG_TILES * IN_TN), BF16)]


def _inproj(x2d, gain, w_bf16, row_tile0, n_row_tiles):
    return pl.pallas_call(
        _inproj_kernel,
        grid=(n_row_tiles, IN_PROJ_WIDTH // IN_TN),
        in_specs=_inproj_specs(row_tile0),
        out_specs=_inproj_out_specs(row_tile0),
        out_shape=_inproj_out_shapes(x2d.shape[0], n_row_tiles),
        scratch_shapes=[pltpu.VMEM((IN_TM, D_MODEL), BF16)],
        compiler_params=pltpu.CompilerParams(
            dimension_semantics=("parallel", "arbitrary"),
            vmem_limit_bytes=56 * 1024 * 1024),
        name="inproj",
    )(x2d, gain, w_bf16)


def _level_reference(g_ref, level, sub):
    half = 1 << level
    blk = 2 * half

    def row_tile(r):
        return g_ref[pl.ds(r, SUBLANES, stride=0), :]

    if blk >= SUBLANES:
        return jnp.concatenate(
            [row_tile(b * blk + half) for b in range(CHUNK // blk) for _ in range(blk // SUBLANES)],
            axis=0)
    tiles = []
    for t in range(CHUNK // SUBLANES):
        acc = row_tile(t * SUBLANES + half)
        for b in range(1, SUBLANES // blk):
            acc = jnp.where(sub >= b * blk, row_tile(t * SUBLANES + b * blk + half), acc)
        tiles.append(acc)
    return jnp.concatenate(tiles, axis=0)


def _neg_abs(x):
    bits = lax.bitcast_convert_type(x, jnp.uint32) | jnp.uint32(0x80000000)
    return lax.bitcast_convert_type(bits, F32)


def _hgrn_kernel(n_carried, lb_ref, hn_ref, hq_ref, hf_ref, hi_ref, hg_ref, *rest):
    o_ref, *scratch = rest[n_carried:]
    for j in range(HGRN_HEADS_PER_STEP):
        cols = slice(j * HGRN_DIM, (j + 1) * HGRN_DIM)
        head = pl.program_id(1) * HGRN_HEADS_PER_STEP + j
        _hgrn_head(lb_ref.at[:, cols], hn_ref.at[pl.ds(head, 1), :], hq_ref.at[:, cols],
                   hf_ref.at[:, cols], hi_ref.at[:, cols], hg_ref.at[:, cols], o_ref.at[:, cols],
                   *scratch)


def _hgrn_head(lb_ref, hn_ref, hq_ref, hf_ref, hi_ref, hg_ref, o_ref,
               g_ref, qg_ref, attn_ref, upd_ref, keep_ref, state_ref, piece_fn=None, n_pieces=0):
    seq = hq_ref.shape[0]
    lg = lb_ref[...]
    ex = jnp.exp(lg - jnp.max(lg, axis=0, keepdims=True))
    lb = ex[0:1, :] / jnp.sum(ex, axis=0, keepdims=True)
    hn = hn_ref[...]

    rowi = lax.broadcasted_iota(jnp.int32, (CHUNK, HGRN_DIM), 0)
    sub = lax.broadcasted_iota(jnp.int32, (SUBLANES, HGRN_DIM), 0)
    ti = lax.broadcasted_iota(jnp.int32, (CHUNK, CHUNK), 0)
    si = lax.broadcasted_iota(jnp.int32, (CHUNK, CHUNK), 1)
    xor = ti ^ si
    lvl = jnp.full((CHUNK, CHUNK), -2, jnp.int32)
    for level in range(CHUNK_LEVELS):
        lvl = jnp.where(xor >= (1 << level), level, lvl)
    lvl = jnp.where(ti > si, lvl, jnp.where(ti == si, -1, -2))

    def chunk_rows(c):
        start = c * CHUNK
        return pl.ds(start if isinstance(c, int) else pl.multiple_of(start, CHUNK), CHUNK)

    def local_step(c, slot):
        rows = chunk_rows(c)
        q = _silu(hq_ref[rows, :].astype(F32))
        f = lb + (1.0 - lb) * _sigmoid(hf_ref[rows, :].astype(F32))
        k = 1.0 - f
        v = hi_ref[rows, :]
        g = jnp.log2(f)
        for shift in (1, 2, 4):
            g = g + jnp.where(rowi >= shift, pltpu.roll(g, shift, 0), 0.0)
        for shift in (8, 16, 32):
            g = g + jnp.concatenate(
                [jnp.zeros((shift, HGRN_DIM), F32), g[:CHUNK - shift]], axis=0)
        g_slot = g_ref.at[slot]
        g_slot[...] = g
        g_last = g_slot[pl.ds(CHUNK - 1, SUBLANES, stride=0), :][0:1]
        qb = q.astype(BF16)
        kb = k.astype(BF16)
        qg_ref[rows, :] = qb * jnp.exp2(g).astype(BF16)

        attn = jnp.where(lvl == -1, jnp.sum(q * k, axis=-1, keepdims=True), 0.0)
        for level in range(CHUNK_LEVELS):
            dec = jnp.exp2(_neg_abs(g - _level_reference(g_slot, level, sub))).astype(BF16)
            a = lax.dot_general(qb * dec, kb * dec, _NT, preferred_element_type=F32)
            attn = jnp.where(lvl == level, a, attn)
        attn_ref[c] = attn.astype(BF16)

        k_dec = kb * jnp.exp2(g_last - g).astype(BF16)
        upd_ref[c] = lax.dot_general(v, k_dec, _TN, preferred_element_type=F32)
        keep_ref[c] = jnp.broadcast_to(jnp.exp2(g_last), (SUBLANES, HGRN_DIM))

    def local_group(i, carry):
        for slot in range(HGRN_UNROLL):
            local_step(i * HGRN_UNROLL + slot, slot)
        return carry

    n_chunks = seq // CHUNK
    if n_pieces:
        bounds = [(p * n_chunks) // n_pieces for p in range(n_pieces + 1)]
        for p in range(n_pieces):
            piece_fn(p)
            for c in range(bounds[p], bounds[p + 1]):
                local_step(c, c - bounds[p])
    else:
        lax.fori_loop(0, n_chunks // HGRN_UNROLL, local_group, 0)

    def state_step(c, state_t):
        state_ref[c] = state_t.astype(BF16)
        return keep_ref[c][0:1] * state_t + upd_ref[c]

    lax.fori_loop(0, n_chunks, state_step, jnp.zeros((HGRN_DIM, HGRN_DIM), F32),
                  unroll=HGRN_UNROLL)

    def out_step(c, carry):
        rows = chunk_rows(c)
        o = lax.dot_general(qg_ref[rows, :], state_ref[c], _NT, preferred_element_type=F32)
        o = o + jnp.dot(attn_ref[c], hi_ref[rows, :], preferred_element_type=F32)
        ms = jnp.mean(o * o, axis=-1, keepdims=True)
        y = o * lax.rsqrt(ms + NORM_EPS) * hn * _silu(hg_ref[rows, :].astype(F32))
        o_ref[rows, :] = y.astype(BF16)
        return carry

    lax.fori_loop(0, n_chunks, out_step, 0, unroll=2 * HGRN_UNROLL)


def _hgrn_scratch(seq):
    n_chunks = seq // CHUNK
    return [
        pltpu.VMEM((HGRN_SLOTS, CHUNK, HGRN_DIM), F32),
        pltpu.VMEM((seq, HGRN_DIM), BF16),
        pltpu.VMEM((n_chunks, CHUNK, CHUNK), BF16),
        pltpu.VMEM((n_chunks, HGRN_DIM, HGRN_DIM), F32),
        pltpu.VMEM((n_chunks, SUBLANES, HGRN_DIM), F32),
        pltpu.VMEM((n_chunks, HGRN_DIM, HGRN_DIM), BF16),
    ]


def _hgrn(hg_seg, lb_logits, hgrn_norm, yh_prev, batch0, tokens, seq):
    width = HGRN_HEADS_PER_STEP * HGRN_DIM
    seg_batch = hg_seg.shape[0] // seq
    blk = lambda off: pl.BlockSpec((seq, width), lambda b, h: (b, (off - HG_COL0) // width + h))
    carried = [] if yh_prev is None else [yh_prev]
    return pl.pallas_call(
        functools.partial(_hgrn_kernel, len(carried)),
        grid=(seg_batch, HGRN_HEADS // HGRN_HEADS_PER_STEP),
        in_specs=[
            pl.BlockSpec((lb_logits.shape[0], width), lambda b, h: (0, h)),
            pl.BlockSpec((HGRN_HEADS, HGRN_DIM), lambda b, h: (0, 0)),
            blk(OFF_HQ), blk(OFF_HF), blk(OFF_HI), blk(OFF_HG),
        ] + [pl.BlockSpec(memory_space=pl.ANY) for _ in carried],
        out_specs=pl.BlockSpec((seq, width), lambda b, h: (batch0 + b, h)),
        out_shape=jax.ShapeDtypeStruct((tokens, HGRN_WIDTH), BF16),
        input_output_aliases={6: 0} if carried else {},
        scratch_shapes=_hgrn_scratch(seq),
        compiler_params=pltpu.CompilerParams(dimension_semantics=("parallel", "parallel")),
        name="hgrn2",
    )(lb_logits, hgrn_norm, hg_seg, hg_seg, hg_seg, hg_seg, *carried)


def _inproj_hgrn_kernel(heads_per_tile, n_carried, x_ref, gain_ref, w_ref, lb_ref, hn_ref, hq_ref,
                        hf_ref, hi_ref, hg_ref, *rest):
    o_ref, hg_out_ref, yh_ref, h_ref, *hgrn_scratch = rest[n_carried:]
    col_step = pl.program_id(1)

    @pl.when(col_step == 0)
    def _():
        _prenorm(x_ref, gain_ref, h_ref)

    def piece(p):
        cs = slice(p * IN_PIECE, (p + 1) * IN_PIECE)
        tile = jnp.dot(h_ref[...], w_ref[:, cs], preferred_element_type=F32).astype(BF16)
        o_ref[:, cs] = tile
        hg_out_ref[:, cs] = tile

    @pl.when(col_step < heads_per_tile)
    def _():
        head = (pl.program_id(0) * heads_per_tile + col_step) % HGRN_HEADS
        _hgrn_head(lb_ref, hn_ref.at[pl.ds(head, 1), :], hq_ref, hf_ref, hi_ref, hg_ref, yh_ref,
                   *hgrn_scratch, piece_fn=piece, n_pieces=IN_PIECES)

    @pl.when(col_step >= heads_per_tile)
    def _():
        o_ref[...] = jnp.dot(h_ref[...], w_ref[...], preferred_element_type=F32).astype(BF16)
        _copy_hgrn_tile(o_ref, hg_out_ref)


def _inproj_hgrn(x2d, gain, w_bf16, row_tile0, n_row_tiles, hg_prev, batch0_prev, proj_prev, yh_prev,
                 lb_logits, hgrn_norm, seq):
    tokens = x2d.shape[0]
    n_heads = (hg_prev.shape[0] // seq) * HGRN_HEADS
    heads_per_tile = n_heads // n_row_tiles
    assert heads_per_tile * n_row_tiles == n_heads
    assert heads_per_tile <= HG_TILE0 + HG_TILES
    assert pl.cdiv(seq // CHUNK, IN_PIECES) <= HGRN_SLOTS

    def flat_head(i, j):
        return i * heads_per_tile + jnp.minimum(j, heads_per_tile - 1)

    def blk(off):
        return pl.BlockSpec((seq, HGRN_DIM), lambda i, j: (
            flat_head(i, j) // HGRN_HEADS, (off - HG_COL0) // HGRN_DIM + flat_head(i, j) % HGRN_HEADS))

    carried = [proj_prev] + ([] if yh_prev is None else [yh_prev])
    n_read = 9
    aliases = {n_read: 0} if yh_prev is None else {n_read: 0, n_read + 1: 2}
    return pl.pallas_call(
        functools.partial(_inproj_hgrn_kernel, heads_per_tile, len(carried)),
        grid=(n_row_tiles, IN_PROJ_WIDTH // IN_TN),
        in_specs=_inproj_specs(row_tile0) + [
            pl.BlockSpec((lb_logits.shape[0], HGRN_DIM),
                         lambda i, j: (0, flat_head(i, j) % HGRN_HEADS)),
            pl.BlockSpec((HGRN_HEADS, HGRN_DIM), lambda i, j: (0, 0)),
            blk(OFF_HQ), blk(OFF_HF), blk(OFF_HI), blk(OFF_HG),
        ] + [pl.BlockSpec(memory_space=pl.ANY) for _ in carried],
        out_specs=_inproj_out_specs(row_tile0) + [
            pl.BlockSpec((seq, HGRN_DIM), lambda i, j: (
                batch0_prev + flat_head(i, j) // HGRN_HEADS, flat_head(i, j) % HGRN_HEADS))],
        out_shape=_inproj_out_shapes(tokens, n_row_tiles) + [
            jax.ShapeDtypeStruct((tokens, HGRN_WIDTH), BF16)],
        input_output_aliases=aliases,
        scratch_shapes=[pltpu.VMEM((IN_TM, D_MODEL), BF16)] + _hgrn_scratch(seq),
        compiler_params=pltpu.CompilerParams(
            dimension_semantics=("arbitrary", "arbitrary"),
            vmem_limit_bytes=58 * 1024 * 1024),
        name="inproj_hgrn2",
    )(x2d, gain, w_bf16, lb_logits, hgrn_norm, hg_prev, hg_prev, hg_prev, hg_prev, *carried)


SM_BLOCKS = 2
SM_TM = SM_BLOCKS * WINDOW
OUT_TN = 512
OUT_NCHUNK = D_MODEL // OUT_TN


def _t5_bucket(dist):
    max_exact = REL_BUCKETS // 2
    d = jnp.maximum(dist, 0)
    df = jnp.maximum(d, 1).astype(F32)
    large = max_exact + (jnp.log(df / max_exact) / math.log(REL_MAX_DIST / max_exact)
                         * (REL_BUCKETS - max_exact)).astype(jnp.int32)
    large = jnp.minimum(large, REL_BUCKETS - 1)
    return jnp.where(d < max_exact, d, large)


def _swap_halves(x):
    half = LANES // 2
    return jnp.concatenate([x[:, half:], x[:, :half]], axis=1)


def _swa_tables(relb_ref, bias_ref, cur_ref):
    qi = lax.broadcasted_iota(jnp.int32, (WINDOW, WINDOW), 0)
    sj = lax.broadcasted_iota(jnp.int32, (WINDOW, WINDOW), 1)
    own = sj <= qi
    bucket = _t5_bucket(jnp.where(own, qi - sj, qi + WINDOW - sj))
    for h in range(ATTN_HEADS):
        def pick(b, acc, h=h):
            return jnp.where(bucket == b, relb_ref[b, h], acc)
        tbl = lax.fori_loop(0, REL_BUCKETS, pick, jnp.zeros((WINDOW, WINDOW), F32))
        bias_ref[0, h] = tbl
        bias_ref[1, h] = jnp.where(own, tbl, MASK_VALUE)
    own_bf = jnp.where(own, 1.0, 0.0).astype(BF16)
    cur_ref[...] = jnp.concatenate([own_bf, own_bf], axis=0)


def _swa_merge_kernel(blocks_per_seq, n_tiles, relb_ref, sink_ref, q_ref, kc_ref, vc_ref, kp_ref,
                      vp_ref, ga_ref, gb_ref, yh_ref, *rest):
    gate_a_refs = rest[:OUT_NCHUNK]
    gate_h_refs = rest[OUT_NCHUNK:2 * OUT_NCHUNK]
    (x_ref, wa_ref, wh_ref, wo_ref, gain_ref, o_ref,
     bias_ref, cur_ref, ya_ref, m_ref) = rest[2 * OUT_NCHUNK:]
    step = pl.program_id(0)

    @pl.when(step == 0)
    def _():
        _swa_tables(relb_ref, bias_ref, cur_ref)
        ya_ref[...] = jnp.zeros_like(ya_ref)

    lane = lax.broadcasted_iota(jnp.int32, (1, LANES), 1)
    is_lo = lane < LANES // 2
    scale = ATTN_HEAD_DIM ** -0.5
    keep_lo = jnp.where(is_lo, 1.0, 0.0).astype(BF16)
    keep_hi = jnp.where(is_lo, 0.0, 1.0).astype(BF16)
    row = lax.broadcasted_iota(jnp.int32, (2 * WINDOW, 1), 0)
    top = row < WINDOW
    own2 = (lax.broadcasted_iota(jnp.int32, (2 * WINDOW, WINDOW), 1)
            <= (lax.broadcasted_iota(jnp.int32, (2 * WINDOW, WINDOW), 0) & (WINDOW - 1)))
    tile = jnp.minimum(step, n_tiles - 1)
    tile_starts_seq = jnp.where(tile % (blocks_per_seq // SM_BLOCKS) == 0, 1, 0)

    def scores(qm, km, first, heads):
        s2 = lax.dot_general(qm, km, _NT, preferred_element_type=F32)
        bias = jnp.concatenate([bias_ref[first, heads[0]], bias_ref[first, heads[1]]], axis=0)
        return jnp.where(own2, s2[:, WINDOW:], s2[:, :WINDOW]) + bias

    def probs(s, heads):
        sink = jnp.where(top, sink_ref[0, heads[0]], sink_ref[0, heads[1]])
        m = jnp.maximum(jnp.max(s, axis=-1, keepdims=True), sink)
        p = jnp.exp(s - m)
        denom = jnp.sum(p, axis=-1, keepdims=True) + jnp.exp(sink - m)
        pb = p.astype(BF16)
        p_own = pb * cur_ref[...]
        return jnp.concatenate([pb - p_own, p_own], axis=1), 1.0 / denom

    def heads_of(hk):
        e = hk % 2
        return (4 * hk + e, 4 * hk + 2 + e), (4 * hk + 1 - e, 4 * hk + 3 - e)

    s_same, s_diff, vals = [], [], []
    for blk in range(SM_BLOCKS):
        rows = slice(blk * WINDOW, (blk + 1) * WINDOW)
        first = tile_starts_seq if blk == 0 else 0
        for hk in range(ATTN_KV_HEADS):
            c, e = divmod(hk, 2)
            cs = slice(c * LANES, (c + 1) * LANES)
            if blk == 0:
                k_prev, v_prev = kp_ref[:, cs], vp_ref[:, cs]
            else:
                prev_rows = slice((blk - 1) * WINDOW, blk * WINDOW)
                k_prev, v_prev = kc_ref[prev_rows, cs], vc_ref[prev_rows, cs]
            k2 = jnp.concatenate([k_prev, kc_ref[rows, cs]], axis=0)
            v2 = jnp.concatenate([v_prev, vc_ref[rows, cs]], axis=0)
            k2s, v2s = _swap_halves(k2), _swap_halves(v2)
            keep_e, keep_o = (keep_lo, keep_hi) if e == 0 else (keep_hi, keep_lo)
            p0 = q_ref[rows, hk * 2 * LANES:hk * 2 * LANES + LANES]
            p1 = q_ref[rows, hk * 2 * LANES + LANES:(hk + 1) * 2 * LANES]
            q_same = jnp.concatenate([p0 * (keep_e * scale), p1 * (keep_e * scale)], axis=0)
            q_diff = jnp.concatenate([p0 * (keep_o * scale), p1 * (keep_o * scale)], axis=0)
            h_same, h_diff = heads_of(hk)
            s_same.append(scores(q_same, k2, first, h_same))
            s_diff.append(scores(q_diff, k2s, first, h_diff))
            vals.append((v2 * keep_lo, v2s * keep_hi) if e == 0 else (v2s * keep_lo, v2 * keep_hi))

    ya_prev = ya_ref[...]
    yh = yh_ref[...]
    for j in range(OUT_NCHUNK):
        cs = slice(j * OUT_TN, (j + 1) * OUT_TN)
        ua = jnp.dot(ya_prev, wa_ref[:, cs], preferred_element_type=F32)
        uh = jnp.dot(yh, wh_ref[:, cs], preferred_element_type=F32)
        merged = (_sigmoid(gate_a_refs[j][...].astype(F32)) * ua
                  + _sigmoid(gate_h_refs[j][...].astype(F32)) * uh)
        m_ref[:, cs] = merged.astype(BF16)

    pr_same = [probs(s_same[i], heads_of(i % ATTN_KV_HEADS)[0]) for i in range(len(s_same))]
    pr_diff = [probs(s_diff[i], heads_of(i % ATTN_KV_HEADS)[1]) for i in range(len(s_diff))]
    for blk in range(SM_BLOCKS):
        rows = slice(blk * WINDOW, (blk + 1) * WINDOW)
        for hk in range(ATTN_KV_HEADS):
            i = blk * ATTN_KV_HEADS + hk
            (p_lo, r_lo), (p_hi, r_hi) = ((pr_same[i], pr_diff[i]) if hk % 2 == 0
                                          else (pr_diff[i], pr_same[i]))
            v_lo, v_hi = vals[i]
            pv = jnp.dot(jnp.concatenate([p_lo, p_hi], axis=1),
                         jnp.concatenate([v_lo, v_hi], axis=0),
                         preferred_element_type=F32)
            pv = pv * jnp.where(is_lo, r_lo, r_hi)
            g_ref = ga_ref if hk < 2 else gb_ref
            for pair in range(2):
                gs = slice((hk % 2) * 2 * LANES + pair * LANES,
                           (hk % 2) * 2 * LANES + (pair + 1) * LANES)
                gate = _silu(g_ref[rows, gs].astype(F32))
                ya_ref[rows, (2 * hk + pair) * LANES:(2 * hk + pair + 1) * LANES] = (
                    pv[pair * WINDOW:(pair + 1) * WINDOW] * gate).astype(BF16)

    y = jnp.dot(m_ref[...], wo_ref[...], preferred_element_type=F32)
    ms = jnp.mean(y * y, axis=-1, keepdims=True)
    o_ref[...] = x_ref[...] + y * lax.rsqrt(ms + NORM_EPS) * gain_ref[...]


def _swa_merge(proj, yh, x2d, rel_bias, sinks, wa, wh, wo, gain, seq):
    tokens = x2d.shape[0]
    n_tiles = tokens // SM_TM
    blocks_per_seq = seq // WINDOW
    attn_tile = lambda t: jnp.minimum(t, n_tiles - 1)
    merge_tile = lambda t: jnp.maximum(t - 1, 0)
    prev_block = lambda t: jnp.maximum(attn_tile(t) * SM_BLOCKS - 1, 0)
    kcol, vcol = OFF_AK // KV_WIDTH, OFF_AV // KV_WIDTH
    gcol = OFF_AG // (ATTN_WIDTH // 2)
    const = lambda t: (0, 0)
    smem = pl.BlockSpec(memory_space=pltpu.SMEM)
    gate = lambda off, j: pl.BlockSpec((SM_TM, OUT_TN), lambda t: (merge_tile(t), off // OUT_TN + j))
    resident = functools.partial(pl.BlockSpec, index_map=const, pipeline_mode=pl.Buffered(1))
    return pl.pallas_call(
        functools.partial(_swa_merge_kernel, blocks_per_seq, n_tiles),
        grid=(n_tiles + 1,),
        in_specs=(
            [smem, smem,
             pl.BlockSpec((SM_TM, ATTN_WIDTH), lambda t: (attn_tile(t), 0)),
             pl.BlockSpec((SM_TM, KV_WIDTH), lambda t: (attn_tile(t), kcol)),
             pl.BlockSpec((SM_TM, KV_WIDTH), lambda t: (attn_tile(t), vcol)),
             pl.BlockSpec((WINDOW, KV_WIDTH), lambda t: (prev_block(t), kcol)),
             pl.BlockSpec((WINDOW, KV_WIDTH), lambda t: (prev_block(t), vcol)),
             pl.BlockSpec((SM_TM, ATTN_WIDTH // 2), lambda t: (attn_tile(t), gcol)),
             pl.BlockSpec((SM_TM, ATTN_WIDTH // 2), lambda t: (attn_tile(t), gcol + 1)),
             pl.BlockSpec((SM_TM, HGRN_WIDTH), lambda t: (merge_tile(t), 0))]
            + [gate(OFF_GA, j) for j in range(OUT_NCHUNK)]
            + [gate(OFF_GH, j) for j in range(OUT_NCHUNK)]
            + [pl.BlockSpec((SM_TM, D_MODEL), lambda t: (merge_tile(t), 0)),
               resident((ATTN_WIDTH, D_MODEL)), resident((HGRN_WIDTH, D_MODEL)),
               resident((D_MODEL, D_MODEL)), pl.BlockSpec((1, D_MODEL), const)]),
        out_specs=pl.BlockSpec((SM_TM, D_MODEL), lambda t: (merge_tile(t), 0)),
        out_shape=jax.ShapeDtypeStruct((tokens, D_MODEL), F32),
        scratch_shapes=[
            pltpu.VMEM((2, ATTN_HEADS, WINDOW, WINDOW), F32),
            pltpu.VMEM((2 * WINDOW, WINDOW), BF16),
            pltpu.VMEM((SM_TM, ATTN_WIDTH), BF16),
            pltpu.VMEM((SM_TM, D_MODEL), BF16),
        ],
        compiler_params=pltpu.CompilerParams(
            dimension_semantics=("arbitrary",), vmem_limit_bytes=52 * 1024 * 1024),
        name="swa_merge",
    )(rel_bias, sinks, proj, proj, proj, proj, proj, proj, proj, yh,
      *([proj] * (2 * OUT_NCHUNK)), x2d, wa, wh, wo, gain)


def kernel(x, norm_pre, w_in, rel_bias, attn_sinks, lb_logits, hgrn_norm, w_branch_attn,
           w_branch_hgrn, w_out, norm_post):
    batch, seq, d_model = x.shape
    depth = w_in.shape[0]
    assert depth == 1 and d_model == D_MODEL and seq % IN_TM == 0
    assert lb_logits.shape == (depth + 1, HGRN_WIDTH)
    assert sum(SEG_BATCHES) == batch
    x2d = x.reshape(batch * seq, d_model)
    layer = 0
    gain_pre = norm_pre[layer][None, :]
    w_bf16 = w_in[layer].astype(BF16)
    tiles_per_seq = seq // IN_TM
    proj, hg = _inproj(x2d, gain_pre, w_bf16, 0, SEG_BATCHES[0] * tiles_per_seq)
    yh = None
    batch0 = 0
    for prev_batches, seg_batches in zip(SEG_BATCHES[:-1], SEG_BATCHES[1:]):
        proj, hg, yh = _inproj_hgrn(
            x2d, gain_pre, w_bf16, (batch0 + prev_batches) * tiles_per_seq,
            seg_batches * tiles_per_seq, hg, batch0, proj, yh, lb_logits, hgrn_norm[layer], seq)
        batch0 += prev_batches
    yh = _hgrn(hg, lb_logits, hgrn_norm[layer], yh, batch0, batch * seq, seq)
    out = _swa_merge(proj, yh, x2d, rel_bias, attn_sinks[layer][None, :],
                     w_branch_attn[layer].astype(BF16), w_branch_hgrn[layer].astype(BF16),
                     w_out[layer].astype(BF16), norm_post[layer][None, :], seq)
    return out.reshape(batch, seq, d_model)
```

```python
import functools
import math

import jax
import jax.numpy as jnp
from jax import lax
from jax.experimental import pallas as pl
from jax.experimental.pallas import tpu as pltpu

D_MODEL = 2048
ATTN_HEADS = 16
ATTN_KV_HEADS = 4
ATTN_HEAD_DIM = 64
ATTN_GROUP = ATTN_HEADS // ATTN_KV_HEADS
WINDOW = 128
ATTN_WIDTH = ATTN_HEADS * ATTN_HEAD_DIM
KV_WIDTH = ATTN_KV_HEADS * ATTN_HEAD_DIM
HGRN_HEADS = 8
HGRN_DIM = 128
HGRN_WIDTH = HGRN_HEADS * HGRN_DIM
CHUNK = 64
CHUNK_LEVELS = 6
HGRN_UNROLL = 4
HGRN_SLOTS = 6
HGRN_HEADS_PER_STEP = 2
SEG_BATCHES = (1, 1, 2, 2, 2, 2, 2, 2, 2)
REL_BUCKETS = 32
REL_MAX_DIST = 128
NORM_EPS = 1e-6
IN_PROJ_WIDTH = 10752

OFF_AQ = 0
OFF_AK = OFF_AQ + ATTN_WIDTH
OFF_AV = OFF_AK + KV_WIDTH
OFF_AG = OFF_AV + KV_WIDTH
OFF_HQ = OFF_AG + ATTN_WIDTH
OFF_HF = OFF_HQ + HGRN_WIDTH
OFF_HI = OFF_HF + HGRN_WIDTH
OFF_HG = OFF_HI + HGRN_WIDTH
OFF_GA = OFF_HG + HGRN_WIDTH
OFF_GH = OFF_GA + D_MODEL

LANES = 128
SUBLANES = 8
MASK_VALUE = -1e30

F32 = jnp.float32
BF16 = jnp.bfloat16

_NT = (((1,), (1,)), ((), ()))
_TN = (((0,), (0,)), ((), ()))


def _sigmoid(x):
    return 1.0 / (1.0 + jnp.exp(-x))


def _silu(x):
    return x * _sigmoid(x)


IN_TM = 1024
IN_TN = 1536
IN_NORM_ROWS = 256
IN_PIECE = 256
IN_PIECES = IN_TN // IN_PIECE


def _prenorm(x_ref, gain_ref, h_ref):
    gain = gain_ref[...]
    for r in range(0, IN_TM, IN_NORM_ROWS):
        xf = x_ref[r:r + IN_NORM_ROWS, :]
        ms = jnp.mean(xf * xf, axis=-1, keepdims=True)
        h_ref[r:r + IN_NORM_ROWS, :] = (xf * lax.rsqrt(ms + NORM_EPS) * gain).astype(BF16)


HG_TILE0 = OFF_HQ // IN_TN
HG_TILES = (OFF_GA - 1) // IN_TN - HG_TILE0 + 1
HG_COL0 = HG_TILE0 * IN_TN


def _copy_hgrn_tile(o_ref, hg_ref):
    col_step = pl.program_id(1)

    @pl.when((col_step >= HG_TILE0) & (col_step < HG_TILE0 + HG_TILES))
    def _():
        hg_ref[...] = o_ref[...]


def _inproj_kernel(x_ref, gain_ref, w_ref, o_ref, hg_ref, h_ref):
    @pl.when(pl.program_id(1) == 0)
    def _():
        _prenorm(x_ref, gain_ref, h_ref)

    o_ref[...] = jnp.dot(h_ref[...], w_ref[...], preferred_element_type=F32).astype(BF16)
    _copy_hgrn_tile(o_ref, hg_ref)


def _inproj_specs(row_tile0):
    return [
        pl.BlockSpec((IN_TM, D_MODEL), lambda i, j: (row_tile0 + i, 0)),
        pl.BlockSpec((1, D_MODEL), lambda i, j: (0, 0)),
        pl.BlockSpec((D_MODEL, IN_TN), lambda i, j: (0, j)),
    ]


def _inproj_out_specs(row_tile0):
    hg_tile = lambda j: jnp.clip(j - HG_TILE0, 0, HG_TILES - 1)
    return [pl.BlockSpec((IN_TM, IN_TN), lambda i, j: (row_tile0 + i, j)),
            pl.BlockSpec((IN_TM, IN_TN), lambda i, j: (i, hg_tile(j)))]


def _inproj_out_shapes(tokens, n_row_tiles):
    return [jax.ShapeDtypeStruct((tokens, IN_PROJ_WIDTH), BF16),
            jax.ShapeDtypeStruct((n_row_tiles * IN_TM, HG_TILES * IN_TN), BF16)]


def _inproj(x2d, gain, w_bf16, row_tile0, n_row_tiles):
    return pl.pallas_call(
        _inproj_kernel,
        grid=(n_row_tiles, IN_PROJ_WIDTH // IN_TN),
        in_specs=_inproj_specs(row_tile0),
        out_specs=_inproj_out_specs(row_tile0),
        out_shape=_inproj_out_shapes(x2d.shape[0], n_row_tiles),
        scratch_shapes=[pltpu.VMEM((IN_TM, D_MODEL), BF16)],
        compiler_params=pltpu.CompilerParams(
            dimension_semantics=("parallel", "arbitrary"),
            vmem_limit_bytes=56 * 1024 * 1024),
        name="inproj",
    )(x2d, gain, w_bf16)


def _level_reference(g_ref, level, sub):
    half = 1 << level
    blk = 2 * half

    def row_tile(r):
        return g_ref[pl.ds(r, SUBLANES, stride=0), :]

    if blk >= SUBLANES:
        return jnp.concatenate(
            [row_tile(b * blk + half) for b in range(CHUNK // blk) for _ in range(blk // SUBLANES)],
            axis=0)
    tiles = []
    for t in range(CHUNK // SUBLANES):
        acc = row_tile(t * SUBLANES + half)
        for b in range(1, SUBLANES // blk):
            acc = jnp.where(sub >= b * blk, row_tile(t * SUBLANES + b * blk + half), acc)
        tiles.append(acc)
    return jnp.concatenate(tiles, axis=0)


def _neg_abs(x):
    bits = lax.bitcast_convert_type(x, jnp.uint32) | jnp.uint32(0x80000000)
    return lax.bitcast_convert_type(bits, F32)


def _hgrn_kernel(n_carried, lb_ref, hn_ref, hq_ref, hf_ref, hi_ref, hg_ref, *rest):
    o_ref, *scratch = rest[n_carried:]
    for j in range(HGRN_HEADS_PER_STEP):
        cols = slice(j * HGRN_DIM, (j + 1) * HGRN_DIM)
        head = pl.program_id(1) * HGRN_HEADS_PER_STEP + j
        _hgrn_head(lb_ref.at[:, cols], hn_ref.at[pl.ds(head, 1), :], hq_ref.at[:, cols],
                   hf_ref.at[:, cols], hi_ref.at[:, cols], hg_ref.at[:, cols], o_ref.at[:, cols],
                   *scratch)


def _hgrn_head(lb_ref, hn_ref, hq_ref, hf_ref, hi_ref, hg_ref, o_ref,
               g_ref, qg_ref, attn_ref, upd_ref, keep_ref, state_ref, piece_fn=None, n_pieces=0):
    seq = hq_ref.shape[0]
    lg = lb_ref[...]
    ex = jnp.exp(lg - jnp.max(lg, axis=0, keepdims=True))
    lb = ex[0:1, :] / jnp.sum(ex, axis=0, keepdims=True)
    hn = hn_ref[...]

    rowi = lax.broadcasted_iota(jnp.int32, (CHUNK, HGRN_DIM), 0)
    sub = lax.broadcasted_iota(jnp.int32, (SUBLANES, HGRN_DIM), 0)
    ti = lax.broadcasted_iota(jnp.int32, (CHUNK, CHUNK), 0)
    si = lax.broadcasted_iota(jnp.int32, (CHUNK, CHUNK), 1)
    xor = ti ^ si
    lvl = jnp.full((CHUNK, CHUNK), -2, jnp.int32)
    for level in range(CHUNK_LEVELS):
        lvl = jnp.where(xor >= (1 << level), level, lvl)
    lvl = jnp.where(ti > si, lvl, jnp.where(ti == si, -1, -2))

    def chunk_rows(c):
        start = c * CHUNK
        return pl.ds(start if isinstance(c, int) else pl.multiple_of(start, CHUNK), CHUNK)

    def local_step(c, slot):
        rows = chunk_rows(c)
        q = _silu(hq_ref[rows, :].astype(F32))
        f = lb + (1.0 - lb) * _sigmoid(hf_ref[rows, :].astype(F32))
        k = 1.0 - f
        v = hi_ref[rows, :]
        g = jnp.log2(f)
        for shift in (1, 2, 4):
            g = g + jnp.where(rowi >= shift, pltpu.roll(g, shift, 0), 0.0)
        for shift in (8, 16, 32):
            g = g + jnp.concatenate(
                [jnp.zeros((shift, HGRN_DIM), F32), g[:CHUNK - shift]], axis=0)
        g_slot = g_ref.at[slot]
        g_slot[...] = g
        g_last = g_slot[pl.ds(CHUNK - 1, SUBLANES, stride=0), :][0:1]
        qb = q.astype(BF16)
        kb = k.astype(BF16)
        qg_ref[rows, :] = qb * jnp.exp2(g).astype(BF16)

        attn = jnp.where(lvl == -1, jnp.sum(q * k, axis=-1, keepdims=True), 0.0)
        for level in range(CHUNK_LEVELS):
            dec = jnp.exp2(_neg_abs(g - _level_reference(g_slot, level, sub))).astype(BF16)
            a = lax.dot_general(qb * dec, kb * dec, _NT, preferred_element_type=F32)
            attn = jnp.where(lvl == level, a, attn)
        attn_ref[c] = attn.astype(BF16)

        k_dec = kb * jnp.exp2(g_last - g).astype(BF16)
        upd_ref[c] = lax.dot_general(v, k_dec, _TN, preferred_element_type=F32)
        keep_ref[c] = jnp.broadcast_to(jnp.exp2(g_last), (SUBLANES, HGRN_DIM))

    def local_group(i, carry):
        for slot in range(HGRN_UNROLL):
            local_step(i * HGRN_UNROLL + slot, slot)
        return carry

    n_chunks = seq // CHUNK
    if n_pieces:
        bounds = [(p * n_chunks) // n_pieces for p in range(n_pieces + 1)]
        for p in range(n_pieces):
            piece_fn(p)
            for c in range(bounds[p], bounds[p + 1]):
                local_step(c, c - bounds[p])
    else:
        lax.fori_loop(0, n_chunks // HGRN_UNROLL, local_group, 0)

    def state_step(c, state_t):
        state_ref[c] = state_t.astype(BF16)
        return keep_ref[c][0:1] * state_t + upd_ref[c]

    lax.fori_loop(0, n_chunks, state_step, jnp.zeros((HGRN_DIM, HGRN_DIM), F32),
                  unroll=HGRN_UNROLL)

    def out_step(c, carry):
        rows = chunk_rows(c)
        o = lax.dot_general(qg_ref[rows, :], state_ref[c], _NT, preferred_element_type=F32)
        o = o + jnp.dot(attn_ref[c], hi_ref[rows, :], preferred_element_type=F32)
        ms = jnp.mean(o * o, axis=-1, keepdims=True)
        y = o * lax.rsqrt(ms + NORM_EPS) * hn * _silu(hg_ref[rows, :].astype(F32))
        o_ref[rows, :] = y.astype(BF16)
        return carry

    lax.fori_loop(0, n_chunks, out_step, 0, unroll=2 * HGRN_UNROLL)


def _hgrn_scratch(seq):
    n_chunks = seq // CHUNK
    return [
        pltpu.VMEM((HGRN_SLOTS, CHUNK, HGRN_DIM), F32),
        pltpu.VMEM((seq, HGRN_DIM), BF16),
        pltpu.VMEM((n_chunks, CHUNK, CHUNK), BF16),
        pltpu.VMEM((n_chunks, HGRN_DIM, HGRN_DIM), F32),
        pltpu.VMEM((n_chunks, SUBLANES, HGRN_DIM), F32),
        pltpu.VMEM((n_chunks, HGRN_DIM, HGRN_DIM), BF16),
    ]


def _hgrn(hg_seg, lb_logits, hgrn_norm, yh_prev, batch0, tokens, seq):
    width = HGRN_HEADS_PER_STEP * HGRN_DIM
    seg_batch = hg_seg.shape[0] // seq
    blk = lambda off: pl.BlockSpec((seq, width), lambda b, h: (b, (off - HG_COL0) // width + h))
    carried = [] if yh_prev is None else [yh_prev]
    return pl.pallas_call(
        functools.partial(_hgrn_kernel, len(carried)),
        grid=(seg_batch, HGRN_HEADS // HGRN_HEADS_PER_STEP),
        in_specs=[
            pl.BlockSpec((lb_logits.shape[0], width), lambda b, h: (0, h)),
            pl.BlockSpec((HGRN_HEADS, HGRN_DIM), lambda b, h: (0, 0)),
            blk(OFF_HQ), blk(OFF_HF), blk(OFF_HI), blk(OFF_HG),
        ] + [pl.BlockSpec(memory_space=pl.ANY) for _ in carried],
        out_specs=pl.BlockSpec((seq, width), lambda b, h: (batch0 + b, h)),
        out_shape=jax.ShapeDtypeStruct((tokens, HGRN_WIDTH), BF16),
        input_output_aliases={6: 0} if carried else {},
        scratch_shapes=_hgrn_scratch(seq),
        compiler_params=pltpu.CompilerParams(dimension_semantics=("parallel", "parallel")),
        name="hgrn2",
    )(lb_logits, hgrn_norm, hg_seg, hg_seg, hg_seg, hg_seg, *carried)


def _inproj_hgrn_kernel(heads_per_tile, n_carried, x_ref, gain_ref, w_ref, lb_ref, hn_ref, hq_ref,
                        hf_ref, hi_ref, hg_ref, *rest):
    o_ref, hg_out_ref, yh_ref, h_ref, *hgrn_scratch = rest[n_carried:]
    col_step = pl.program_id(1)

    @pl.when(col_step == 0)
    def _():
        _prenorm(x_ref, gain_ref, h_ref)

    def piece(p):
        cs = slice(p * IN_PIECE, (p + 1) * IN_PIECE)
        tile = jnp.dot(h_ref[...], w_ref[:, cs], preferred_element_type=F32).astype(BF16)
        o_ref[:, cs] = tile
        hg_out_ref[:, cs] = tile

    @pl.when(col_step < heads_per_tile)
    def _():
        head = (pl.program_id(0) * heads_per_tile + col_step) % HGRN_HEADS
        _hgrn_head(lb_ref, hn_ref.at[pl.ds(head, 1), :], hq_ref, hf_ref, hi_ref, hg_ref, yh_ref,
                   *hgrn_scratch, piece_fn=piece, n_pieces=IN_PIECES)

    @pl.when(col_step >= heads_per_tile)
    def _():
        o_ref[...] = jnp.dot(h_ref[...], w_ref[...], preferred_element_type=F32).astype(BF16)
        _copy_hgrn_tile(o_ref, hg_out_ref)


def _inproj_hgrn(x2d, gain, w_bf16, row_tile0, n_row_tiles, hg_prev, batch0_prev, proj_prev, yh_prev,
                 lb_logits, hgrn_norm, seq):
    tokens = x2d.shape[0]
    n_heads = (hg_prev.shape[0] // seq) * HGRN_HEADS
    heads_per_tile = n_heads // n_row_tiles
    assert heads_per_tile * n_row_tiles == n_heads
    assert heads_per_tile <= HG_TILE0 + HG_TILES
    assert pl.cdiv(seq // CHUNK, IN_PIECES) <= HGRN_SLOTS

    def flat_head(i, j):
        return i * heads_per_tile + jnp.minimum(j, heads_per_tile - 1)

    def blk(off):
        return pl.BlockSpec((seq, HGRN_DIM), lambda i, j: (
            flat_head(i, j) // HGRN_HEADS, (off - HG_COL0) // HGRN_DIM + flat_head(i, j) % HGRN_HEADS))

    carried = [proj_prev] + ([] if yh_prev is None else [yh_prev])
    n_read = 9
    aliases = {n_read: 0} if yh_prev is None else {n_read: 0, n_read + 1: 2}
    return pl.pallas_call(
        functools.partial(_inproj_hgrn_kernel, heads_per_tile, len(carried)),
        grid=(n_row_tiles, IN_PROJ_WIDTH // IN_TN),
        in_specs=_inproj_specs(row_tile0) + [
            pl.BlockSpec((lb_logits.shape[0], HGRN_DIM),
                         lambda i, j: (0, flat_head(i, j) % HGRN_HEADS)),
            pl.BlockSpec((HGRN_HEADS, HGRN_DIM), lambda i, j: (0, 0)),
            blk(OFF_HQ), blk(OFF_HF), blk(OFF_HI), blk(OFF_HG),
        ] + [pl.BlockSpec(memory_space=pl.ANY) for _ in carried],
        out_specs=_inproj_out_specs(row_tile0) + [
            pl.BlockSpec((seq, HGRN_DIM), lambda i, j: (
                batch0_prev + flat_head(i, j) // HGRN_HEADS, flat_head(i, j) % HGRN_HEADS))],
        out_shape=_inproj_out_shapes(tokens, n_row_tiles) + [
            jax.ShapeDtypeStruct((tokens, HGRN_WIDTH), BF16)],
        input_output_aliases=aliases,
        scratch_shapes=[pltpu.VMEM((IN_TM, D_MODEL), BF16)] + _hgrn_scratch(seq),
        compiler_params=pltpu.CompilerParams(
            dimension_semantics=("arbitrary", "arbitrary"),
            vmem_limit_bytes=58 * 1024 * 1024),
        name="inproj_hgrn2",
    )(x2d, gain, w_bf16, lb_logits, hgrn_norm, hg_prev, hg_prev, hg_prev, hg_prev, *carried)


SM_BLOCKS = 2
SM_TM = SM_BLOCKS * WINDOW
OUT_TN = 512
OUT_NCHUNK = D_MODEL // OUT_TN


def _t5_bucket(dist):
    max_exact = REL_BUCKETS // 2
    d = jnp.maximum(dist, 0)
    df = jnp.maximum(d, 1).astype(F32)
    large = max_exact + (jnp.log(df / max_exact) / math.log(REL_MAX_DIST / max_exact)
                         * (REL_BUCKETS - max_exact)).astype(jnp.int32)
    large = jnp.minimum(large, REL_BUCKETS - 1)
    return jnp.where(d < max_exact, d, large)


def _swap_halves(x):
    half = LANES // 2
    return jnp.concatenate([x[:, half:], x[:, :half]], axis=1)


def _swa_tables(relb_ref, bias_ref, cur_ref):
    qi = lax.broadcasted_iota(jnp.int32, (WINDOW, WINDOW), 0)
    sj = lax.broadcasted_iota(jnp.int32, (WINDOW, WINDOW), 1)
    own = sj <= qi
    bucket = _t5_bucket(jnp.where(own, qi - sj, qi + WINDOW - sj))
    for h in range(ATTN_HEADS):
        def pick(b, acc, h=h):
            return jnp.where(bucket == b, relb_ref[b, h], acc)
        tbl = lax.fori_loop(0, REL_BUCKETS, pick, jnp.zeros((WINDOW, WINDOW), F32))
        bias_ref[0, h] = tbl
        bias_ref[1, h] = jnp.where(own, tbl, MASK_VALUE)
    own_bf = jnp.where(own, 1.0, 0.0).astype(BF16)
    cur_ref[...] = jnp.concatenate([own_bf, own_bf], axis=0)


def _swa_merge_kernel(blocks_per_seq, n_tiles, relb_ref, sink_ref, q_ref, kc_ref, vc_ref, kp_ref,
                      vp_ref, ga_ref, gb_ref, yh_ref, *rest):
    gate_a_refs = rest[:OUT_NCHUNK]
    gate_h_refs = rest[OUT_NCHUNK:2 * OUT_NCHUNK]
    (x_ref, wa_ref, wh_ref, wo_ref, gain_ref, o_ref,
     bias_ref, cur_ref, ya_ref, m_ref) = rest[2 * OUT_NCHUNK:]
    step = pl.program_id(0)

    @pl.when(step == 0)
    def _():
        _swa_tables(relb_ref, bias_ref, cur_ref)
        ya_ref[...] = jnp.zeros_like(ya_ref)

    lane = lax.broadcasted_iota(jnp.int32, (1, LANES), 1)
    is_lo = lane < LANES // 2
    scale = ATTN_HEAD_DIM ** -0.5
    keep_lo = jnp.where(is_lo, 1.0, 0.0).astype(BF16)
    keep_hi = jnp.where(is_lo, 0.0, 1.0).astype(BF16)
    row = lax.broadcasted_iota(jnp.int32, (2 * WINDOW, 1), 0)
    top = row < WINDOW
    own2 = (lax.broadcasted_iota(jnp.int32, (2 * WINDOW, WINDOW), 1)
            <= (lax.broadcasted_iota(jnp.int32, (2 * WINDOW, WINDOW), 0) & (WINDOW - 1)))
    tile = jnp.minimum(step, n_tiles - 1)
    tile_starts_seq = jnp.where(tile % (blocks_per_seq // SM_BLOCKS) == 0, 1, 0)

    def scores(qm, km, first, heads):
        s2 = lax.dot_general(qm, km, _NT, preferred_element_type=F32)
        bias = jnp.concatenate([bias_ref[first, heads[0]], bias_ref[first, heads[1]]], axis=0)
        return jnp.where(own2, s2[:, WINDOW:], s2[:, :WINDOW]) + bias

    def probs(s, heads):
        sink = jnp.where(top, sink_ref[0, heads[0]], sink_ref[0, heads[1]])
        m = jnp.maximum(jnp.max(s, axis=-1, keepdims=True), sink)
        p = jnp.exp(s - m)
        denom = jnp.sum(p, axis=-1, keepdims=True) + jnp.exp(sink - m)
        pb = p.astype(BF16)
        p_own = pb * cur_ref[...]
        return jnp.concatenate([pb - p_own, p_own], axis=1), 1.0 / denom

    def heads_of(hk):
        e = hk % 2
        return (4 * hk + e, 4 * hk + 2 + e), (4 * hk + 1 - e, 4 * hk + 3 - e)

    ya_prev = ya_ref[...]
    yh = yh_ref[...]
    for j in range(OUT_NCHUNK):
        cs = slice(j * OUT_TN, (j + 1) * OUT_TN)
        ua = jnp.dot(ya_prev, wa_ref[:, cs], preferred_element_type=F32)
        uh = jnp.dot(yh, wh_ref[:, cs], preferred_element_type=F32)
        merged = (_sigmoid(gate_a_refs[j][...].astype(F32)) * ua
                  + _sigmoid(gate_h_refs[j][...].astype(F32)) * uh)
        m_ref[:, cs] = merged.astype(BF16)

    s_same, s_diff, vals = [], [], []
    for blk in range(SM_BLOCKS):
        rows = slice(blk * WINDOW, (blk + 1) * WINDOW)
        first = tile_starts_seq if blk == 0 else 0
        for hk in range(ATTN_KV_HEADS):
            c, e = divmod(hk, 2)
            cs = slice(c * LANES, (c + 1) * LANES)
            if blk == 0:
                k_prev, v_prev = kp_ref[:, cs], vp_ref[:, cs]
            else:
                prev_rows = slice((blk - 1) * WINDOW, blk * WINDOW)
                k_prev, v_prev = kc_ref[prev_rows, cs], vc_ref[prev_rows, cs]
            k2 = jnp.concatenate([k_prev, kc_ref[rows, cs]], axis=0)
            v2 = jnp.concatenate([v_prev, vc_ref[rows, cs]], axis=0)
            k2s, v2s = _swap_halves(k2), _swap_halves(v2)
            keep_e, keep_o = (keep_lo, keep_hi) if e == 0 else (keep_hi, keep_lo)
            p0 = q_ref[rows, hk * 2 * LANES:hk * 2 * LANES + LANES]
            p1 = q_ref[rows, hk * 2 * LANES + LANES:(hk + 1) * 2 * LANES]
            q_same = jnp.concatenate([p0 * (keep_e * scale), p1 * (keep_e * scale)], axis=0)
            q_diff = jnp.concatenate([p0 * (keep_o * scale), p1 * (keep_o * scale)], axis=0)
            h_same, h_diff = heads_of(hk)
            s_same.append(scores(q_same, k2, first, h_same))
            s_diff.append(scores(q_diff, k2s, first, h_diff))
            vals.append((v2 * keep_lo, v2s * keep_hi) if e == 0 else (v2s * keep_lo, v2 * keep_hi))

    y = jnp.dot(m_ref[...], wo_ref[...], preferred_element_type=F32)

    pr_same = [probs(s_same[i], heads_of(i % ATTN_KV_HEADS)[0]) for i in range(len(s_same))]
    pr_diff = [probs(s_diff[i], heads_of(i % ATTN_KV_HEADS)[1]) for i in range(len(s_diff))]
    for blk in range(SM_BLOCKS):
        rows = slice(blk * WINDOW, (blk + 1) * WINDOW)
        for hk in range(ATTN_KV_HEADS):
            i = blk * ATTN_KV_HEADS + hk
            (p_lo, r_lo), (p_hi, r_hi) = ((pr_same[i], pr_diff[i]) if hk % 2 == 0
                                          else (pr_diff[i], pr_same[i]))
            v_lo, v_hi = vals[i]
            pv = jnp.dot(jnp.concatenate([p_lo, p_hi], axis=1),
                         jnp.concatenate([v_lo, v_hi], axis=0),
                         preferred_element_type=F32)
            pv = pv * jnp.where(is_lo, r_lo, r_hi)
            g_ref = ga_ref if hk < 2 else gb_ref
            for pair in range(2):
                gs = slice((hk % 2) * 2 * LANES + pair * LANES,
                           (hk % 2) * 2 * LANES + (pair + 1) * LANES)
                gate = _silu(g_ref[rows, gs].astype(F32))
                ya_ref[rows, (2 * hk + pair) * LANES:(2 * hk + pair + 1) * LANES] = (
                    pv[pair * WINDOW:(pair + 1) * WINDOW] * gate).astype(BF16)

    ms = jnp.mean(y * y, axis=-1, keepdims=True)
    o_ref[...] = x_ref[...] + y * lax.rsqrt(ms + NORM_EPS) * gain_ref[...]


def _swa_merge(proj, yh, x2d, rel_bias, sinks, wa, wh, wo, gain, seq):
    tokens = x2d.shape[0]
    n_tiles = tokens // SM_TM
    blocks_per_seq = seq // WINDOW
    attn_tile = lambda t: jnp.minimum(t, n_tiles - 1)
    merge_tile = lambda t: jnp.maximum(t - 1, 0)
    prev_block = lambda t: jnp.maximum(attn_tile(t) * SM_BLOCKS - 1, 0)
    kcol, vcol = OFF_AK // KV_WIDTH, OFF_AV // KV_WIDTH
    gcol = OFF_AG // (ATTN_WIDTH // 2)
    const = lambda t: (0, 0)
    smem = pl.BlockSpec(memory_space=pltpu.SMEM)
    gate = lambda off, j: pl.BlockSpec((SM_TM, OUT_TN), lambda t: (merge_tile(t), off // OUT_TN + j))
    resident = functools.partial(pl.BlockSpec, index_map=const, pipeline_mode=pl.Buffered(1))
    return pl.pallas_call(
        functools.partial(_swa_merge_kernel, blocks_per_seq, n_tiles),
        grid=(n_tiles + 1,),
        in_specs=(
            [smem, smem,
             pl.BlockSpec((SM_TM, ATTN_WIDTH), lambda t: (attn_tile(t), 0)),
             pl.BlockSpec((SM_TM, KV_WIDTH), lambda t: (attn_tile(t), kcol)),
             pl.BlockSpec((SM_TM, KV_WIDTH), lambda t: (attn_tile(t), vcol)),
             pl.BlockSpec((WINDOW, KV_WIDTH), lambda t: (prev_block(t), kcol)),
             pl.BlockSpec((WINDOW, KV_WIDTH), lambda t: (prev_block(t), vcol)),
             pl.BlockSpec((SM_TM, ATTN_WIDTH // 2), lambda t: (attn_tile(t), gcol)),
             pl.BlockSpec((SM_TM, ATTN_WIDTH // 2), lambda t: (attn_tile(t), gcol + 1)),
             pl.BlockSpec((SM_TM, HGRN_WIDTH), lambda t: (merge_tile(t), 0))]
            + [gate(OFF_GA, j) for j in range(OUT_NCHUNK)]
            + [gate(OFF_GH, j) for j in range(OUT_NCHUNK)]
            + [pl.BlockSpec((SM_TM, D_MODEL), lambda t: (merge_tile(t), 0)),
               resident((ATTN_WIDTH, D_MODEL)), resident((HGRN_WIDTH, D_MODEL)),
               resident((D_MODEL, D_MODEL)), pl.BlockSpec((1, D_MODEL), const)]),
        out_specs=pl.BlockSpec((SM_TM, D_MODEL), lambda t: (merge_tile(t), 0)),
        out_shape=jax.ShapeDtypeStruct((tokens, D_MODEL), F32),
        scratch_shapes=[
            pltpu.VMEM((2, ATTN_HEADS, WINDOW, WINDOW), F32),
            pltpu.VMEM((2 * WINDOW, WINDOW), BF16),
            pltpu.VMEM((SM_TM, ATTN_WIDTH), BF16),
            pltpu.VMEM((SM_TM, D_MODEL), BF16),
        ],
        compiler_params=pltpu.CompilerParams(
            dimension_semantics=("arbitrary",), vmem_limit_bytes=52 * 1024 * 1024),
        name="swa_merge",
    )(rel_bias, sinks, proj, proj, proj, proj, proj, proj, proj, yh,
      *([proj] * (2 * OUT_NCHUNK)), x2d, wa, wh, wo, gain)


def kernel(x, norm_pre, w_in, rel_bias, attn_sinks, lb_logits, hgrn_norm, w_branch_attn,
           w_branch_hgrn, w_out, norm_post):
    batch, seq, d_model = x.shape
    depth = w_in.shape[0]
    assert depth == 1 and d_model == D_MODEL and seq % IN_TM == 0
    assert lb_logits.shape == (depth + 1, HGRN_WIDTH)
    assert sum(SEG_BATCHES) == batch
    x2d = x.reshape(batch * seq, d_model)
    layer = 0
    gain_pre = norm_pre[layer][None, :]
    w_bf16 = w_in[layer].astype(BF16)
    tiles_per_seq = seq // IN_TM
    proj, hg = _inproj(x2d, gain_pre, w_bf16, 0, SEG_BATCHES[0] * tiles_per_seq)
    yh = None
    batch0 = 0
    for prev_batches, seg_batches in zip(SEG_BATCHES[:-1], SEG_BATCHES[1:]):
        proj, hg, yh = _inproj_hgrn(
            x2d, gain_pre, w_bf16, (batch0 + prev_batches) * tiles_per_seq,
            seg_batches * tiles_per_seq, hg, batch0, proj, yh, lb_logits, hgrn_norm[layer], seq)
        batch0 += prev_batches
    yh = _hgrn(hg, lb_logits, hgrn_norm[layer], yh, batch0, batch * seq, seq)
    out = _swa_merge(proj, yh, x2d, rel_bias, attn_sinks[layer][None, :],
                     w_branch_attn[layer].astype(BF16), w_branch_hgrn[layer].astype(BF16),
                     w_out[layer].astype(BF16), norm_post[layer][None, :], seq)
    return out.reshape(batch, seq, d_model)
```

```python
import functools
import math

import jax
import jax.numpy as jnp
from jax import lax
from jax.experimental import pallas as pl
from jax.experimental.pallas import tpu as pltpu

D_MODEL = 2048
ATTN_HEADS = 16
ATTN_KV_HEADS = 4
ATTN_HEAD_DIM = 64
ATTN_GROUP = ATTN_HEADS // ATTN_KV_HEADS
WINDOW = 128
ATTN_WIDTH = ATTN_HEADS * ATTN_HEAD_DIM
KV_WIDTH = ATTN_KV_HEADS * ATTN_HEAD_DIM
HGRN_HEADS = 8
HGRN_DIM = 128
HGRN_WIDTH = HGRN_HEADS * HGRN_DIM
CHUNK = 64
CHUNK_LEVELS = 6
HGRN_UNROLL = 4
HGRN_SLOTS = 6
HGRN_HEADS_PER_STEP = 2
SEG_BATCHES = (1, 1, 2, 2, 2, 2, 2, 2, 2)
REL_BUCKETS = 32
REL_MAX_DIST = 128
NORM_EPS = 1e-6
IN_PROJ_WIDTH = 10752

OFF_AQ = 0
OFF_AK = OFF_AQ + ATTN_WIDTH
OFF_AV = OFF_AK + KV_WIDTH
OFF_AG = OFF_AV + KV_WIDTH
OFF_HQ = OFF_AG + ATTN_WIDTH
OFF_HF = OFF_HQ + HGRN_WIDTH
OFF_HI = OFF_HF + HGRN_WIDTH
OFF_HG = OFF_HI + HGRN_WIDTH
OFF_GA = OFF_HG + HGRN_WIDTH
OFF_GH = OFF_GA + D_MODEL

LANES = 128
SUBLANES = 8
MASK_VALUE = -1e30

F32 = jnp.float32
BF16 = jnp.bfloat16

_NT = (((1,), (1,)), ((), ()))
_TN = (((0,), (0,)), ((), ()))


def _sigmoid(x):
    return 1.0 / (1.0 + jnp.exp(-x))


def _silu(x):
    return x * _sigmoid(x)


IN_TM = 1024
IN_TN = 1536
IN_NORM_ROWS = 256
IN_PIECE = 256
IN_PIECES = IN_TN // IN_PIECE


def _prenorm(x_ref, gain_ref, h_ref):
    gain = gain_ref[...]
    for r in range(0, IN_TM, IN_NORM_ROWS):
        xf = x_ref[r:r + IN_NORM_ROWS, :]
        ms = jnp.mean(xf * xf, axis=-1, keepdims=True)
        h_ref[r:r + IN_NORM_ROWS, :] = (xf * lax.rsqrt(ms + NORM_EPS) * gain).astype(BF16)


HG_TILE0 = OFF_HQ // IN_TN
HG_TILES = (OFF_GA - 1) // IN_TN - HG_TILE0 + 1
HG_COL0 = HG_TILE0 * IN_TN


def _copy_hgrn_tile(o_ref, hg_ref):
    col_step = pl.program_id(1)

    @pl.when((col_step >= HG_TILE0) & (col_step < HG_TILE0 + HG_TILES))
    def _():
        hg_ref[...] = o_ref[...]


def _inproj_kernel(x_ref, gain_ref, w_ref, o_ref, hg_ref, h_ref):
    @pl.when(pl.program_id(1) == 0)
    def _():
        _prenorm(x_ref, gain_ref, h_ref)

    o_ref[...] = jnp.dot(h_ref[...], w_ref[...], preferred_element_type=F32).astype(BF16)
    _copy_hgrn_tile(o_ref, hg_ref)


def _inproj_specs(row_tile0):
    return [
        pl.BlockSpec((IN_TM, D_MODEL), lambda i, j: (row_tile0 + i, 0)),
        pl.BlockSpec((1, D_MODEL), lambda i, j: (0, 0)),
        pl.BlockSpec((D_MODEL, IN_TN), lambda i, j: (0, j)),
    ]


def _inproj_out_specs(row_tile0):
    hg_tile = lambda j: jnp.clip(j - HG_TILE0, 0, HG_TILES - 1)
    return [pl.BlockSpec((IN_TM, IN_TN), lambda i, j: (row_tile0 + i, j)),
            pl.BlockSpec((IN_TM, IN_TN), lambda i, j: (i, hg_tile(j)))]


def _inproj_out_shapes(tokens, n_row_tiles):
    return [jax.ShapeDtypeStruct((tokens, IN_PROJ_WIDTH), BF16),
            jax.ShapeDtypeStruct((n_row_tiles * IN_TM, HG_TILES * IN_TN), BF16)]


def _inproj(x2d, gain, w_bf16, row_tile0, n_row_tiles):
    return pl.pallas_call(
        _inproj_kernel,
        grid=(n_row_tiles, IN_PROJ_WIDTH // IN_TN),
        in_specs=_inproj_specs(row_tile0),
        out_specs=_inproj_out_specs(row_tile0),
        out_shape=_inproj_out_shapes(x2d.shape[0], n_row_tiles),
        scratch_shapes=[pltpu.VMEM((IN_TM, D_MODEL), BF16)],
        compiler_params=pltpu.CompilerParams(
            dimension_semantics=("parallel", "arbitrary"),
            vmem_limit_bytes=56 * 1024 * 1024),
        name="inproj",
    )(x2d, gain, w_bf16)


def _level_reference(g_ref, level, sub):
    half = 1 << level
    blk = 2 * half

    def row_tile(r):
        return g_ref[pl.ds(r, SUBLANES, stride=0), :]

    if blk >= SUBLANES:
        return jnp.concatenate(
            [row_tile(b * blk + half) for b in range(CHUNK // blk) for _ in range(blk // SUBLANES)],
            axis=0)
    tiles = []
    for t in range(CHUNK // SUBLANES):
        acc = row_tile(t * SUBLANES + half)
        for b in range(1, SUBLANES // blk):
            acc = jnp.where(sub >= b * blk, row_tile(t * SUBLANES + b * blk + half), acc)
        tiles.append(acc)
    return jnp.concatenate(tiles, axis=0)


def _neg_abs(x):
    bits = lax.bitcast_convert_type(x, jnp.uint32) | jnp.uint32(0x80000000)
    return lax.bitcast_convert_type(bits, F32)


def _hgrn_kernel(n_carried, lb_ref, hn_ref, hq_ref, hf_ref, hi_ref, hg_ref, *rest):
    o_ref, *scratch = rest[n_carried:]
    for j in range(HGRN_HEADS_PER_STEP):
        cols = slice(j * HGRN_DIM, (j + 1) * HGRN_DIM)
        head = pl.program_id(1) * HGRN_HEADS_PER_STEP + j
        _hgrn_head(lb_ref.at[:, cols], hn_ref.at[pl.ds(head, 1), :], hq_ref.at[:, cols],
                   hf_ref.at[:, cols], hi_ref.at[:, cols], hg_ref.at[:, cols], o_ref.at[:, cols],
                   *scratch)


def _hgrn_head(lb_ref, hn_ref, hq_ref, hf_ref, hi_ref, hg_ref, o_ref,
               g_ref, qg_ref, attn_ref, upd_ref, keep_ref, state_ref, piece_fn=None, n_pieces=0):
    seq = hq_ref.shape[0]
    lg = lb_ref[...]
    ex = jnp.exp(lg - jnp.max(lg, axis=0, keepdims=True))
    lb = ex[0:1, :] / jnp.sum(ex, axis=0, keepdims=True)
    hn = hn_ref[...]

    rowi = lax.broadcasted_iota(jnp.int32, (CHUNK, HGRN_DIM), 0)
    sub = lax.broadcasted_iota(jnp.int32, (SUBLANES, HGRN_DIM), 0)
    ti = lax.broadcasted_iota(jnp.int32, (CHUNK, CHUNK), 0)
    si = lax.broadcasted_iota(jnp.int32, (CHUNK, CHUNK), 1)
    xor = ti ^ si
    lvl = jnp.full((CHUNK, CHUNK), -2, jnp.int32)
    for level in range(CHUNK_LEVELS):
        lvl = jnp.where(xor >= (1 << level), level, lvl)
    lvl = jnp.where(ti > si, lvl, jnp.where(ti == si, -1, -2))

    def chunk_rows(c):
        start = c * CHUNK
        return pl.ds(start if isinstance(c, int) else pl.multiple_of(start, CHUNK), CHUNK)

    def local_step(c, slot):
        rows = chunk_rows(c)
        q = _silu(hq_ref[rows, :].astype(F32))
        f = lb + (1.0 - lb) * _sigmoid(hf_ref[rows, :].astype(F32))
        k = 1.0 - f
        v = hi_ref[rows, :]
        g = jnp.log2(f)
        for shift in (1, 2, 4):
            g = g + jnp.where(rowi >= shift, pltpu.roll(g, shift, 0), 0.0)
        for shift in (8, 16, 32):
            g = g + jnp.concatenate(
                [jnp.zeros((shift, HGRN_DIM), F32), g[:CHUNK - shift]], axis=0)
        g_slot = g_ref.at[slot]
        g_slot[...] = g
        g_last = g_slot[pl.ds(CHUNK - 1, SUBLANES, stride=0), :][0:1]
        qb = q.astype(BF16)
        kb = k.astype(BF16)
        qg_ref[rows, :] = qb * jnp.exp2(g).astype(BF16)

        attn = jnp.where(lvl == -1, jnp.sum(q * k, axis=-1, keepdims=True), 0.0)
        for level in range(CHUNK_LEVELS):
            dec = jnp.exp2(_neg_abs(g - _level_reference(g_slot, level, sub))).astype(BF16)
            a = lax.dot_general(qb * dec, kb * dec, _NT, preferred_element_type=F32)
            attn = jnp.where(lvl == level, a, attn)
        attn_ref[c] = attn.astype(BF16)

        k_dec = kb * jnp.exp2(g_last - g).astype(BF16)
        upd_ref[c] = lax.dot_general(v, k_dec, _TN, preferred_element_type=F32)
        keep_ref[c] = jnp.broadcast_to(jnp.exp2(g_last), (SUBLANES, HGRN_DIM))

    def local_group(i, carry):
        for slot in range(HGRN_UNROLL):
            local_step(i * HGRN_UNROLL + slot, slot)
        return carry

    def state_step(c, state_t):
        state_ref[c] = state_t.astype(BF16)
        return keep_ref[c][0:1] * state_t + upd_ref[c]

    def out_step(c, carry):
        rows = chunk_rows(c)
        o = lax.dot_general(qg_ref[rows, :], state_ref[c], _NT, preferred_element_type=F32)
        o = o + jnp.dot(attn_ref[c], hi_ref[rows, :], preferred_element_type=F32)
        ms = jnp.mean(o * o, axis=-1, keepdims=True)
        y = o * lax.rsqrt(ms + NORM_EPS) * hn * _silu(hg_ref[rows, :].astype(F32))
        o_ref[rows, :] = y.astype(BF16)
        return carry

    n_chunks = seq // CHUNK
    state0 = jnp.zeros((HGRN_DIM, HGRN_DIM), F32)
    if n_pieces:
        bounds = [(p * n_chunks) // n_pieces for p in range(n_pieces + 1)]
        state_t = state0
        for p in range(n_pieces):
            piece_fn(p)
            for c in range(bounds[p], bounds[p + 1]):
                local_step(c, c - bounds[p])
                if c >= 1:
                    state_t = state_step(c - 1, state_t)
                if c >= 2:
                    out_step(c - 2, 0)
        state_step(n_chunks - 1, state_t)
        out_step(n_chunks - 2, 0)
        out_step(n_chunks - 1, 0)
    else:
        lax.fori_loop(0, n_chunks // HGRN_UNROLL, local_group, 0)
        lax.fori_loop(0, n_chunks, state_step, state0, unroll=HGRN_UNROLL)
        lax.fori_loop(0, n_chunks, out_step, 0, unroll=2 * HGRN_UNROLL)


def _hgrn_scratch(seq):
    n_chunks = seq // CHUNK
    return [
        pltpu.VMEM((HGRN_SLOTS, CHUNK, HGRN_DIM), F32),
        pltpu.VMEM((seq, HGRN_DIM), BF16),
        pltpu.VMEM((n_chunks, CHUNK, CHUNK), BF16),
        pltpu.VMEM((n_chunks, HGRN_DIM, HGRN_DIM), F32),
        pltpu.VMEM((n_chunks, SUBLANES, HGRN_DIM), F32),
        pltpu.VMEM((n_chunks, HGRN_DIM, HGRN_DIM), BF16),
    ]


def _hgrn(hg_seg, lb_logits, hgrn_norm, yh_prev, batch0, tokens, seq):
    width = HGRN_HEADS_PER_STEP * HGRN_DIM
    seg_batch = hg_seg.shape[0] // seq
    blk = lambda off: pl.BlockSpec((seq, width), lambda b, h: (b, (off - HG_COL0) // width + h))
    carried = [] if yh_prev is None else [yh_prev]
    return pl.pallas_call(
        functools.partial(_hgrn_kernel, len(carried)),
        grid=(seg_batch, HGRN_HEADS // HGRN_HEADS_PER_STEP),
        in_specs=[
            pl.BlockSpec((lb_logits.shape[0], width), lambda b, h: (0, h)),
            pl.BlockSpec((HGRN_HEADS, HGRN_DIM), lambda b, h: (0, 0)),
            blk(OFF_HQ), blk(OFF_HF), blk(OFF_HI), blk(OFF_HG),
        ] + [pl.BlockSpec(memory_space=pl.ANY) for _ in carried],
        out_specs=pl.BlockSpec((seq, width), lambda b, h: (batch0 + b, h)),
        out_shape=jax.ShapeDtypeStruct((tokens, HGRN_WIDTH), BF16),
        input_output_aliases={6: 0} if carried else {},
        scratch_shapes=_hgrn_scratch(seq),
        compiler_params=pltpu.CompilerParams(dimension_semantics=("parallel", "parallel")),
        name="hgrn2",
    )(lb_logits, hgrn_norm, hg_seg, hg_seg, hg_seg, hg_seg, *carried)


def _inproj_hgrn_kernel(heads_per_tile, n_carried, x_ref, gain_ref, w_ref, lb_ref, hn_ref, hq_ref,
                        hf_ref, hi_ref, hg_ref, *rest):
    o_ref, hg_out_ref, yh_ref, h_ref, *hgrn_scratch = rest[n_carried:]
    col_step = pl.program_id(1)

    @pl.when(col_step == 0)
    def _():
        _prenorm(x_ref, gain_ref, h_ref)

    def piece(p):
        cs = slice(p * IN_PIECE, (p + 1) * IN_PIECE)
        tile = jnp.dot(h_ref[...], w_ref[:, cs], preferred_element_type=F32).astype(BF16)
        o_ref[:, cs] = tile
        hg_out_ref[:, cs] = tile

    @pl.when(col_step < heads_per_tile)
    def _():
        head = (pl.program_id(0) * heads_per_tile + col_step) % HGRN_HEADS
        _hgrn_head(lb_ref, hn_ref.at[pl.ds(head, 1), :], hq_ref, hf_ref, hi_ref, hg_ref, yh_ref,
                   *hgrn_scratch, piece_fn=piece, n_pieces=IN_PIECES)

    @pl.when(col_step >= heads_per_tile)
    def _():
        o_ref[...] = jnp.dot(h_ref[...], w_ref[...], preferred_element_type=F32).astype(BF16)
        _copy_hgrn_tile(o_ref, hg_out_ref)


def _inproj_hgrn(x2d, gain, w_bf16, row_tile0, n_row_tiles, hg_prev, batch0_prev, proj_prev, yh_prev,
                 lb_logits, hgrn_norm, seq):
    tokens = x2d.shape[0]
    n_heads = (hg_prev.shape[0] // seq) * HGRN_HEADS
    heads_per_tile = n_heads // n_row_tiles
    assert heads_per_tile * n_row_tiles == n_heads
    assert heads_per_tile <= HG_TILE0 + HG_TILES
    assert pl.cdiv(seq // CHUNK, IN_PIECES) <= HGRN_SLOTS

    def flat_head(i, j):
        return i * heads_per_tile + jnp.minimum(j, heads_per_tile - 1)

    def blk(off):
        return pl.BlockSpec((seq, HGRN_DIM), lambda i, j: (
            flat_head(i, j) // HGRN_HEADS, (off - HG_COL0) // HGRN_DIM + flat_head(i, j) % HGRN_HEADS))

    carried = [proj_prev] + ([] if yh_prev is None else [yh_prev])
    n_read = 9
    aliases = {n_read: 0} if yh_prev is None else {n_read: 0, n_read + 1: 2}
    return pl.pallas_call(
        functools.partial(_inproj_hgrn_kernel, heads_per_tile, len(carried)),
        grid=(n_row_tiles, IN_PROJ_WIDTH // IN_TN),
        in_specs=_inproj_specs(row_tile0) + [
            pl.BlockSpec((lb_logits.shape[0], HGRN_DIM),
                         lambda i, j: (0, flat_head(i, j) % HGRN_HEADS)),
            pl.BlockSpec((HGRN_HEADS, HGRN_DIM), lambda i, j: (0, 0)),
            blk(OFF_HQ), blk(OFF_HF), blk(OFF_HI), blk(OFF_HG),
        ] + [pl.BlockSpec(memory_space=pl.ANY) for _ in carried],
        out_specs=_inproj_out_specs(row_tile0) + [
            pl.BlockSpec((seq, HGRN_DIM), lambda i, j: (
                batch0_prev + flat_head(i, j) // HGRN_HEADS, flat_head(i, j) % HGRN_HEADS))],
        out_shape=_inproj_out_shapes(tokens, n_row_tiles) + [
            jax.ShapeDtypeStruct((tokens, HGRN_WIDTH), BF16)],
        input_output_aliases=aliases,
        scratch_shapes=[pltpu.VMEM((IN_TM, D_MODEL), BF16)] + _hgrn_scratch(seq),
        compiler_params=pltpu.CompilerParams(
            dimension_semantics=("arbitrary", "arbitrary"),
            vmem_limit_bytes=58 * 1024 * 1024),
        name="inproj_hgrn2",
    )(x2d, gain, w_bf16, lb_logits, hgrn_norm, hg_prev, hg_prev, hg_prev, hg_prev, *carried)


SM_BLOCKS = 2
SM_TM = SM_BLOCKS * WINDOW
OUT_TN = 512
OUT_NCHUNK = D_MODEL // OUT_TN


def _t5_bucket(dist):
    max_exact = REL_BUCKETS // 2
    d = jnp.maximum(dist, 0)
    df = jnp.maximum(d, 1).astype(F32)
    large = max_exact + (jnp.log(df / max_exact) / math.log(REL_MAX_DIST / max_exact)
                         * (REL_BUCKETS - max_exact)).astype(jnp.int32)
    large = jnp.minimum(large, REL_BUCKETS - 1)
    return jnp.where(d < max_exact, d, large)


def _swap_halves(x):
    half = LANES // 2
    return jnp.concatenate([x[:, half:], x[:, :half]], axis=1)


def _swa_tables(relb_ref, bias_ref, cur_ref):
    qi = lax.broadcasted_iota(jnp.int32, (WINDOW, WINDOW), 0)
    sj = lax.broadcasted_iota(jnp.int32, (WINDOW, WINDOW), 1)
    own = sj <= qi
    bucket = _t5_bucket(jnp.where(own, qi - sj, qi + WINDOW - sj))
    for h in range(ATTN_HEADS):
        def pick(b, acc, h=h):
            return jnp.where(bucket == b, relb_ref[b, h], acc)
        tbl = lax.fori_loop(0, REL_BUCKETS, pick, jnp.zeros((WINDOW, WINDOW), F32))
        bias_ref[0, h] = tbl
        bias_ref[1, h] = jnp.where(own, tbl, MASK_VALUE)
    own_bf = jnp.where(own, 1.0, 0.0).astype(BF16)
    cur_ref[...] = jnp.concatenate([own_bf, own_bf], axis=0)


def _swa_merge_kernel(blocks_per_seq, n_tiles, relb_ref, sink_ref, q_ref, kc_ref, vc_ref, kp_ref,
                      vp_ref, ga_ref, gb_ref, yh_ref, *rest):
    gate_a_refs = rest[:OUT_NCHUNK]
    gate_h_refs = rest[OUT_NCHUNK:2 * OUT_NCHUNK]
    (x_ref, wa_ref, wh_ref, wo_ref, gain_ref, o_ref,
     bias_ref, cur_ref, ya_ref, m_ref) = rest[2 * OUT_NCHUNK:]
    step = pl.program_id(0)

    @pl.when(step == 0)
    def _():
        _swa_tables(relb_ref, bias_ref, cur_ref)
        ya_ref[...] = jnp.zeros_like(ya_ref)

    lane = lax.broadcasted_iota(jnp.int32, (1, LANES), 1)
    is_lo = lane < LANES // 2
    scale = ATTN_HEAD_DIM ** -0.5
    keep_lo = jnp.where(is_lo, 1.0, 0.0).astype(BF16)
    keep_hi = jnp.where(is_lo, 0.0, 1.0).astype(BF16)
    row = lax.broadcasted_iota(jnp.int32, (2 * WINDOW, 1), 0)
    top = row < WINDOW
    own2 = (lax.broadcasted_iota(jnp.int32, (2 * WINDOW, WINDOW), 1)
            <= (lax.broadcasted_iota(jnp.int32, (2 * WINDOW, WINDOW), 0) & (WINDOW - 1)))
    tile = jnp.minimum(step, n_tiles - 1)
    tile_starts_seq = jnp.where(tile % (blocks_per_seq // SM_BLOCKS) == 0, 1, 0)

    def scores(qm, km, first, heads):
        s2 = lax.dot_general(qm, km, _NT, preferred_element_type=F32)
        bias = jnp.concatenate([bias_ref[first, heads[0]], bias_ref[first, heads[1]]], axis=0)
        return jnp.where(own2, s2[:, WINDOW:], s2[:, :WINDOW]) + bias

    def probs(s, heads):
        sink = jnp.where(top, sink_ref[0, heads[0]], sink_ref[0, heads[1]])
        m = jnp.maximum(jnp.max(s, axis=-1, keepdims=True), sink)
        p = jnp.exp(s - m)
        denom = jnp.sum(p, axis=-1, keepdims=True) + jnp.exp(sink - m)
        pb = p.astype(BF16)
        p_own = pb * cur_ref[...]
        return jnp.concatenate([pb - p_own, p_own], axis=1), 1.0 / denom

    def heads_of(hk):
        e = hk % 2
        return (4 * hk + e, 4 * hk + 2 + e), (4 * hk + 1 - e, 4 * hk + 3 - e)

    ya_prev = ya_ref[...]
    yh = yh_ref[...]
    for j in range(OUT_NCHUNK):
        cs = slice(j * OUT_TN, (j + 1) * OUT_TN)
        ua = jnp.dot(ya_prev, wa_ref[:, cs], preferred_element_type=F32)
        uh = jnp.dot(yh, wh_ref[:, cs], preferred_element_type=F32)
        merged = (_sigmoid(gate_a_refs[j][...].astype(F32)) * ua
                  + _sigmoid(gate_h_refs[j][...].astype(F32)) * uh)
        m_ref[:, cs] = merged.astype(BF16)

    s_same, s_diff, vals = [], [], []
    for blk in range(SM_BLOCKS):
        rows = slice(blk * WINDOW, (blk + 1) * WINDOW)
        first = tile_starts_seq if blk == 0 else 0
        for hk in range(ATTN_KV_HEADS):
            c, e = divmod(hk, 2)
            cs = slice(c * LANES, (c + 1) * LANES)
            if blk == 0:
                k_prev, v_prev = kp_ref[:, cs], vp_ref[:, cs]
            else:
                prev_rows = slice((blk - 1) * WINDOW, blk * WINDOW)
                k_prev, v_prev = kc_ref[prev_rows, cs], vc_ref[prev_rows, cs]
            k2 = jnp.concatenate([k_prev, kc_ref[rows, cs]], axis=0)
            v2 = jnp.concatenate([v_prev, vc_ref[rows, cs]], axis=0)
            k2s, v2s = _swap_halves(k2), _swap_halves(v2)
            keep_e, keep_o = (keep_lo, keep_hi) if e == 0 else (keep_hi, keep_lo)
            p0 = q_ref[rows, hk * 2 * LANES:hk * 2 * LANES + LANES]
            p1 = q_ref[rows, hk * 2 * LANES + LANES:(hk + 1) * 2 * LANES]
            q_same = jnp.concatenate([p0 * (keep_e * scale), p1 * (keep_e * scale)], axis=0)
            q_diff = jnp.concatenate([p0 * (keep_o * scale), p1 * (keep_o * scale)], axis=0)
            h_same, h_diff = heads_of(hk)
            s_same.append(scores(q_same, k2, first, h_same))
            s_diff.append(scores(q_diff, k2s, first, h_diff))
            vals.append((v2 * keep_lo, v2s * keep_hi) if e == 0 else (v2s * keep_lo, v2 * keep_hi))

    y = jnp.dot(m_ref[...], wo_ref[...], preferred_element_type=F32)

    pr_same = [probs(s_same[i], heads_of(i % ATTN_KV_HEADS)[0]) for i in range(len(s_same))]
    pr_diff = [probs(s_diff[i], heads_of(i % ATTN_KV_HEADS)[1]) for i in range(len(s_diff))]
    for blk in range(SM_BLOCKS):
        rows = slice(blk * WINDOW, (blk + 1) * WINDOW)
        for hk in range(ATTN_KV_HEADS):
            i = blk * ATTN_KV_HEADS + hk
            (p_lo, r_lo), (p_hi, r_hi) = ((pr_same[i], pr_diff[i]) if hk % 2 == 0
                                          else (pr_diff[i], pr_same[i]))
            v_lo, v_hi = vals[i]
            pv = jnp.dot(jnp.concatenate([p_lo, p_hi], axis=1),
                         jnp.concatenate([v_lo, v_hi], axis=0),
                         preferred_element_type=F32)
            pv = pv * jnp.where(is_lo, r_lo, r_hi)
            g_ref = ga_ref if hk < 2 else gb_ref
            for pair in range(2):
                gs = slice((hk % 2) * 2 * LANES + pair * LANES,
                           (hk % 2) * 2 * LANES + (pair + 1) * LANES)
                gate = _silu(g_ref[rows, gs].astype(F32))
                ya_ref[rows, (2 * hk + pair) * LANES:(2 * hk + pair + 1) * LANES] = (
                    pv[pair * WINDOW:(pair + 1) * WINDOW] * gate).astype(BF16)

    ms = jnp.mean(y * y, axis=-1, keepdims=True)
    o_ref[...] = x_ref[...] + y * lax.rsqrt(ms + NORM_EPS) * gain_ref[...]


def _swa_merge(proj, yh, x2d, rel_bias, sinks, wa, wh, wo, gain, seq):
    tokens = x2d.shape[0]
    n_tiles = tokens // SM_TM
    blocks_per_seq = seq // WINDOW
    attn_tile = lambda t: jnp.minimum(t, n_tiles - 1)
    merge_tile = lambda t: jnp.maximum(t - 1, 0)
    prev_block = lambda t: jnp.maximum(attn_tile(t) * SM_BLOCKS - 1, 0)
    kcol, vcol = OFF_AK // KV_WIDTH, OFF_AV // KV_WIDTH
    gcol = OFF_AG // (ATTN_WIDTH // 2)
    const = lambda t: (0, 0)
    smem = pl.BlockSpec(memory_space=pltpu.SMEM)
    gate = lambda off, j: pl.BlockSpec((SM_TM, OUT_TN), lambda t: (merge_tile(t), off // OUT_TN + j))
    resident = functools.partial(pl.BlockSpec, index_map=const, pipeline_mode=pl.Buffered(1))
    return pl.pallas_call(
        functools.partial(_swa_merge_kernel, blocks_per_seq, n_tiles),
        grid=(n_tiles + 1,),
        in_specs=(
            [smem, smem,
             pl.BlockSpec((SM_TM, ATTN_WIDTH), lambda t: (attn_tile(t), 0)),
             pl.BlockSpec((SM_TM, KV_WIDTH), lambda t: (attn_tile(t), kcol)),
             pl.BlockSpec((SM_TM, KV_WIDTH), lambda t: (attn_tile(t), vcol)),
             pl.BlockSpec((WINDOW, KV_WIDTH), lambda t: (prev_block(t), kcol)),
             pl.BlockSpec((WINDOW, KV_WIDTH), lambda t: (prev_block(t), vcol)),
             pl.BlockSpec((SM_TM, ATTN_WIDTH // 2), lambda t: (attn_tile(t), gcol)),
             pl.BlockSpec((SM_TM, ATTN_WIDTH // 2), lambda t: (attn_tile(t), gcol + 1)),
             pl.BlockSpec((SM_TM, HGRN_WIDTH), lambda t: (merge_tile(t), 0))]
            + [gate(OFF_GA, j) for j in range(OUT_NCHUNK)]
            + [gate(OFF_GH, j) for j in range(OUT_NCHUNK)]
            + [pl.BlockSpec((SM_TM, D_MODEL), lambda t: (merge_tile(t), 0)),
               resident((ATTN_WIDTH, D_MODEL)), resident((HGRN_WIDTH, D_MODEL)),
               resident((D_MODEL, D_MODEL)), pl.BlockSpec((1, D_MODEL), const)]),
        out_specs=pl.BlockSpec((SM_TM, D_MODEL), lambda t: (merge_tile(t), 0)),
        out_shape=jax.ShapeDtypeStruct((tokens, D_MODEL), F32),
        scratch_shapes=[
            pltpu.VMEM((2, ATTN_HEADS, WINDOW, WINDOW), F32),
            pltpu.VMEM((2 * WINDOW, WINDOW), BF16),
            pltpu.VMEM((SM_TM, ATTN_WIDTH), BF16),
            pltpu.VMEM((SM_TM, D_MODEL), BF16),
        ],
        compiler_params=pltpu.CompilerParams(
            dimension_semantics=("arbitrary",), vmem_limit_bytes=52 * 1024 * 1024),
        name="swa_merge",
    )(rel_bias, sinks, proj, proj, proj, proj, proj, proj, proj, yh,
      *([proj] * (2 * OUT_NCHUNK)), x2d, wa, wh, wo, gain)


def kernel(x, norm_pre, w_in, rel_bias, attn_sinks, lb_logits, hgrn_norm, w_branch_attn,
           w_branch_hgrn, w_out, norm_post):
    batch, seq, d_model = x.shape
    depth = w_in.shape[0]
    assert depth == 1 and d_model == D_MODEL and seq % IN_TM == 0
    assert lb_logits.shape == (depth + 1, HGRN_WIDTH)
    assert sum(SEG_BATCHES) == batch
    x2d = x.reshape(batch * seq, d_model)
    layer = 0
    gain_pre = norm_pre[layer][None, :]
    w_bf16 = w_in[layer].astype(BF16)
    tiles_per_seq = seq // IN_TM
    proj, hg = _inproj(x2d, gain_pre, w_bf16, 0, SEG_BATCHES[0] * tiles_per_seq)
    yh = None
    batch0 = 0
    for prev_batches, seg_batches in zip(SEG_BATCHES[:-1], SEG_BATCHES[1:]):
        proj, hg, yh = _inproj_hgrn(
            x2d, gain_pre, w_bf16, (batch0 + prev_batches) * tiles_per_seq,
            seg_batches * tiles_per_seq, hg, batch0, proj, yh, lb_logits, hgrn_norm[layer], seq)
        batch0 += prev_batches
    yh = _hgrn(hg, lb_logits, hgrn_norm[layer], yh, batch0, batch * seq, seq)
    out = _swa_merge(proj, yh, x2d, rel_bias, attn_sinks[layer][None, :],
                     w_branch_attn[layer].astype(BF16), w_branch_hgrn[layer].astype(BF16),
                     w_out[layer].astype(BF16), norm_post[layer][None, :], seq)
    return out.reshape(batch, seq, d_model)
```

```python
import functools
import math

import jax
import jax.numpy as jnp
from jax import lax
from jax.experimental import pallas as pl
from jax.experimental.pallas import tpu as pltpu

D_MODEL = 2048
ATTN_HEADS = 16
ATTN_KV_HEADS = 4
ATTN_HEAD_DIM = 64
WINDOW = 128
ATTN_WIDTH = ATTN_HEADS * ATTN_HEAD_DIM
KV_WIDTH = ATTN_KV_HEADS * ATTN_HEAD_DIM
HGRN_HEADS = 8
HGRN_DIM = 128
HGRN_WIDTH = HGRN_HEADS * HGRN_DIM
CHUNK = 64
CHUNK_LEVELS = 6
HGRN_UNROLL = 4
HGRN_SLOTS = 6
HGRN_HEADS_PER_STEP = 2
SEG_BATCHES = (2,) * 8
REL_BUCKETS = 32
REL_MAX_DIST = 128
NORM_EPS = 1e-6
IN_PROJ_WIDTH = 10752

OFF_AQ = 0
OFF_AK = OFF_AQ + ATTN_WIDTH
OFF_AV = OFF_AK + KV_WIDTH
OFF_AG = OFF_AV + KV_WIDTH
OFF_HQ = OFF_AG + ATTN_WIDTH
OFF_HF = OFF_HQ + HGRN_WIDTH
OFF_HI = OFF_HF + HGRN_WIDTH
OFF_HG = OFF_HI + HGRN_WIDTH
OFF_GA = OFF_HG + HGRN_WIDTH
OFF_GH = OFF_GA + D_MODEL

LANES = 128
SUBLANES = 8
MASK_VALUE = -1e30

F32 = jnp.float32
BF16 = jnp.bfloat16

_NT = (((1,), (1,)), ((), ()))
_TN = (((0,), (0,)), ((), ()))


def _sigmoid(x):
    return 1.0 / (1.0 + jnp.exp(-x))


def _silu(x):
    return x * _sigmoid(x)


IN_TM = 1024
IN_TN = 1536
IN_NORM_ROWS = 256
IN_PIECE = 256
IN_PIECES = IN_TN // IN_PIECE
MIB = 1024 * 1024
IN_VMEM_BYTES = 56 * MIB
IN_HGRN_VMEM_BYTES = 58 * MIB


def _prenorm(x_ref, gain_ref, h_ref):
    gain = gain_ref[...]
    for r in range(0, IN_TM, IN_NORM_ROWS):
        xf = x_ref[r:r + IN_NORM_ROWS, :]
        ms = jnp.mean(xf * xf, axis=-1, keepdims=True)
        h_ref[r:r + IN_NORM_ROWS, :] = (xf * lax.rsqrt(ms + NORM_EPS) * gain).astype(BF16)


HG_TILE0 = OFF_HQ // IN_TN
HG_TILES = (OFF_GA - 1) // IN_TN - HG_TILE0 + 1
HG_COL0 = HG_TILE0 * IN_TN


def _copy_hgrn_tile(o_ref, hg_ref):
    col_step = pl.program_id(1)

    @pl.when((col_step >= HG_TILE0) & (col_step < HG_TILE0 + HG_TILES))
    def _():
        hg_ref[...] = o_ref[...]


def _inproj_kernel(x_ref, gain_ref, w_ref, o_ref, hg_ref, h_ref):
    @pl.when(pl.program_id(1) == 0)
    def _():
        _prenorm(x_ref, gain_ref, h_ref)

    o_ref[...] = jnp.dot(h_ref[...], w_ref[...], preferred_element_type=F32).astype(BF16)
    _copy_hgrn_tile(o_ref, hg_ref)


def _inproj_specs(row_tile0):
    return [
        pl.BlockSpec((IN_TM, D_MODEL), lambda i, j: (row_tile0 + i, 0)),
        pl.BlockSpec((1, D_MODEL), lambda i, j: (0, 0)),
        pl.BlockSpec((D_MODEL, IN_TN), lambda i, j: (0, j)),
    ]


def _inproj_out_specs(row_tile0):
    hg_tile = lambda j: jnp.clip(j - HG_TILE0, 0, HG_TILES - 1)
    return [pl.BlockSpec((IN_TM, IN_TN), lambda i, j: (row_tile0 + i, j)),
            pl.BlockSpec((IN_TM, IN_TN), lambda i, j: (i, hg_tile(j)))]


def _inproj_out_shapes(tokens, n_row_tiles):
    return [jax.ShapeDtypeStruct((tokens, IN_PROJ_WIDTH), BF16),
            jax.ShapeDtypeStruct((n_row_tiles * IN_TM, HG_TILES * IN_TN), BF16)]


def _inproj(x2d, gain, w_bf16, row_tile0, n_row_tiles):
    return pl.pallas_call(
        _inproj_kernel,
        grid=(n_row_tiles, IN_PROJ_WIDTH // IN_TN),
        in_specs=_inproj_specs(row_tile0),
        out_specs=_inproj_out_specs(row_tile0),
        out_shape=_inproj_out_shapes(x2d.shape[0], n_row_tiles),
        scratch_shapes=[pltpu.VMEM((IN_TM, D_MODEL), BF16)],
        compiler_params=pltpu.CompilerParams(
            dimension_semantics=("parallel", "arbitrary"),
            vmem_limit_bytes=IN_VMEM_BYTES),
        name="inproj",
    )(x2d, gain, w_bf16)


def _level_reference(g_ref, level, sub):
    half = 1 << level
    blk = 2 * half

    def row_tile(r):
        return g_ref[pl.ds(r, SUBLANES, stride=0), :]

    if blk >= SUBLANES:
        return jnp.concatenate(
            [row_tile(b * blk + half) for b in range(CHUNK // blk) for _ in range(blk // SUBLANES)],
            axis=0)
    tiles = []
    for t in range(CHUNK // SUBLANES):
        acc = row_tile(t * SUBLANES + half)
        for b in range(1, SUBLANES // blk):
            acc = jnp.where(sub >= b * blk, row_tile(t * SUBLANES + b * blk + half), acc)
        tiles.append(acc)
    return jnp.concatenate(tiles, axis=0)


def _neg_abs(x):
    bits = lax.bitcast_convert_type(x, jnp.uint32) | jnp.uint32(0x80000000)
    return lax.bitcast_convert_type(bits, F32)


def _hgrn_kernel(n_carried, lb_ref, hn_ref, hq_ref, hf_ref, hi_ref, hg_ref, *rest):
    o_ref, *scratch = rest[n_carried:]
    for j in range(HGRN_HEADS_PER_STEP):
        cols = slice(j * HGRN_DIM, (j + 1) * HGRN_DIM)
        head = pl.program_id(1) * HGRN_HEADS_PER_STEP + j
        _hgrn_head(lb_ref.at[:, cols], hn_ref.at[pl.ds(head, 1), :], hq_ref.at[:, cols],
                   hf_ref.at[:, cols], hi_ref.at[:, cols], hg_ref.at[:, cols], o_ref.at[:, cols],
                   *scratch)


def _hgrn_head(lb_ref, hn_ref, hq_ref, hf_ref, hi_ref, hg_ref, o_ref,
               g_ref, qg_ref, attn_ref, upd_ref, keep_ref, state_ref, piece_fn=None, n_pieces=0):
    seq = hq_ref.shape[0]
    lg = lb_ref[...]
    ex = jnp.exp(lg - jnp.max(lg, axis=0, keepdims=True))
    lb = ex[0:1, :] / jnp.sum(ex, axis=0, keepdims=True)
    hn = hn_ref[...]

    rowi = lax.broadcasted_iota(jnp.int32, (CHUNK, HGRN_DIM), 0)
    sub = lax.broadcasted_iota(jnp.int32, (SUBLANES, HGRN_DIM), 0)
    ti = lax.broadcasted_iota(jnp.int32, (CHUNK, CHUNK), 0)
    si = lax.broadcasted_iota(jnp.int32, (CHUNK, CHUNK), 1)
    xor = ti ^ si
    lvl = jnp.full((CHUNK, CHUNK), -2, jnp.int32)
    for level in range(CHUNK_LEVELS):
        lvl = jnp.where(xor >= (1 << level), level, lvl)
    lvl = jnp.where(ti > si, lvl, jnp.where(ti == si, -1, -2))

    def chunk_rows(c):
        start = c * CHUNK
        return pl.ds(start if isinstance(c, int) else pl.multiple_of(start, CHUNK), CHUNK)

    def local_step(c, slot):
        rows = chunk_rows(c)
        q = _silu(hq_ref[rows, :].astype(F32))
        f = lb + (1.0 - lb) * _sigmoid(hf_ref[rows, :].astype(F32))
        k = 1.0 - f
        v = hi_ref[rows, :]
        g = jnp.log2(f)
        for shift in (1, 2, 4):
            g = g + jnp.where(rowi >= shift, pltpu.roll(g, shift, 0), 0.0)
        for shift in (8, 16, 32):
            g = g + jnp.concatenate(
                [jnp.zeros((shift, HGRN_DIM), F32), g[:CHUNK - shift]], axis=0)
        g_slot = g_ref.at[slot]
        g_slot[...] = g
        g_last = g_slot[pl.ds(CHUNK - 1, SUBLANES, stride=0), :][0:1]
        qb = q.astype(BF16)
        kb = k.astype(BF16)
        qg_ref[rows, :] = qb * jnp.exp2(g).astype(BF16)

        attn = jnp.where(lvl == -1, jnp.sum(q * k, axis=-1, keepdims=True), 0.0)
        for level in range(CHUNK_LEVELS):
            dec = jnp.exp2(_neg_abs(g - _level_reference(g_slot, level, sub))).astype(BF16)
            a = lax.dot_general(qb * dec, kb * dec, _NT, preferred_element_type=F32)
            attn = jnp.where(lvl == level, a, attn)
        attn_ref[c] = attn.astype(BF16)

        k_dec = kb * jnp.exp2(g_last - g).astype(BF16)
        upd_ref[c] = lax.dot_general(v, k_dec, _TN, preferred_element_type=F32)
        keep_ref[c] = jnp.broadcast_to(jnp.exp2(g_last), (SUBLANES, HGRN_DIM))

    def local_group(i, carry):
        for slot in range(HGRN_UNROLL):
            local_step(i * HGRN_UNROLL + slot, slot)
        return carry

    def state_step(c, state_t):
        state_ref[c] = state_t.astype(BF16)
        return keep_ref[c][0:1] * state_t + upd_ref[c]

    def out_step(c, carry):
        rows = chunk_rows(c)
        o = lax.dot_general(qg_ref[rows, :], state_ref[c], _NT, preferred_element_type=F32)
        o = o + jnp.dot(attn_ref[c], hi_ref[rows, :], preferred_element_type=F32)
        ms = jnp.mean(o * o, axis=-1, keepdims=True)
        y = o * lax.rsqrt(ms + NORM_EPS) * hn * _silu(hg_ref[rows, :].astype(F32))
        o_ref[rows, :] = y.astype(BF16)
        return carry

    n_chunks = seq // CHUNK
    state0 = jnp.zeros((HGRN_DIM, HGRN_DIM), F32)
    if n_pieces:
        bounds = [(p * n_chunks) // n_pieces for p in range(n_pieces + 1)]
        state_t = state0
        for p in range(n_pieces):
            piece_fn(p)
            for c in range(bounds[p], bounds[p + 1]):
                local_step(c, c - bounds[p])
                if c >= 1:
                    state_t = state_step(c - 1, state_t)
                if c >= 2:
                    out_step(c - 2, 0)
        state_step(n_chunks - 1, state_t)
        out_step(n_chunks - 2, 0)
        out_step(n_chunks - 1, 0)
    else:
        lax.fori_loop(0, n_chunks // HGRN_UNROLL, local_group, 0)
        lax.fori_loop(0, n_chunks, state_step, state0, unroll=HGRN_UNROLL)
        lax.fori_loop(0, n_chunks, out_step, 0, unroll=2 * HGRN_UNROLL)


def _hgrn_scratch(seq):
    n_chunks = seq // CHUNK
    return [
        pltpu.VMEM((HGRN_SLOTS, CHUNK, HGRN_DIM), F32),
        pltpu.VMEM((seq, HGRN_DIM), BF16),
        pltpu.VMEM((n_chunks, CHUNK, CHUNK), BF16),
        pltpu.VMEM((n_chunks, HGRN_DIM, HGRN_DIM), F32),
        pltpu.VMEM((n_chunks, SUBLANES, HGRN_DIM), F32),
        pltpu.VMEM((n_chunks, HGRN_DIM, HGRN_DIM), BF16),
    ]


def _hgrn(hg_seg, lb_logits, hgrn_norm, yh_prev, batch0, tokens, seq):
    width = HGRN_HEADS_PER_STEP * HGRN_DIM
    seg_batch = hg_seg.shape[0] // seq
    blk = lambda off: pl.BlockSpec((seq, width), lambda b, h: (b, (off - HG_COL0) // width + h))
    carried = [] if yh_prev is None else [yh_prev]
    return pl.pallas_call(
        functools.partial(_hgrn_kernel, len(carried)),
        grid=(seg_batch, HGRN_HEADS // HGRN_HEADS_PER_STEP),
        in_specs=[
            pl.BlockSpec((lb_logits.shape[0], width), lambda b, h: (0, h)),
            pl.BlockSpec((HGRN_HEADS, HGRN_DIM), lambda b, h: (0, 0)),
            blk(OFF_HQ), blk(OFF_HF), blk(OFF_HI), blk(OFF_HG),
        ] + [pl.BlockSpec(memory_space=pl.ANY) for _ in carried],
        out_specs=pl.BlockSpec((seq, width), lambda b, h: (batch0 + b, h)),
        out_shape=jax.ShapeDtypeStruct((tokens, HGRN_WIDTH), BF16),
        input_output_aliases={6: 0} if carried else {},
        scratch_shapes=_hgrn_scratch(seq),
        compiler_params=pltpu.CompilerParams(dimension_semantics=("parallel", "parallel")),
        name="hgrn2",
    )(lb_logits, hgrn_norm, hg_seg, hg_seg, hg_seg, hg_seg, *carried)


def _inproj_hgrn_kernel(heads_per_tile, n_carried, x_ref, gain_ref, w_ref, lb_ref, hn_ref, hq_ref,
                        hf_ref, hi_ref, hg_ref, *rest):
    o_ref, hg_out_ref, yh_ref, h_ref, *hgrn_scratch = rest[n_carried:]
    col_step = pl.program_id(1)

    @pl.when(col_step == 0)
    def _():
        _prenorm(x_ref, gain_ref, h_ref)

    def piece(p):
        cs = slice(p * IN_PIECE, (p + 1) * IN_PIECE)
        tile = jnp.dot(h_ref[...], w_ref[:, cs], preferred_element_type=F32).astype(BF16)
        o_ref[:, cs] = tile
        hg_out_ref[:, cs] = tile

    @pl.when(col_step < heads_per_tile)
    def _():
        head = (pl.program_id(0) * heads_per_tile + col_step) % HGRN_HEADS
        _hgrn_head(lb_ref, hn_ref.at[pl.ds(head, 1), :], hq_ref, hf_ref, hi_ref, hg_ref, yh_ref,
                   *hgrn_scratch, piece_fn=piece, n_pieces=IN_PIECES)

    @pl.when(col_step >= heads_per_tile)
    def _():
        o_ref[...] = jnp.dot(h_ref[...], w_ref[...], preferred_element_type=F32).astype(BF16)
        _copy_hgrn_tile(o_ref, hg_out_ref)


def _inproj_hgrn(x2d, gain, w_bf16, row_tile0, n_row_tiles, hg_prev, batch0_prev, proj_prev, yh_prev,
                 lb_logits, hgrn_norm, seq):
    tokens = x2d.shape[0]
    n_heads = (hg_prev.shape[0] // seq) * HGRN_HEADS
    heads_per_tile = n_heads // n_row_tiles
    assert heads_per_tile * n_row_tiles == n_heads
    assert heads_per_tile <= HG_TILE0 + HG_TILES
    assert pl.cdiv(seq // CHUNK, IN_PIECES) <= HGRN_SLOTS

    def flat_head(i, j):
        return i * heads_per_tile + jnp.minimum(j, heads_per_tile - 1)

    def blk(off):
        return pl.BlockSpec((seq, HGRN_DIM), lambda i, j: (
            flat_head(i, j) // HGRN_HEADS, (off - HG_COL0) // HGRN_DIM + flat_head(i, j) % HGRN_HEADS))

    carried = [proj_prev] + ([] if yh_prev is None else [yh_prev])
    n_read = 9
    aliases = {n_read: 0} if yh_prev is None else {n_read: 0, n_read + 1: 2}
    return pl.pallas_call(
        functools.partial(_inproj_hgrn_kernel, heads_per_tile, len(carried)),
        grid=(n_row_tiles, IN_PROJ_WIDTH // IN_TN),
        in_specs=_inproj_specs(row_tile0) + [
            pl.BlockSpec((lb_logits.shape[0], HGRN_DIM),
                         lambda i, j: (0, flat_head(i, j) % HGRN_HEADS)),
            pl.BlockSpec((HGRN_HEADS, HGRN_DIM), lambda i, j: (0, 0)),
            blk(OFF_HQ), blk(OFF_HF), blk(OFF_HI), blk(OFF_HG),
        ] + [pl.BlockSpec(memory_space=pl.ANY) for _ in carried],
        out_specs=_inproj_out_specs(row_tile0) + [
            pl.BlockSpec((seq, HGRN_DIM), lambda i, j: (
                batch0_prev + flat_head(i, j) // HGRN_HEADS, flat_head(i, j) % HGRN_HEADS))],
        out_shape=_inproj_out_shapes(tokens, n_row_tiles) + [
            jax.ShapeDtypeStruct((tokens, HGRN_WIDTH), BF16)],
        input_output_aliases=aliases,
        scratch_shapes=[pltpu.VMEM((IN_TM, D_MODEL), BF16)] + _hgrn_scratch(seq),
        compiler_params=pltpu.CompilerParams(
            dimension_semantics=("arbitrary", "arbitrary"),
            vmem_limit_bytes=IN_HGRN_VMEM_BYTES),
        name="inproj_hgrn2",
    )(x2d, gain, w_bf16, lb_logits, hgrn_norm, hg_prev, hg_prev, hg_prev, hg_prev, *carried)


SM_BLOCKS = 2
SM_TM = SM_BLOCKS * WINDOW
OUT_TN = 512
OUT_NCHUNK = D_MODEL // OUT_TN
SM_VMEM_BYTES = 52 * MIB


def _t5_bucket(dist):
    max_exact = REL_BUCKETS // 2
    d = jnp.maximum(dist, 0)
    df = jnp.maximum(d, 1).astype(F32)
    large = max_exact + (jnp.log(df / max_exact) / math.log(REL_MAX_DIST / max_exact)
                         * (REL_BUCKETS - max_exact)).astype(jnp.int32)
    large = jnp.minimum(large, REL_BUCKETS - 1)
    return jnp.where(d < max_exact, d, large)


def _swap_halves(x):
    half = LANES // 2
    return jnp.concatenate([x[:, half:], x[:, :half]], axis=1)


def _swa_tables(relb_ref, bias_ref, cur_ref):
    qi = lax.broadcasted_iota(jnp.int32, (WINDOW, WINDOW), 0)
    sj = lax.broadcasted_iota(jnp.int32, (WINDOW, WINDOW), 1)
    own = sj <= qi
    bucket = _t5_bucket(jnp.where(own, qi - sj, qi + WINDOW - sj))
    for h in range(ATTN_HEADS):
        def pick(b, acc, h=h):
            return jnp.where(bucket == b, relb_ref[b, h], acc)
        tbl = lax.fori_loop(0, REL_BUCKETS, pick, jnp.zeros((WINDOW, WINDOW), F32))
        bias_ref[0, h] = tbl
        bias_ref[1, h] = jnp.where(own, tbl, MASK_VALUE)
    own_bf = jnp.where(own, 1.0, 0.0).astype(BF16)
    cur_ref[...] = jnp.concatenate([own_bf, own_bf], axis=0)


def _swa_merge_kernel(blocks_per_seq, n_tiles, relb_ref, sink_ref, q_ref, kc_ref, vc_ref, kp_ref,
                      vp_ref, ga_ref, gb_ref, yh_ref, *rest):
    gate_a_refs = rest[:OUT_NCHUNK]
    gate_h_refs = rest[OUT_NCHUNK:2 * OUT_NCHUNK]
    (x_ref, wa_ref, wh_ref, wo_ref, gain_ref, o_ref,
     bias_ref, cur_ref, ya_ref, m_ref) = rest[2 * OUT_NCHUNK:]
    step = pl.program_id(0)

    @pl.when(step == 0)
    def _():
        _swa_tables(relb_ref, bias_ref, cur_ref)
        ya_ref[...] = jnp.zeros_like(ya_ref)

    lane = lax.broadcasted_iota(jnp.int32, (1, LANES), 1)
    is_lo = lane < LANES // 2
    scale = ATTN_HEAD_DIM ** -0.5
    keep_lo = jnp.where(is_lo, 1.0, 0.0).astype(BF16)
    keep_hi = jnp.where(is_lo, 0.0, 1.0).astype(BF16)
    row = lax.broadcasted_iota(jnp.int32, (2 * WINDOW, 1), 0)
    top = row < WINDOW
    own2 = (lax.broadcasted_iota(jnp.int32, (2 * WINDOW, WINDOW), 1)
            <= (lax.broadcasted_iota(jnp.int32, (2 * WINDOW, WINDOW), 0) & (WINDOW - 1)))
    tile = jnp.minimum(step, n_tiles - 1)
    tile_starts_seq = jnp.where(tile % (blocks_per_seq // SM_BLOCKS) == 0, 1, 0)

    def scores(qm, km, first, heads):
        s2 = lax.dot_general(qm, km, _NT, preferred_element_type=F32)
        bias = jnp.concatenate([bias_ref[first, heads[0]], bias_ref[first, heads[1]]], axis=0)
        return jnp.where(own2, s2[:, WINDOW:], s2[:, :WINDOW]) + bias

    def probs(s, heads):
        sink = jnp.where(top, sink_ref[0, heads[0]], sink_ref[0, heads[1]])
        m = jnp.maximum(jnp.max(s, axis=-1, keepdims=True), sink)
        p = jnp.exp(s - m)
        denom = jnp.sum(p, axis=-1, keepdims=True) + jnp.exp(sink - m)
        pb = p.astype(BF16)
        p_own = pb * cur_ref[...]
        return jnp.concatenate([pb - p_own, p_own], axis=1), 1.0 / denom

    def heads_of(hk):
        e = hk % 2
        return (4 * hk + e, 4 * hk + 2 + e), (4 * hk + 1 - e, 4 * hk + 3 - e)

    ya_prev = ya_ref[...]
    yh = yh_ref[...]
    for j in range(OUT_NCHUNK):
        cs = slice(j * OUT_TN, (j + 1) * OUT_TN)
        ua = jnp.dot(ya_prev, wa_ref[:, cs], preferred_element_type=F32)
        uh = jnp.dot(yh, wh_ref[:, cs], preferred_element_type=F32)
        merged = (_sigmoid(gate_a_refs[j][...].astype(F32)) * ua
                  + _sigmoid(gate_h_refs[j][...].astype(F32)) * uh)
        m_ref[:, cs] = merged.astype(BF16)

    s_same, s_diff, vals = [], [], []
    for blk in range(SM_BLOCKS):
        rows = slice(blk * WINDOW, (blk + 1) * WINDOW)
        first = tile_starts_seq if blk == 0 else 0
        for hk in range(ATTN_KV_HEADS):
            c, e = divmod(hk, 2)
            cs = slice(c * LANES, (c + 1) * LANES)
            if blk == 0:
                k_prev, v_prev = kp_ref[:, cs], vp_ref[:, cs]
            else:
                prev_rows = slice((blk - 1) * WINDOW, blk * WINDOW)
                k_prev, v_prev = kc_ref[prev_rows, cs], vc_ref[prev_rows, cs]
            k2 = jnp.concatenate([k_prev, kc_ref[rows, cs]], axis=0)
            v2 = jnp.concatenate([v_prev, vc_ref[rows, cs]], axis=0)
            k2s, v2s = _swap_halves(k2), _swap_halves(v2)
            keep_e, keep_o = (keep_lo, keep_hi) if e == 0 else (keep_hi, keep_lo)
            p0 = q_ref[rows, hk * 2 * LANES:hk * 2 * LANES + LANES]
            p1 = q_ref[rows, hk * 2 * LANES + LANES:(hk + 1) * 2 * LANES]
            q_same = jnp.concatenate([p0 * (keep_e * scale), p1 * (keep_e * scale)], axis=0)
            q_diff = jnp.concatenate([p0 * (keep_o * scale), p1 * (keep_o * scale)], axis=0)
            h_same, h_diff = heads_of(hk)
            s_same.append(scores(q_same, k2, first, h_same))
            s_diff.append(scores(q_diff, k2s, first, h_diff))
            vals.append((v2 * keep_lo, v2s * keep_hi) if e == 0 else (v2s * keep_lo, v2 * keep_hi))

    y = jnp.dot(m_ref[...], wo_ref[...], preferred_element_type=F32)

    pr_same = [probs(s_same[i], heads_of(i % ATTN_KV_HEADS)[0]) for i in range(len(s_same))]
    pr_diff = [probs(s_diff[i], heads_of(i % ATTN_KV_HEADS)[1]) for i in range(len(s_diff))]
    for blk in range(SM_BLOCKS):
        rows = slice(blk * WINDOW, (blk + 1) * WINDOW)
        for hk in range(ATTN_KV_HEADS):
            i = blk * ATTN_KV_HEADS + hk
            (p_lo, r_lo), (p_hi, r_hi) = ((pr_same[i], pr_diff[i]) if hk % 2 == 0
                                          else (pr_diff[i], pr_same[i]))
            v_lo, v_hi = vals[i]
            pv = jnp.dot(jnp.concatenate([p_lo, p_hi], axis=1),
                         jnp.concatenate([v_lo, v_hi], axis=0),
                         preferred_element_type=F32)
            pv = pv * jnp.where(is_lo, r_lo, r_hi)
            g_ref = ga_ref if hk < 2 else gb_ref
            for pair in range(2):
                gs = slice((hk % 2) * 2 * LANES + pair * LANES,
                           (hk % 2) * 2 * LANES + (pair + 1) * LANES)
                gate = _silu(g_ref[rows, gs].astype(F32))
                ya_ref[rows, (2 * hk + pair) * LANES:(2 * hk + pair + 1) * LANES] = (
                    pv[pair * WINDOW:(pair + 1) * WINDOW] * gate).astype(BF16)

    ms = jnp.mean(y * y, axis=-1, keepdims=True)
    o_ref[...] = x_ref[...] + y * lax.rsqrt(ms + NORM_EPS) * gain_ref[...]


def _swa_merge(proj, yh, x2d, rel_bias, sinks, wa, wh, wo, gain, seq):
    tokens = x2d.shape[0]
    n_tiles = tokens // SM_TM
    blocks_per_seq = seq // WINDOW
    attn_tile = lambda t: jnp.minimum(t, n_tiles - 1)
    merge_tile = lambda t: jnp.maximum(t - 1, 0)
    prev_block = lambda t: jnp.maximum(attn_tile(t) * SM_BLOCKS - 1, 0)
    kcol, vcol = OFF_AK // KV_WIDTH, OFF_AV // KV_WIDTH
    gcol = OFF_AG // (ATTN_WIDTH // 2)
    const = lambda t: (0, 0)
    smem = pl.BlockSpec(memory_space=pltpu.SMEM)
    gate = lambda off, j: pl.BlockSpec((SM_TM, OUT_TN), lambda t: (merge_tile(t), off // OUT_TN + j))
    resident = functools.partial(pl.BlockSpec, index_map=const, pipeline_mode=pl.Buffered(1))
    return pl.pallas_call(
        functools.partial(_swa_merge_kernel, blocks_per_seq, n_tiles),
        grid=(n_tiles + 1,),
        in_specs=(
            [smem, smem,
             pl.BlockSpec((SM_TM, ATTN_WIDTH), lambda t: (attn_tile(t), 0)),
             pl.BlockSpec((SM_TM, KV_WIDTH), lambda t: (attn_tile(t), kcol)),
             pl.BlockSpec((SM_TM, KV_WIDTH), lambda t: (attn_tile(t), vcol)),
             pl.BlockSpec((WINDOW, KV_WIDTH), lambda t: (prev_block(t), kcol)),
             pl.BlockSpec((WINDOW, KV_WIDTH), lambda t: (prev_block(t), vcol)),
             pl.BlockSpec((SM_TM, ATTN_WIDTH // 2), lambda t: (attn_tile(t), gcol)),
             pl.BlockSpec((SM_TM, ATTN_WIDTH // 2), lambda t: (attn_tile(t), gcol + 1)),
             pl.BlockSpec((SM_TM, HGRN_WIDTH), lambda t: (merge_tile(t), 0))]
            + [gate(OFF_GA, j) for j in range(OUT_NCHUNK)]
            + [gate(OFF_GH, j) for j in range(OUT_NCHUNK)]
            + [pl.BlockSpec((SM_TM, D_MODEL), lambda t: (merge_tile(t), 0)),
               resident((ATTN_WIDTH, D_MODEL)), resident((HGRN_WIDTH, D_MODEL)),
               resident((D_MODEL, D_MODEL)), pl.BlockSpec((1, D_MODEL), const)]),
        out_specs=pl.BlockSpec((SM_TM, D_MODEL), lambda t: (merge_tile(t), 0)),
        out_shape=jax.ShapeDtypeStruct((tokens, D_MODEL), F32),
        scratch_shapes=[
            pltpu.VMEM((2, ATTN_HEADS, WINDOW, WINDOW), F32),
            pltpu.VMEM((2 * WINDOW, WINDOW), BF16),
            pltpu.VMEM((SM_TM, ATTN_WIDTH), BF16),
            pltpu.VMEM((SM_TM, D_MODEL), BF16),
        ],
        compiler_params=pltpu.CompilerParams(
            dimension_semantics=("arbitrary",), vmem_limit_bytes=SM_VMEM_BYTES),
        name="swa_merge",
    )(rel_bias, sinks, proj, proj, proj, proj, proj, proj, proj, yh,
      *([proj] * (2 * OUT_NCHUNK)), x2d, wa, wh, wo, gain)


def kernel(x, norm_pre, w_in, rel_bias, attn_sinks, lb_logits, hgrn_norm, w_branch_attn,
           w_branch_hgrn, w_out, norm_post):
    batch, seq, d_model = x.shape
    depth = w_in.shape[0]
    assert depth == 1 and d_model == D_MODEL and seq % IN_TM == 0
    assert lb_logits.shape == (depth + 1, HGRN_WIDTH)
    assert sum(SEG_BATCHES) == batch
    x2d = x.reshape(batch * seq, d_model)
    layer = 0
    gain_pre = norm_pre[layer][None, :]
    w_bf16 = w_in[layer].astype(BF16)
    tiles_per_seq = seq // IN_TM
    proj, hg = _inproj(x2d, gain_pre, w_bf16, 0, SEG_BATCHES[0] * tiles_per_seq)
    yh = None
    batch0 = 0
    for prev_batches, seg_batches in zip(SEG_BATCHES[:-1], SEG_BATCHES[1:]):
        proj, hg, yh = _inproj_hgrn(
            x2d, gain_pre, w_bf16, (batch0 + prev_batches) * tiles_per_seq,
            seg_batches * tiles_per_seq, hg, batch0, proj, yh, lb_logits, hgrn_norm[layer], seq)
        batch0 += prev_batches
    yh = _hgrn(hg, lb_logits, hgrn_norm[layer], yh, batch0, batch * seq, seq)
    out = _swa_merge(proj, yh, x2d, rel_bias, attn_sinks[layer][None, :],
                     w_branch_attn[layer].astype(BF16), w_branch_hgrn[layer].astype(BF16),
                     w_out[layer].astype(BF16), norm_post[layer][None, :], seq)
    return out.reshape(batch, seq, d_model)
```

```python
import functools
import math

import jax
import jax.numpy as jnp
from jax import lax
from jax.experimental import pallas as pl
from jax.experimental.pallas import tpu as pltpu

D_MODEL = 2048
ATTN_HEADS = 16
ATTN_KV_HEADS = 4
ATTN_HEAD_DIM = 64
WINDOW = 128
ATTN_WIDTH = ATTN_HEADS * ATTN_HEAD_DIM
KV_WIDTH = ATTN_KV_HEADS * ATTN_HEAD_DIM
HGRN_HEADS = 8
HGRN_DIM = 128
HGRN_WIDTH = HGRN_HEADS * HGRN_DIM
CHUNK = 64
CHUNK_LEVELS = 6
HGRN_UNROLL = 4
HGRN_SLOTS = 6
HGRN_HEADS_PER_STEP = 2
SEG_BATCHES = (2,) * 8
REL_BUCKETS = 32
REL_MAX_DIST = 128
NORM_EPS = 1e-6
IN_PROJ_WIDTH = 10752

OFF_AQ = 0
OFF_AK = OFF_AQ + ATTN_WIDTH
OFF_AV = OFF_AK + KV_WIDTH
OFF_AG = OFF_AV + KV_WIDTH
OFF_HQ = OFF_AG + ATTN_WIDTH
OFF_HF = OFF_HQ + HGRN_WIDTH
OFF_HI = OFF_HF + HGRN_WIDTH
OFF_HG = OFF_HI + HGRN_WIDTH
OFF_GA = OFF_HG + HGRN_WIDTH
OFF_GH = OFF_GA + D_MODEL

LANES = 128
SUBLANES = 8
MASK_VALUE = -1e30

F32 = jnp.float32
BF16 = jnp.bfloat16

_NT = (((1,), (1,)), ((), ()))
_TN = (((0,), (0,)), ((), ()))


def _sigmoid(x):
    return 1.0 / (1.0 + jnp.exp(-x))


def _silu(x):
    return x * _sigmoid(x)


IN_TM = 1024
IN_TN = 1536
IN_NORM_ROWS = 256
IN_PIECE = 256
IN_PIECES = IN_TN // IN_PIECE
MIB = 1024 * 1024
IN_VMEM_BYTES = 56 * MIB
IN_HGRN_VMEM_BYTES = 58 * MIB


def _prenorm(x_ref, gain_ref, h_ref):
    gain = gain_ref[...]
    for r in range(0, IN_TM, IN_NORM_ROWS):
        xf = x_ref[r:r + IN_NORM_ROWS, :]
        ms = jnp.mean(xf * xf, axis=-1, keepdims=True)
        h_ref[r:r + IN_NORM_ROWS, :] = (xf * lax.rsqrt(ms + NORM_EPS) * gain).astype(BF16)


HG_TILE0 = OFF_HQ // IN_TN
HG_TILES = (OFF_GA - 1) // IN_TN - HG_TILE0 + 1
HG_COL0 = HG_TILE0 * IN_TN


def _copy_hgrn_tile(o_ref, hg_ref):
    col_step = pl.program_id(1)

    @pl.when((col_step >= HG_TILE0) & (col_step < HG_TILE0 + HG_TILES))
    def _():
        hg_ref[...] = o_ref[...]


def _inproj_kernel(x_ref, gain_ref, w_ref, o_ref, hg_ref, h_ref):
    @pl.when(pl.program_id(1) == 0)
    def _():
        _prenorm(x_ref, gain_ref, h_ref)

    o_ref[...] = jnp.dot(h_ref[...], w_ref[...], preferred_element_type=F32).astype(BF16)
    _copy_hgrn_tile(o_ref, hg_ref)


def _inproj_specs(row_tile0):
    return [
        pl.BlockSpec((IN_TM, D_MODEL), lambda i, j: (row_tile0 + i, 0)),
        pl.BlockSpec((1, D_MODEL), lambda i, j: (0, 0)),
        pl.BlockSpec((D_MODEL, IN_TN), lambda i, j: (0, j)),
    ]


def _inproj_out_specs(row_tile0):
    hg_tile = lambda j: jnp.clip(j - HG_TILE0, 0, HG_TILES - 1)
    return [pl.BlockSpec((IN_TM, IN_TN), lambda i, j: (row_tile0 + i, j)),
            pl.BlockSpec((IN_TM, IN_TN), lambda i, j: (i, hg_tile(j)))]


def _inproj_out_shapes(tokens, n_row_tiles):
    return [jax.ShapeDtypeStruct((tokens, IN_PROJ_WIDTH), BF16),
            jax.ShapeDtypeStruct((n_row_tiles * IN_TM, HG_TILES * IN_TN), BF16)]


def _inproj(x2d, gain, w_bf16, row_tile0, n_row_tiles):
    return pl.pallas_call(
        _inproj_kernel,
        grid=(n_row_tiles, IN_PROJ_WIDTH // IN_TN),
        in_specs=_inproj_specs(row_tile0),
        out_specs=_inproj_out_specs(row_tile0),
        out_shape=_inproj_out_shapes(x2d.shape[0], n_row_tiles),
        scratch_shapes=[pltpu.VMEM((IN_TM, D_MODEL), BF16)],
        compiler_params=pltpu.CompilerParams(
            dimension_semantics=("parallel", "arbitrary"),
            vmem_limit_bytes=IN_VMEM_BYTES),
        name="inproj",
    )(x2d, gain, w_bf16)


def _level_reference(g_ref, level, sub):
    half = 1 << level
    blk = 2 * half

    def row_tile(r):
        return g_ref[pl.ds(r, SUBLANES, stride=0), :]

    if blk >= SUBLANES:
        return jnp.concatenate(
            [row_tile(b * blk + half) for b in range(CHUNK // blk) for _ in range(blk // SUBLANES)],
            axis=0)
    tiles = []
    for t in range(CHUNK // SUBLANES):
        acc = row_tile(t * SUBLANES + half)
        for b in range(1, SUBLANES // blk):
            acc = jnp.where(sub >= b * blk, row_tile(t * SUBLANES + b * blk + half), acc)
        tiles.append(acc)
    return jnp.concatenate(tiles, axis=0)


def _neg_abs(x):
    bits = lax.bitcast_convert_type(x, jnp.uint32) | jnp.uint32(0x80000000)
    return lax.bitcast_convert_type(bits, F32)


def _hgrn_kernel(n_carried, lb_ref, hn_ref, hq_ref, hf_ref, hi_ref, hg_ref, *rest):
    o_ref, *scratch = rest[n_carried:]
    for j in range(HGRN_HEADS_PER_STEP):
        cols = slice(j * HGRN_DIM, (j + 1) * HGRN_DIM)
        head = pl.program_id(1) * HGRN_HEADS_PER_STEP + j
        _hgrn_head(lb_ref.at[:, cols], hn_ref.at[pl.ds(head, 1), :], hq_ref.at[:, cols],
                   hf_ref.at[:, cols], hi_ref.at[:, cols], hg_ref.at[:, cols], o_ref.at[:, cols],
                   *scratch)


def _hgrn_head(lb_ref, hn_ref, hq_ref, hf_ref, hi_ref, hg_ref, o_ref,
               g_ref, qg_ref, attn_ref, upd_ref, keep_ref, state_ref, piece_fn=None, n_pieces=0):
    seq = hq_ref.shape[0]
    lg = lb_ref[...]
    ex = jnp.exp(lg - jnp.max(lg, axis=0, keepdims=True))
    lb = ex[0:1, :] / jnp.sum(ex, axis=0, keepdims=True)
    hn = hn_ref[...]

    rowi = lax.broadcasted_iota(jnp.int32, (CHUNK, HGRN_DIM), 0)
    sub = lax.broadcasted_iota(jnp.int32, (SUBLANES, HGRN_DIM), 0)
    ti = lax.broadcasted_iota(jnp.int32, (CHUNK, CHUNK), 0)
    si = lax.broadcasted_iota(jnp.int32, (CHUNK, CHUNK), 1)
    xor = ti ^ si
    lvl = jnp.full((CHUNK, CHUNK), -2, jnp.int32)
    for level in range(CHUNK_LEVELS):
        lvl = jnp.where(xor >= (1 << level), level, lvl)
    lvl = jnp.where(ti > si, lvl, jnp.where(ti == si, -1, -2))

    def chunk_rows(c):
        start = c * CHUNK
        return pl.ds(start if isinstance(c, int) else pl.multiple_of(start, CHUNK), CHUNK)

    def local_step(c, slot):
        rows = chunk_rows(c)
        q = _silu(hq_ref[rows, :].astype(F32))
        f = lb + (1.0 - lb) * _sigmoid(hf_ref[rows, :].astype(F32))
        k = 1.0 - f
        v = hi_ref[rows, :]
        g = jnp.log2(f)
        for shift in (1, 2, 4):
            g = g + jnp.where(rowi >= shift, pltpu.roll(g, shift, 0), 0.0)
        for shift in (8, 16, 32):
            g = g + jnp.concatenate(
                [jnp.zeros((shift, HGRN_DIM), F32), g[:CHUNK - shift]], axis=0)
        g_slot = g_ref.at[slot]
        g_slot[...] = g
        g_last = g_slot[pl.ds(CHUNK - 1, SUBLANES, stride=0), :][0:1]
        qb = q.astype(BF16)
        kb = k.astype(BF16)
        qg_ref[rows, :] = qb * jnp.exp2(g).astype(BF16)

        attn = jnp.where(lvl == -1, jnp.sum(q * k, axis=-1, keepdims=True), 0.0)
        for level in range(CHUNK_LEVELS):
            dec = jnp.exp2(_neg_abs(g - _level_reference(g_slot, level, sub))).astype(BF16)
            a = lax.dot_general(qb * dec, kb * dec, _NT, preferred_element_type=F32)
            attn = jnp.where(lvl == level, a, attn)
        attn_ref[c] = attn.astype(BF16)

        k_dec = kb * jnp.exp2(g_last - g).astype(BF16)
        upd_ref[c] = lax.dot_general(v, k_dec, _TN, preferred_element_type=F32)
        keep_ref[c] = jnp.broadcast_to(jnp.exp2(g_last), (SUBLANES, HGRN_DIM))

    def local_group(i, carry):
        for slot in range(HGRN_UNROLL):
            local_step(i * HGRN_UNROLL + slot, slot)
        return carry

    def state_step(c, state_t):
        state_ref[c] = state_t.astype(BF16)
        return keep_ref[c][0:1] * state_t + upd_ref[c]

    def out_step(c, carry):
        rows = chunk_rows(c)
        o = lax.dot_general(qg_ref[rows, :], state_ref[c], _NT, preferred_element_type=F32)
        o = o + jnp.dot(attn_ref[c], hi_ref[rows, :], preferred_element_type=F32)
        ms = jnp.mean(o * o, axis=-1, keepdims=True)
        y = o * lax.rsqrt(ms + NORM_EPS) * hn * _silu(hg_ref[rows, :].astype(F32))
        o_ref[rows, :] = y.astype(BF16)
        return carry

    n_chunks = seq // CHUNK
    state0 = jnp.zeros((HGRN_DIM, HGRN_DIM), F32)
    if n_pieces:
        bounds = [(p * n_chunks) // n_pieces for p in range(n_pieces + 1)]
        state_t = state0
        for p in range(n_pieces):
            piece_fn(p)
            for c in range(bounds[p], bounds[p + 1]):
                local_step(c, c - bounds[p])
                if c >= 1:
                    state_t = state_step(c - 1, state_t)
                if c >= 2:
                    out_step(c - 2, 0)
        state_step(n_chunks - 1, state_t)
        out_step(n_chunks - 2, 0)
        out_step(n_chunks - 1, 0)
    else:
        def finish_group(i, state_t):
            for slot in range(HGRN_UNROLL):
                c = i * HGRN_UNROLL + slot
                state_t = state_step(c, state_t)
                out_step(c, 0)
            return state_t

        def lagged_group(i, state_t):
            state_t = finish_group(i - 1, state_t)
            local_group(i, 0)
            return state_t

        n_groups = n_chunks // HGRN_UNROLL
        local_group(0, 0)
        state_t = lax.fori_loop(1, n_groups, lagged_group, state0)
        finish_group(n_groups - 1, state_t)


def _hgrn_scratch(seq):
    n_chunks = seq // CHUNK
    return [
        pltpu.VMEM((HGRN_SLOTS, CHUNK, HGRN_DIM), F32),
        pltpu.VMEM((seq, HGRN_DIM), BF16),
        pltpu.VMEM((n_chunks, CHUNK, CHUNK), BF16),
        pltpu.VMEM((n_chunks, HGRN_DIM, HGRN_DIM), F32),
        pltpu.VMEM((n_chunks, SUBLANES, HGRN_DIM), F32),
        pltpu.VMEM((n_chunks, HGRN_DIM, HGRN_DIM), BF16),
    ]


def _hgrn(hg_seg, lb_logits, hgrn_norm, yh_prev, batch0, tokens, seq):
    width = HGRN_HEADS_PER_STEP * HGRN_DIM
    seg_batch = hg_seg.shape[0] // seq
    blk = lambda off: pl.BlockSpec((seq, width), lambda b, h: (b, (off - HG_COL0) // width + h))
    carried = [] if yh_prev is None else [yh_prev]
    return pl.pallas_call(
        functools.partial(_hgrn_kernel, len(carried)),
        grid=(seg_batch, HGRN_HEADS // HGRN_HEADS_PER_STEP),
        in_specs=[
            pl.BlockSpec((lb_logits.shape[0], width), lambda b, h: (0, h)),
            pl.BlockSpec((HGRN_HEADS, HGRN_DIM), lambda b, h: (0, 0)),
            blk(OFF_HQ), blk(OFF_HF), blk(OFF_HI), blk(OFF_HG),
        ] + [pl.BlockSpec(memory_space=pl.ANY) for _ in carried],
        out_specs=pl.BlockSpec((seq, width), lambda b, h: (batch0 + b, h)),
        out_shape=jax.ShapeDtypeStruct((tokens, HGRN_WIDTH), BF16),
        input_output_aliases={6: 0} if carried else {},
        scratch_shapes=_hgrn_scratch(seq),
        compiler_params=pltpu.CompilerParams(dimension_semantics=("parallel", "parallel")),
        name="hgrn2",
    )(lb_logits, hgrn_norm, hg_seg, hg_seg, hg_seg, hg_seg, *carried)


def _inproj_hgrn_kernel(heads_per_tile, n_carried, x_ref, gain_ref, w_ref, lb_ref, hn_ref, hq_ref,
                        hf_ref, hi_ref, hg_ref, *rest):
    o_ref, hg_out_ref, yh_ref, h_ref, *hgrn_scratch = rest[n_carried:]
    col_step = pl.program_id(1)

    @pl.when(col_step == 0)
    def _():
        _prenorm(x_ref, gain_ref, h_ref)

    def piece(p):
        cs = slice(p * IN_PIECE, (p + 1) * IN_PIECE)
        tile = jnp.dot(h_ref[...], w_ref[:, cs], preferred_element_type=F32).astype(BF16)
        o_ref[:, cs] = tile
        hg_out_ref[:, cs] = tile

    @pl.when(col_step < heads_per_tile)
    def _():
        head = (pl.program_id(0) * heads_per_tile + col_step) % HGRN_HEADS
        _hgrn_head(lb_ref, hn_ref.at[pl.ds(head, 1), :], hq_ref, hf_ref, hi_ref, hg_ref, yh_ref,
                   *hgrn_scratch, piece_fn=piece, n_pieces=IN_PIECES)

    @pl.when(col_step >= heads_per_tile)
    def _():
        o_ref[...] = jnp.dot(h_ref[...], w_ref[...], preferred_element_type=F32).astype(BF16)
        _copy_hgrn_tile(o_ref, hg_out_ref)


def _inproj_hgrn(x2d, gain, w_bf16, row_tile0, n_row_tiles, hg_prev, batch0_prev, proj_prev, yh_prev,
                 lb_logits, hgrn_norm, seq):
    tokens = x2d.shape[0]
    n_heads = (hg_prev.shape[0] // seq) * HGRN_HEADS
    heads_per_tile = n_heads // n_row_tiles
    assert heads_per_tile * n_row_tiles == n_heads
    assert heads_per_tile <= HG_TILE0 + HG_TILES
    assert pl.cdiv(seq // CHUNK, IN_PIECES) <= HGRN_SLOTS

    def flat_head(i, j):
        return i * heads_per_tile + jnp.minimum(j, heads_per_tile - 1)

    def blk(off):
        return pl.BlockSpec((seq, HGRN_DIM), lambda i, j: (
            flat_head(i, j) // HGRN_HEADS, (off - HG_COL0) // HGRN_DIM + flat_head(i, j) % HGRN_HEADS))

    carried = [proj_prev] + ([] if yh_prev is None else [yh_prev])
    n_read = 9
    aliases = {n_read: 0} if yh_prev is None else {n_read: 0, n_read + 1: 2}
    return pl.pallas_call(
        functools.partial(_inproj_hgrn_kernel, heads_per_tile, len(carried)),
        grid=(n_row_tiles, IN_PROJ_WIDTH // IN_TN),
        in_specs=_inproj_specs(row_tile0) + [
            pl.BlockSpec((lb_logits.shape[0], HGRN_DIM),
                         lambda i, j: (0, flat_head(i, j) % HGRN_HEADS)),
            pl.BlockSpec((HGRN_HEADS, HGRN_DIM), lambda i, j: (0, 0)),
            blk(OFF_HQ), blk(OFF_HF), blk(OFF_HI), blk(OFF_HG),
        ] + [pl.BlockSpec(memory_space=pl.ANY) for _ in carried],
        out_specs=_inproj_out_specs(row_tile0) + [
            pl.BlockSpec((seq, HGRN_DIM), lambda i, j: (
                batch0_prev + flat_head(i, j) // HGRN_HEADS, flat_head(i, j) % HGRN_HEADS))],
        out_shape=_inproj_out_shapes(tokens, n_row_tiles) + [
            jax.ShapeDtypeStruct((tokens, HGRN_WIDTH), BF16)],
        input_output_aliases=aliases,
        scratch_shapes=[pltpu.VMEM((IN_TM, D_MODEL), BF16)] + _hgrn_scratch(seq),
        compiler_params=pltpu.CompilerParams(
            dimension_semantics=("arbitrary", "arbitrary"),
            vmem_limit_bytes=IN_HGRN_VMEM_BYTES),
        name="inproj_hgrn2",
    )(x2d, gain, w_bf16, lb_logits, hgrn_norm, hg_prev, hg_prev, hg_prev, hg_prev, *carried)


SM_BLOCKS = 2
SM_TM = SM_BLOCKS * WINDOW
OUT_TN = 512
OUT_NCHUNK = D_MODEL // OUT_TN
SM_VMEM_BYTES = 52 * MIB


def _t5_bucket(dist):
    max_exact = REL_BUCKETS // 2
    d = jnp.maximum(dist, 0)
    df = jnp.maximum(d, 1).astype(F32)
    large = max_exact + (jnp.log(df / max_exact) / math.log(REL_MAX_DIST / max_exact)
                         * (REL_BUCKETS - max_exact)).astype(jnp.int32)
    large = jnp.minimum(large, REL_BUCKETS - 1)
    return jnp.where(d < max_exact, d, large)


def _swap_halves(x):
    half = LANES // 2
    return jnp.concatenate([x[:, half:], x[:, :half]], axis=1)


def _swa_tables(relb_ref, bias_ref, cur_ref):
    qi = lax.broadcasted_iota(jnp.int32, (WINDOW, WINDOW), 0)
    sj = lax.broadcasted_iota(jnp.int32, (WINDOW, WINDOW), 1)
    own = sj <= qi
    bucket = _t5_bucket(jnp.where(own, qi - sj, qi + WINDOW - sj))
    for h in range(ATTN_HEADS):
        def pick(b, acc, h=h):
            return jnp.where(bucket == b, relb_ref[b, h], acc)
        tbl = lax.fori_loop(0, REL_BUCKETS, pick, jnp.zeros((WINDOW, WINDOW), F32))
        bias_ref[0, h] = tbl
        bias_ref[1, h] = jnp.where(own, tbl, MASK_VALUE)
    own_bf = jnp.where(own, 1.0, 0.0).astype(BF16)
    cur_ref[...] = jnp.concatenate([own_bf, own_bf], axis=0)


def _swa_merge_kernel(blocks_per_seq, n_tiles, relb_ref, sink_ref, q_ref, kc_ref, vc_ref, kp_ref,
                      vp_ref, ga_ref, gb_ref, yh_ref, *rest):
    gate_a_refs = rest[:OUT_NCHUNK]
    gate_h_refs = rest[OUT_NCHUNK:2 * OUT_NCHUNK]
    (x_ref, wa_ref, wh_ref, wo_ref, gain_ref, o_ref,
     bias_ref, cur_ref, ya_ref, m_ref) = rest[2 * OUT_NCHUNK:]
    step = pl.program_id(0)

    @pl.when(step == 0)
    def _():
        _swa_tables(relb_ref, bias_ref, cur_ref)
        ya_ref[...] = jnp.zeros_like(ya_ref)

    lane = lax.broadcasted_iota(jnp.int32, (1, LANES), 1)
    is_lo = lane < LANES // 2
    scale = ATTN_HEAD_DIM ** -0.5
    keep_lo = jnp.where(is_lo, 1.0, 0.0).astype(BF16)
    keep_hi = jnp.where(is_lo, 0.0, 1.0).astype(BF16)
    row = lax.broadcasted_iota(jnp.int32, (2 * WINDOW, 1), 0)
    top = row < WINDOW
    own2 = (lax.broadcasted_iota(jnp.int32, (2 * WINDOW, WINDOW), 1)
            <= (lax.broadcasted_iota(jnp.int32, (2 * WINDOW, WINDOW), 0) & (WINDOW - 1)))
    tile = jnp.minimum(step, n_tiles - 1)
    tile_starts_seq = jnp.where(tile % (blocks_per_seq // SM_BLOCKS) == 0, 1, 0)

    def scores(qm, km, first, heads):
        s2 = lax.dot_general(qm, km, _NT, preferred_element_type=F32)
        bias = jnp.concatenate([bias_ref[first, heads[0]], bias_ref[first, heads[1]]], axis=0)
        return jnp.where(own2, s2[:, WINDOW:], s2[:, :WINDOW]) + bias

    def probs(s, heads):
        sink = jnp.where(top, sink_ref[0, heads[0]], sink_ref[0, heads[1]])
        m = jnp.maximum(jnp.max(s, axis=-1, keepdims=True), sink)
        p = jnp.exp(s - m)
        denom = jnp.sum(p, axis=-1, keepdims=True) + jnp.exp(sink - m)
        pb = p.astype(BF16)
        p_own = pb * cur_ref[...]
        return jnp.concatenate([pb - p_own, p_own], axis=1), 1.0 / denom

    def heads_of(hk):
        e = hk % 2
        return (4 * hk + e, 4 * hk + 2 + e), (4 * hk + 1 - e, 4 * hk + 3 - e)

    ya_prev = ya_ref[...]
    yh = yh_ref[...]
    for j in range(OUT_NCHUNK):
        cs = slice(j * OUT_TN, (j + 1) * OUT_TN)
        ua = jnp.dot(ya_prev, wa_ref[:, cs], preferred_element_type=F32)
        uh = jnp.dot(yh, wh_ref[:, cs], preferred_element_type=F32)
        merged = (_sigmoid(gate_a_refs[j][...].astype(F32)) * ua
                  + _sigmoid(gate_h_refs[j][...].astype(F32)) * uh)
        m_ref[:, cs] = merged.astype(BF16)

    s_same, s_diff, vals = [], [], []
    for blk in range(SM_BLOCKS):
        rows = slice(blk * WINDOW, (blk + 1) * WINDOW)
        first = tile_starts_seq if blk == 0 else 0
        for hk in range(ATTN_KV_HEADS):
            c, e = divmod(hk, 2)
            cs = slice(c * LANES, (c + 1) * LANES)
            if blk == 0:
                k_prev, v_prev = kp_ref[:, cs], vp_ref[:, cs]
            else:
                prev_rows = slice((blk - 1) * WINDOW, blk * WINDOW)
                k_prev, v_prev = kc_ref[prev_rows, cs], vc_ref[prev_rows, cs]
            k2 = jnp.concatenate([k_prev, kc_ref[rows, cs]], axis=0)
            v2 = jnp.concatenate([v_prev, vc_ref[rows, cs]], axis=0)
            k2s, v2s = _swap_halves(k2), _swap_halves(v2)
            keep_e, keep_o = (keep_lo, keep_hi) if e == 0 else (keep_hi, keep_lo)
            p0 = q_ref[rows, hk * 2 * LANES:hk * 2 * LANES + LANES]
            p1 = q_ref[rows, hk * 2 * LANES + LANES:(hk + 1) * 2 * LANES]
            q_same = jnp.concatenate([p0 * (keep_e * scale), p1 * (keep_e * scale)], axis=0)
            q_diff = jnp.concatenate([p0 * (keep_o * scale), p1 * (keep_o * scale)], axis=0)
            h_same, h_diff = heads_of(hk)
            s_same.append(scores(q_same, k2, first, h_same))
            s_diff.append(scores(q_diff, k2s, first, h_diff))
            vals.append((v2 * keep_lo, v2s * keep_hi) if e == 0 else (v2s * keep_lo, v2 * keep_hi))

    y = jnp.dot(m_ref[...], wo_ref[...], preferred_element_type=F32)

    pr_same = [probs(s_same[i], heads_of(i % ATTN_KV_HEADS)[0]) for i in range(len(s_same))]
    pr_diff = [probs(s_diff[i], heads_of(i % ATTN_KV_HEADS)[1]) for i in range(len(s_diff))]
    for blk in range(SM_BLOCKS):
        rows = slice(blk * WINDOW, (blk + 1) * WINDOW)
        for hk in range(ATTN_KV_HEADS):
            i = blk * ATTN_KV_HEADS + hk
            (p_lo, r_lo), (p_hi, r_hi) = ((pr_same[i], pr_diff[i]) if hk % 2 == 0
                                          else (pr_diff[i], pr_same[i]))
            v_lo, v_hi = vals[i]
            pv = jnp.dot(jnp.concatenate([p_lo, p_hi], axis=1),
                         jnp.concatenate([v_lo, v_hi], axis=0),
                         preferred_element_type=F32)
            pv = pv * jnp.where(is_lo, r_lo, r_hi)
            g_ref = ga_ref if hk < 2 else gb_ref
            for pair in range(2):
                gs = slice((hk % 2) * 2 * LANES + pair * LANES,
                           (hk % 2) * 2 * LANES + (pair + 1) * LANES)
                gate = _silu(g_ref[rows, gs].astype(F32))
                ya_ref[rows, (2 * hk + pair) * LANES:(2 * hk + pair + 1) * LANES] = (
                    pv[pair * WINDOW:(pair + 1) * WINDOW] * gate).astype(BF16)

    ms = jnp.mean(y * y, axis=-1, keepdims=True)
    o_ref[...] = x_ref[...] + y * lax.rsqrt(ms + NORM_EPS) * gain_ref[...]


def _swa_merge(proj, yh, x2d, rel_bias, sinks, wa, wh, wo, gain, seq):
    tokens = x2d.shape[0]
    n_tiles = tokens // SM_TM
    blocks_per_seq = seq // WINDOW
    attn_tile = lambda t: jnp.minimum(t, n_tiles - 1)
    merge_tile = lambda t: jnp.maximum(t - 1, 0)
    prev_block = lambda t: jnp.maximum(attn_tile(t) * SM_BLOCKS - 1, 0)
    kcol, vcol = OFF_AK // KV_WIDTH, OFF_AV // KV_WIDTH
    gcol = OFF_AG // (ATTN_WIDTH // 2)
    const = lambda t: (0, 0)
    smem = pl.BlockSpec(memory_space=pltpu.SMEM)
    gate = lambda off, j: pl.BlockSpec((SM_TM, OUT_TN), lambda t: (merge_tile(t), off // OUT_TN + j))
    resident = functools.partial(pl.BlockSpec, index_map=const, pipeline_mode=pl.Buffered(1))
    return pl.pallas_call(
        functools.partial(_swa_merge_kernel, blocks_per_seq, n_tiles),
        grid=(n_tiles + 1,),
        in_specs=(
            [smem, smem,
             pl.BlockSpec((SM_TM, ATTN_WIDTH), lambda t: (attn_tile(t), 0)),
             pl.BlockSpec((SM_TM, KV_WIDTH), lambda t: (attn_tile(t), kcol)),
             pl.BlockSpec((SM_TM, KV_WIDTH), lambda t: (attn_tile(t), vcol)),
             pl.BlockSpec((WINDOW, KV_WIDTH), lambda t: (prev_block(t), kcol)),
             pl.BlockSpec((WINDOW, KV_WIDTH), lambda t: (prev_block(t), vcol)),
             pl.BlockSpec((SM_TM, ATTN_WIDTH // 2), lambda t: (attn_tile(t), gcol)),
             pl.BlockSpec((SM_TM, ATTN_WIDTH // 2), lambda t: (attn_tile(t), gcol + 1)),
             pl.BlockSpec((SM_TM, HGRN_WIDTH), lambda t: (merge_tile(t), 0))]
            + [gate(OFF_GA, j) for j in range(OUT_NCHUNK)]
            + [gate(OFF_GH, j) for j in range(OUT_NCHUNK)]
            + [pl.BlockSpec((SM_TM, D_MODEL), lambda t: (merge_tile(t), 0)),
               resident((ATTN_WIDTH, D_MODEL)), resident((HGRN_WIDTH, D_MODEL)),
               resident((D_MODEL, D_MODEL)), pl.BlockSpec((1, D_MODEL), const)]),
        out_specs=pl.BlockSpec((SM_TM, D_MODEL), lambda t: (merge_tile(t), 0)),
        out_shape=jax.ShapeDtypeStruct((tokens, D_MODEL), F32),
        scratch_shapes=[
            pltpu.VMEM((2, ATTN_HEADS, WINDOW, WINDOW), F32),
            pltpu.VMEM((2 * WINDOW, WINDOW), BF16),
            pltpu.VMEM((SM_TM, ATTN_WIDTH), BF16),
            pltpu.VMEM((SM_TM, D_MODEL), BF16),
        ],
        compiler_params=pltpu.CompilerParams(
            dimension_semantics=("arbitrary",), vmem_limit_bytes=SM_VMEM_BYTES),
        name="swa_merge",
    )(rel_bias, sinks, proj, proj, proj, proj, proj, proj, proj, yh,
      *([proj] * (2 * OUT_NCHUNK)), x2d, wa, wh, wo, gain)


def kernel(x, norm_pre, w_in, rel_bias, attn_sinks, lb_logits, hgrn_norm, w_branch_attn,
           w_branch_hgrn, w_out, norm_post):
    batch, seq, d_model = x.shape
    depth = w_in.shape[0]
    assert depth == 1 and d_model == D_MODEL and seq % IN_TM == 0
    assert lb_logits.shape == (depth + 1, HGRN_WIDTH)
    assert sum(SEG_BATCHES) == batch
    x2d = x.reshape(batch * seq, d_model)
    layer = 0
    gain_pre = norm_pre[layer][None, :]
    w_bf16 = w_in[layer].astype(BF16)
    tiles_per_seq = seq // IN_TM
    proj, hg = _inproj(x2d, gain_pre, w_bf16, 0, SEG_BATCHES[0] * tiles_per_seq)
    yh = None
    batch0 = 0
    for prev_batches, seg_batches in zip(SEG_BATCHES[:-1], SEG_BATCHES[1:]):
        proj, hg, yh = _inproj_hgrn(
            x2d, gain_pre, w_bf16, (batch0 + prev_batches) * tiles_per_seq,
            seg_batches * tiles_per_seq, hg, batch0, proj, yh, lb_logits, hgrn_norm[layer], seq)
        batch0 += prev_batches
    yh = _hgrn(hg, lb_logits, hgrn_norm[layer], yh, batch0, batch * seq, seq)
    out = _swa_merge(proj, yh, x2d, rel_bias, attn_sinks[layer][None, :],
                     w_branch_attn[layer].astype(BF16), w_branch_hgrn[layer].astype(BF16),
                     w_out[layer].astype(BF16), norm_post[layer][None, :], seq)
    return out.reshape(batch, seq, d_model)
```

```python
import functools
import math

import jax
import jax.numpy as jnp
from jax import lax
from jax.experimental import pallas as pl
from jax.experimental.pallas import tpu as pltpu

D_MODEL = 2048
ATTN_HEADS = 16
ATTN_KV_HEADS = 4
ATTN_HEAD_DIM = 64
WINDOW = 128
ATTN_WIDTH = ATTN_HEADS * ATTN_HEAD_DIM
KV_WIDTH = ATTN_KV_HEADS * ATTN_HEAD_DIM
HGRN_HEADS = 8
HGRN_DIM = 128
HGRN_WIDTH = HGRN_HEADS * HGRN_DIM
CHUNK = 64
CHUNK_LEVELS = 6
HGRN_UNROLL = 8
HGRN_SLOTS = 8
HGRN_HEADS_PER_STEP = 2
SEG_BATCHES = (2,) * 8
REL_BUCKETS = 32
REL_MAX_DIST = 128
NORM_EPS = 1e-6
IN_PROJ_WIDTH = 10752

OFF_AQ = 0
OFF_AK = OFF_AQ + ATTN_WIDTH
OFF_AV = OFF_AK + KV_WIDTH
OFF_AG = OFF_AV + KV_WIDTH
OFF_HQ = OFF_AG + ATTN_WIDTH
OFF_HF = OFF_HQ + HGRN_WIDTH
OFF_HI = OFF_HF + HGRN_WIDTH
OFF_HG = OFF_HI + HGRN_WIDTH
OFF_GA = OFF_HG + HGRN_WIDTH
OFF_GH = OFF_GA + D_MODEL

LANES = 128
SUBLANES = 8
MASK_VALUE = -1e30

F32 = jnp.float32
BF16 = jnp.bfloat16

_NT = (((1,), (1,)), ((), ()))
_TN = (((0,), (0,)), ((), ()))


def _sigmoid(x):
    return 1.0 / (1.0 + jnp.exp(-x))


def _silu(x):
    return x * _sigmoid(x)


IN_TM = 1024
IN_TN = 1536
IN_NORM_ROWS = 256
IN_PIECE = 256
IN_PIECES = IN_TN // IN_PIECE
MIB = 1024 * 1024
IN_VMEM_BYTES = 56 * MIB
IN_HGRN_VMEM_BYTES = 58 * MIB


def _prenorm(x_ref, gain_ref, h_ref):
    gain = gain_ref[...]
    for r in range(0, IN_TM, IN_NORM_ROWS):
        xf = x_ref[r:r + IN_NORM_ROWS, :]
        ms = jnp.mean(xf * xf, axis=-1, keepdims=True)
        h_ref[r:r + IN_NORM_ROWS, :] = (xf * lax.rsqrt(ms + NORM_EPS) * gain).astype(BF16)


HG_TILE0 = OFF_HQ // IN_TN
HG_TILES = (OFF_GA - 1) // IN_TN - HG_TILE0 + 1
HG_COL0 = HG_TILE0 * IN_TN


def _copy_hgrn_tile(o_ref, hg_ref):
    col_step = pl.program_id(1)

    @pl.when((col_step >= HG_TILE0) & (col_step < HG_TILE0 + HG_TILES))
    def _():
        hg_ref[...] = o_ref[...]


def _inproj_kernel(x_ref, gain_ref, w_ref, o_ref, hg_ref, h_ref):
    @pl.when(pl.program_id(1) == 0)
    def _():
        _prenorm(x_ref, gain_ref, h_ref)

    o_ref[...] = jnp.dot(h_ref[...], w_ref[...], preferred_element_type=F32).astype(BF16)
    _copy_hgrn_tile(o_ref, hg_ref)


def _inproj_specs(row_tile0):
    return [
        pl.BlockSpec((IN_TM, D_MODEL), lambda i, j: (row_tile0 + i, 0)),
        pl.BlockSpec((1, D_MODEL), lambda i, j: (0, 0)),
        pl.BlockSpec((D_MODEL, IN_TN), lambda i, j: (0, j)),
    ]


def _inproj_out_specs(row_tile0):
    hg_tile = lambda j: jnp.clip(j - HG_TILE0, 0, HG_TILES - 1)
    return [pl.BlockSpec((IN_TM, IN_TN), lambda i, j: (row_tile0 + i, j)),
            pl.BlockSpec((IN_TM, IN_TN), lambda i, j: (i, hg_tile(j)))]


def _inproj_out_shapes(tokens, n_row_tiles):
    return [jax.ShapeDtypeStruct((tokens, IN_PROJ_WIDTH), BF16),
            jax.ShapeDtypeStruct((n_row_tiles * IN_TM, HG_TILES * IN_TN), BF16)]


def _inproj(x2d, gain, w_bf16, row_tile0, n_row_tiles):
    return pl.pallas_call(
        _inproj_kernel,
        grid=(n_row_tiles, IN_PROJ_WIDTH // IN_TN),
        in_specs=_inproj_specs(row_tile0),
        out_specs=_inproj_out_specs(row_tile0),
        out_shape=_inproj_out_shapes(x2d.shape[0], n_row_tiles),
        scratch_shapes=[pltpu.VMEM((IN_TM, D_MODEL), BF16)],
        compiler_params=pltpu.CompilerParams(
            dimension_semantics=("parallel", "arbitrary"),
            vmem_limit_bytes=IN_VMEM_BYTES),
        name="inproj",
    )(x2d, gain, w_bf16)


def _level_reference(g_ref, level, sub):
    half = 1 << level
    blk = 2 * half

    def row_tile(r):
        return g_ref[pl.ds(r, SUBLANES, stride=0), :]

    if blk >= SUBLANES:
        return jnp.concatenate(
            [row_tile(b * blk + half) for b in range(CHUNK // blk) for _ in range(blk // SUBLANES)],
            axis=0)
    tiles = []
    for t in range(CHUNK // SUBLANES):
        acc = row_tile(t * SUBLANES + half)
        for b in range(1, SUBLANES // blk):
            acc = jnp.where(sub >= b * blk, row_tile(t * SUBLANES + b * blk + half), acc)
        tiles.append(acc)
    return jnp.concatenate(tiles, axis=0)


def _neg_abs(x):
    bits = lax.bitcast_convert_type(x, jnp.uint32) | jnp.uint32(0x80000000)
    return lax.bitcast_convert_type(bits, F32)


def _hgrn_kernel(n_carried, lb_ref, hn_ref, hq_ref, hf_ref, hi_ref, hg_ref, *rest):
    o_ref, *scratch = rest[n_carried:]
    for j in range(HGRN_HEADS_PER_STEP):
        cols = slice(j * HGRN_DIM, (j + 1) * HGRN_DIM)
        head = pl.program_id(1) * HGRN_HEADS_PER_STEP + j
        _hgrn_head(lb_ref.at[:, cols], hn_ref.at[pl.ds(head, 1), :], hq_ref.at[:, cols],
                   hf_ref.at[:, cols], hi_ref.at[:, cols], hg_ref.at[:, cols], o_ref.at[:, cols],
                   *scratch)


def _hgrn_head(lb_ref, hn_ref, hq_ref, hf_ref, hi_ref, hg_ref, o_ref,
               g_ref, qg_ref, attn_ref, upd_ref, keep_ref, state_ref, piece_fn=None, n_pieces=0):
    seq = hq_ref.shape[0]
    lg = lb_ref[...]
    ex = jnp.exp(lg - jnp.max(lg, axis=0, keepdims=True))
    lb = ex[0:1, :] / jnp.sum(ex, axis=0, keepdims=True)
    hn = hn_ref[...]

    rowi = lax.broadcasted_iota(jnp.int32, (CHUNK, HGRN_DIM), 0)
    sub = lax.broadcasted_iota(jnp.int32, (SUBLANES, HGRN_DIM), 0)
    ti = lax.broadcasted_iota(jnp.int32, (CHUNK, CHUNK), 0)
    si = lax.broadcasted_iota(jnp.int32, (CHUNK, CHUNK), 1)
    xor = ti ^ si
    lvl = jnp.full((CHUNK, CHUNK), -2, jnp.int32)
    for level in range(CHUNK_LEVELS):
        lvl = jnp.where(xor >= (1 << level), level, lvl)
    lvl = jnp.where(ti > si, lvl, jnp.where(ti == si, -1, -2))

    def chunk_rows(c):
        start = c * CHUNK
        return pl.ds(start if isinstance(c, int) else pl.multiple_of(start, CHUNK), CHUNK)

    def local_step(c, slot):
        rows = chunk_rows(c)
        q = _silu(hq_ref[rows, :].astype(F32))
        f = lb + (1.0 - lb) * _sigmoid(hf_ref[rows, :].astype(F32))
        k = 1.0 - f
        v = hi_ref[rows, :]
        g = jnp.log2(f)
        for shift in (1, 2, 4):
            g = g + jnp.where(rowi >= shift, pltpu.roll(g, shift, 0), 0.0)
        for shift in (8, 16, 32):
            g = g + jnp.concatenate(
                [jnp.zeros((shift, HGRN_DIM), F32), g[:CHUNK - shift]], axis=0)
        g_slot = g_ref.at[slot]
        g_slot[...] = g
        g_last = g_slot[pl.ds(CHUNK - 1, SUBLANES, stride=0), :][0:1]
        qb = q.astype(BF16)
        kb = k.astype(BF16)
        qg_ref[rows, :] = qb * jnp.exp2(g).astype(BF16)

        attn = jnp.where(lvl == -1, jnp.sum(q * k, axis=-1, keepdims=True), 0.0)
        for level in range(CHUNK_LEVELS):
            dec = jnp.exp2(_neg_abs(g - _level_reference(g_slot, level, sub))).astype(BF16)
            a = lax.dot_general(qb * dec, kb * dec, _NT, preferred_element_type=F32)
            attn = jnp.where(lvl == level, a, attn)
        attn_ref[c] = attn.astype(BF16)

        k_dec = kb * jnp.exp2(g_last - g).astype(BF16)
        upd_ref[c] = lax.dot_general(v, k_dec, _TN, preferred_element_type=F32)
        keep_ref[c] = jnp.broadcast_to(jnp.exp2(g_last), (SUBLANES, HGRN_DIM))

    def local_group(i, carry):
        for slot in range(HGRN_UNROLL):
            local_step(i * HGRN_UNROLL + slot, slot)
        return carry

    def state_step(c, state_t):
        state_ref[c] = state_t.astype(BF16)
        return keep_ref[c][0:1] * state_t + upd_ref[c]

    def out_step(c, carry):
        rows = chunk_rows(c)
        o = lax.dot_general(qg_ref[rows, :], state_ref[c], _NT, preferred_element_type=F32)
        o = o + jnp.dot(attn_ref[c], hi_ref[rows, :], preferred_element_type=F32)
        ms = jnp.mean(o * o, axis=-1, keepdims=True)
        y = o * lax.rsqrt(ms + NORM_EPS) * hn * _silu(hg_ref[rows, :].astype(F32))
        o_ref[rows, :] = y.astype(BF16)
        return carry

    n_chunks = seq // CHUNK
    state0 = jnp.zeros((HGRN_DIM, HGRN_DIM), F32)
    if n_pieces:
        bounds = [(p * n_chunks) // n_pieces for p in range(n_pieces + 1)]
        state_t = state0
        for p in range(n_pieces):
            piece_fn(p)
            for c in range(bounds[p], bounds[p + 1]):
                local_step(c, c - bounds[p])
                if c >= 1:
                    state_t = state_step(c - 1, state_t)
                if c >= 2:
                    out_step(c - 2, 0)
        state_step(n_chunks - 1, state_t)
        out_step(n_chunks - 2, 0)
        out_step(n_chunks - 1, 0)
    else:
        def finish_group(i, state_t):
            for slot in range(HGRN_UNROLL):
                c = i * HGRN_UNROLL + slot
                state_t = state_step(c, state_t)
                out_step(c, 0)
            return state_t

        def lagged_group(i, state_t):
            state_t = finish_group(i - 1, state_t)
            local_group(i, 0)
            return state_t

        n_groups = n_chunks // HGRN_UNROLL
        local_group(0, 0)
        state_t = lax.fori_loop(1, n_groups, lagged_group, state0)
        finish_group(n_groups - 1, state_t)


def _hgrn_scratch(seq):
    n_chunks = seq // CHUNK
    return [
        pltpu.VMEM((HGRN_SLOTS, CHUNK, HGRN_DIM), F32),
        pltpu.VMEM((seq, HGRN_DIM), BF16),
        pltpu.VMEM((n_chunks, CHUNK, CHUNK), BF16),
        pltpu.VMEM((n_chunks, HGRN_DIM, HGRN_DIM), F32),
        pltpu.VMEM((n_chunks, SUBLANES, HGRN_DIM), F32),
        pltpu.VMEM((n_chunks, HGRN_DIM, HGRN_DIM), BF16),
    ]


def _hgrn(hg_seg, lb_logits, hgrn_norm, yh_prev, batch0, tokens, seq):
    width = HGRN_HEADS_PER_STEP * HGRN_DIM
    seg_batch = hg_seg.shape[0] // seq
    blk = lambda off: pl.BlockSpec((seq, width), lambda b, h: (b, (off - HG_COL0) // width + h))
    carried = [] if yh_prev is None else [yh_prev]
    return pl.pallas_call(
        functools.partial(_hgrn_kernel, len(carried)),
        grid=(seg_batch, HGRN_HEADS // HGRN_HEADS_PER_STEP),
        in_specs=[
            pl.BlockSpec((lb_logits.shape[0], width), lambda b, h: (0, h)),
            pl.BlockSpec((HGRN_HEADS, HGRN_DIM), lambda b, h: (0, 0)),
            blk(OFF_HQ), blk(OFF_HF), blk(OFF_HI), blk(OFF_HG),
        ] + [pl.BlockSpec(memory_space=pl.ANY) for _ in carried],
        out_specs=pl.BlockSpec((seq, width), lambda b, h: (batch0 + b, h)),
        out_shape=jax.ShapeDtypeStruct((tokens, HGRN_WIDTH), BF16),
        input_output_aliases={6: 0} if carried else {},
        scratch_shapes=_hgrn_scratch(seq),
        compiler_params=pltpu.CompilerParams(dimension_semantics=("parallel", "parallel")),
        name="hgrn2",
    )(lb_logits, hgrn_norm, hg_seg, hg_seg, hg_seg, hg_seg, *carried)


def _inproj_hgrn_kernel(heads_per_tile, n_carried, x_ref, gain_ref, w_ref, lb_ref, hn_ref, hq_ref,
                        hf_ref, hi_ref, hg_ref, *rest):
    o_ref, hg_out_ref, yh_ref, h_ref, *hgrn_scratch = rest[n_carried:]
    col_step = pl.program_id(1)

    @pl.when(col_step == 0)
    def _():
        _prenorm(x_ref, gain_ref, h_ref)

    def piece(p):
        cs = slice(p * IN_PIECE, (p + 1) * IN_PIECE)
        tile = jnp.dot(h_ref[...], w_ref[:, cs], preferred_element_type=F32).astype(BF16)
        o_ref[:, cs] = tile
        hg_out_ref[:, cs] = tile

    @pl.when(col_step < heads_per_tile)
    def _():
        head = (pl.program_id(0) * heads_per_tile + col_step) % HGRN_HEADS
        _hgrn_head(lb_ref, hn_ref.at[pl.ds(head, 1), :], hq_ref, hf_ref, hi_ref, hg_ref, yh_ref,
                   *hgrn_scratch, piece_fn=piece, n_pieces=IN_PIECES)

    @pl.when(col_step >= heads_per_tile)
    def _():
        o_ref[...] = jnp.dot(h_ref[...], w_ref[...], preferred_element_type=F32).astype(BF16)
        _copy_hgrn_tile(o_ref, hg_out_ref)


def _inproj_hgrn(x2d, gain, w_bf16, row_tile0, n_row_tiles, hg_prev, batch0_prev, proj_prev, yh_prev,
                 lb_logits, hgrn_norm, seq):
    tokens = x2d.shape[0]
    n_heads = (hg_prev.shape[0] // seq) * HGRN_HEADS
    heads_per_tile = n_heads // n_row_tiles
    assert heads_per_tile * n_row_tiles == n_heads
    assert heads_per_tile <= HG_TILE0 + HG_TILES
    assert pl.cdiv(seq // CHUNK, IN_PIECES) <= HGRN_SLOTS

    def flat_head(i, j):
        return i * heads_per_tile + jnp.minimum(j, heads_per_tile - 1)

    def blk(off):
        return pl.BlockSpec((seq, HGRN_DIM), lambda i, j: (
            flat_head(i, j) // HGRN_HEADS, (off - HG_COL0) // HGRN_DIM + flat_head(i, j) % HGRN_HEADS))

    carried = [proj_prev] + ([] if yh_prev is None else [yh_prev])
    n_read = 9
    aliases = {n_read: 0} if yh_prev is None else {n_read: 0, n_read + 1: 2}
    return pl.pallas_call(
        functools.partial(_inproj_hgrn_kernel, heads_per_tile, len(carried)),
        grid=(n_row_tiles, IN_PROJ_WIDTH // IN_TN),
        in_specs=_inproj_specs(row_tile0) + [
            pl.BlockSpec((lb_logits.shape[0], HGRN_DIM),
                         lambda i, j: (0, flat_head(i, j) % HGRN_HEADS)),
            pl.BlockSpec((HGRN_HEADS, HGRN_DIM), lambda i, j: (0, 0)),
            blk(OFF_HQ), blk(OFF_HF), blk(OFF_HI), blk(OFF_HG),
        ] + [pl.BlockSpec(memory_space=pl.ANY) for _ in carried],
        out_specs=_inproj_out_specs(row_tile0) + [
            pl.BlockSpec((seq, HGRN_DIM), lambda i, j: (
                batch0_prev + flat_head(i, j) // HGRN_HEADS, flat_head(i, j) % HGRN_HEADS))],
        out_shape=_inproj_out_shapes(tokens, n_row_tiles) + [
            jax.ShapeDtypeStruct((tokens, HGRN_WIDTH), BF16)],
        input_output_aliases=aliases,
        scratch_shapes=[pltpu.VMEM((IN_TM, D_MODEL), BF16)] + _hgrn_scratch(seq),
        compiler_params=pltpu.CompilerParams(
            dimension_semantics=("arbitrary", "arbitrary"),
            vmem_limit_bytes=IN_HGRN_VMEM_BYTES),
        name="inproj_hgrn2",
    )(x2d, gain, w_bf16, lb_logits, hgrn_norm, hg_prev, hg_prev, hg_prev, hg_prev, *carried)


SM_BLOCKS = 2
SM_TM = SM_BLOCKS * WINDOW
OUT_TN = 512
OUT_NCHUNK = D_MODEL // OUT_TN
SM_VMEM_BYTES = 52 * MIB


def _t5_bucket(dist):
    max_exact = REL_BUCKETS // 2
    d = jnp.maximum(dist, 0)
    df = jnp.maximum(d, 1).astype(F32)
    large = max_exact + (jnp.log(df / max_exact) / math.log(REL_MAX_DIST / max_exact)
                         * (REL_BUCKETS - max_exact)).astype(jnp.int32)
    large = jnp.minimum(large, REL_BUCKETS - 1)
    return jnp.where(d < max_exact, d, large)


def _swap_halves(x):
    half = LANES // 2
    return jnp.concatenate([x[:, half:], x[:, :half]], axis=1)


def _swa_tables(relb_ref, bias_ref, cur_ref):
    qi = lax.broadcasted_iota(jnp.int32, (WINDOW, WINDOW), 0)
    sj = lax.broadcasted_iota(jnp.int32, (WINDOW, WINDOW), 1)
    own = sj <= qi
    bucket = _t5_bucket(jnp.where(own, qi - sj, qi + WINDOW - sj))
    for h in range(ATTN_HEADS):
        def pick(b, acc, h=h):
            return jnp.where(bucket == b, relb_ref[b, h], acc)
        tbl = lax.fori_loop(0, REL_BUCKETS, pick, jnp.zeros((WINDOW, WINDOW), F32))
        bias_ref[0, h] = tbl
        bias_ref[1, h] = jnp.where(own, tbl, MASK_VALUE)
    own_bf = jnp.where(own, 1.0, 0.0).astype(BF16)
    cur_ref[...] = jnp.concatenate([own_bf, own_bf], axis=0)


def _swa_merge_kernel(blocks_per_seq, n_tiles, relb_ref, sink_ref, q_ref, kc_ref, vc_ref, kp_ref,
                      vp_ref, ga_ref, gb_ref, yh_ref, *rest):
    gate_a_refs = rest[:OUT_NCHUNK]
    gate_h_refs = rest[OUT_NCHUNK:2 * OUT_NCHUNK]
    (x_ref, wa_ref, wh_ref, wo_ref, gain_ref, o_ref,
     bias_ref, cur_ref, ya_ref, m_ref) = rest[2 * OUT_NCHUNK:]
    step = pl.program_id(0)

    @pl.when(step == 0)
    def _():
        _swa_tables(relb_ref, bias_ref, cur_ref)
        ya_ref[...] = jnp.zeros_like(ya_ref)

    lane = lax.broadcasted_iota(jnp.int32, (1, LANES), 1)
    is_lo = lane < LANES // 2
    scale = ATTN_HEAD_DIM ** -0.5
    keep_lo = jnp.where(is_lo, 1.0, 0.0).astype(BF16)
    keep_hi = jnp.where(is_lo, 0.0, 1.0).astype(BF16)
    row = lax.broadcasted_iota(jnp.int32, (2 * WINDOW, 1), 0)
    top = row < WINDOW
    own2 = (lax.broadcasted_iota(jnp.int32, (2 * WINDOW, WINDOW), 1)
            <= (lax.broadcasted_iota(jnp.int32, (2 * WINDOW, WINDOW), 0) & (WINDOW - 1)))
    tile = jnp.minimum(step, n_tiles - 1)
    tile_starts_seq = jnp.where(tile % (blocks_per_seq // SM_BLOCKS) == 0, 1, 0)

    def scores(qm, km, first, heads):
        s2 = lax.dot_general(qm, km, _NT, preferred_element_type=F32)
        bias = jnp.concatenate([bias_ref[first, heads[0]], bias_ref[first, heads[1]]], axis=0)
        return jnp.where(own2, s2[:, WINDOW:], s2[:, :WINDOW]) + bias

    def probs(s, heads):
        sink = jnp.where(top, sink_ref[0, heads[0]], sink_ref[0, heads[1]])
        m = jnp.maximum(jnp.max(s, axis=-1, keepdims=True), sink)
        p = jnp.exp(s - m)
        denom = jnp.sum(p, axis=-1, keepdims=True) + jnp.exp(sink - m)
        pb = p.astype(BF16)
        p_own = pb * cur_ref[...]
        return jnp.concatenate([pb - p_own, p_own], axis=1), 1.0 / denom

    def heads_of(hk):
        e = hk % 2
        return (4 * hk + e, 4 * hk + 2 + e), (4 * hk + 1 - e, 4 * hk + 3 - e)

    ya_prev = ya_ref[...]
    yh = yh_ref[...]
    for j in range(OUT_NCHUNK):
        cs = slice(j * OUT_TN, (j + 1) * OUT_TN)
        ua = jnp.dot(ya_prev, wa_ref[:, cs], preferred_element_type=F32)
        uh = jnp.dot(yh, wh_ref[:, cs], preferred_element_type=F32)
        merged = (_sigmoid(gate_a_refs[j][...].astype(F32)) * ua
                  + _sigmoid(gate_h_refs[j][...].astype(F32)) * uh)
        m_ref[:, cs] = merged.astype(BF16)

    s_same, s_diff, vals = [], [], []
    for blk in range(SM_BLOCKS):
        rows = slice(blk * WINDOW, (blk + 1) * WINDOW)
        first = tile_starts_seq if blk == 0 else 0
        for hk in range(ATTN_KV_HEADS):
            c, e = divmod(hk, 2)
            cs = slice(c * LANES, (c + 1) * LANES)
            if blk == 0:
                k_prev, v_prev = kp_ref[:, cs], vp_ref[:, cs]
            else:
                prev_rows = slice((blk - 1) * WINDOW, blk * WINDOW)
                k_prev, v_prev = kc_ref[prev_rows, cs], vc_ref[prev_rows, cs]
            k2 = jnp.concatenate([k_prev, kc_ref[rows, cs]], axis=0)
            v2 = jnp.concatenate([v_prev, vc_ref[rows, cs]], axis=0)
            k2s, v2s = _swap_halves(k2), _swap_halves(v2)
            keep_e, keep_o = (keep_lo, keep_hi) if e == 0 else (keep_hi, keep_lo)
            p0 = q_ref[rows, hk * 2 * LANES:hk * 2 * LANES + LANES]
            p1 = q_ref[rows, hk * 2 * LANES + LANES:(hk + 1) * 2 * LANES]
            q_same = jnp.concatenate([p0 * (keep_e * scale), p1 * (keep_e * scale)], axis=0)
            q_diff = jnp.concatenate([p0 * (keep_o * scale), p1 * (keep_o * scale)], axis=0)
            h_same, h_diff = heads_of(hk)
            s_same.append(scores(q_same, k2, first, h_same))
            s_diff.append(scores(q_diff, k2s, first, h_diff))
            vals.append((v2 * keep_lo, v2s * keep_hi) if e == 0 else (v2s * keep_lo, v2 * keep_hi))

    y = jnp.dot(m_ref[...], wo_ref[...], preferred_element_type=F32)

    pr_same = [probs(s_same[i], heads_of(i % ATTN_KV_HEADS)[0]) for i in range(len(s_same))]
    pr_diff = [probs(s_diff[i], heads_of(i % ATTN_KV_HEADS)[1]) for i in range(len(s_diff))]
    for blk in range(SM_BLOCKS):
        rows = slice(blk * WINDOW, (blk + 1) * WINDOW)
        for hk in range(ATTN_KV_HEADS):
            i = blk * ATTN_KV_HEADS + hk
            (p_lo, r_lo), (p_hi, r_hi) = ((pr_same[i], pr_diff[i]) if hk % 2 == 0
                                          else (pr_diff[i], pr_same[i]))
            v_lo, v_hi = vals[i]
            pv = jnp.dot(jnp.concatenate([p_lo, p_hi], axis=1),
                         jnp.concatenate([v_lo, v_hi], axis=0),
                         preferred_element_type=F32)
            pv = pv * jnp.where(is_lo, r_lo, r_hi)
            g_ref = ga_ref if hk < 2 else gb_ref
            for pair in range(2):
                gs = slice((hk % 2) * 2 * LANES + pair * LANES,
                           (hk % 2) * 2 * LANES + (pair + 1) * LANES)
                gate = _silu(g_ref[rows, gs].astype(F32))
                ya_ref[rows, (2 * hk + pair) * LANES:(2 * hk + pair + 1) * LANES] = (
                    pv[pair * WINDOW:(pair + 1) * WINDOW] * gate).astype(BF16)

    ms = jnp.mean(y * y, axis=-1, keepdims=True)
    o_ref[...] = x_ref[...] + y * lax.rsqrt(ms + NORM_EPS) * gain_ref[...]


def _swa_merge(proj, yh, x2d, rel_bias, sinks, wa, wh, wo, gain, seq):
    tokens = x2d.shape[0]
    n_tiles = tokens // SM_TM
    blocks_per_seq = seq // WINDOW
    attn_tile = lambda t: jnp.minimum(t, n_tiles - 1)
    merge_tile = lambda t: jnp.maximum(t - 1, 0)
    prev_block = lambda t: jnp.maximum(attn_tile(t) * SM_BLOCKS - 1, 0)
    kcol, vcol = OFF_AK // KV_WIDTH, OFF_AV // KV_WIDTH
    gcol = OFF_AG // (ATTN_WIDTH // 2)
    const = lambda t: (0, 0)
    smem = pl.BlockSpec(memory_space=pltpu.SMEM)
    gate = lambda off, j: pl.BlockSpec((SM_TM, OUT_TN), lambda t: (merge_tile(t), off // OUT_TN + j))
    resident = functools.partial(pl.BlockSpec, index_map=const, pipeline_mode=pl.Buffered(1))
    return pl.pallas_call(
        functools.partial(_swa_merge_kernel, blocks_per_seq, n_tiles),
        grid=(n_tiles + 1,),
        in_specs=(
            [smem, smem,
             pl.BlockSpec((SM_TM, ATTN_WIDTH), lambda t: (attn_tile(t), 0)),
             pl.BlockSpec((SM_TM, KV_WIDTH), lambda t: (attn_tile(t), kcol)),
             pl.BlockSpec((SM_TM, KV_WIDTH), lambda t: (attn_tile(t), vcol)),
             pl.BlockSpec((WINDOW, KV_WIDTH), lambda t: (prev_block(t), kcol)),
             pl.BlockSpec((WINDOW, KV_WIDTH), lambda t: (prev_block(t), vcol)),
             pl.BlockSpec((SM_TM, ATTN_WIDTH // 2), lambda t: (attn_tile(t), gcol)),
             pl.BlockSpec((SM_TM, ATTN_WIDTH // 2), lambda t: (attn_tile(t), gcol + 1)),
             pl.BlockSpec((SM_TM, HGRN_WIDTH), lambda t: (merge_tile(t), 0))]
            + [gate(OFF_GA, j) for j in range(OUT_NCHUNK)]
            + [gate(OFF_GH, j) for j in range(OUT_NCHUNK)]
            + [pl.BlockSpec((SM_TM, D_MODEL), lambda t: (merge_tile(t), 0)),
               resident((ATTN_WIDTH, D_MODEL)), resident((HGRN_WIDTH, D_MODEL)),
               resident((D_MODEL, D_MODEL)), pl.BlockSpec((1, D_MODEL), const)]),
        out_specs=pl.BlockSpec((SM_TM, D_MODEL), lambda t: (merge_tile(t), 0)),
        out_shape=jax.ShapeDtypeStruct((tokens, D_MODEL), F32),
        scratch_shapes=[
            pltpu.VMEM((2, ATTN_HEADS, WINDOW, WINDOW), F32),
            pltpu.VMEM((2 * WINDOW, WINDOW), BF16),
            pltpu.VMEM((SM_TM, ATTN_WIDTH), BF16),
            pltpu.VMEM((SM_TM, D_MODEL), BF16),
        ],
        compiler_params=pltpu.CompilerParams(
            dimension_semantics=("arbitrary",), vmem_limit_bytes=SM_VMEM_BYTES),
        name="swa_merge",
    )(rel_bias, sinks, proj, proj, proj, proj, proj, proj, proj, yh,
      *([proj] * (2 * OUT_NCHUNK)), x2d, wa, wh, wo, gain)


def kernel(x, norm_pre, w_in, rel_bias, attn_sinks, lb_logits, hgrn_norm, w_branch_attn,
           w_branch_hgrn, w_out, norm_post):
    batch, seq, d_model = x.shape
    depth = w_in.shape[0]
    assert depth == 1 and d_model == D_MODEL and seq % IN_TM == 0
    assert lb_logits.shape == (depth + 1, HGRN_WIDTH)
    assert sum(SEG_BATCHES) == batch
    x2d = x.reshape(batch * seq, d_model)
    layer = 0
    gain_pre = norm_pre[layer][None, :]
    w_bf16 = w_in[layer].astype(BF16)
    tiles_per_seq = seq // IN_TM
    proj, hg = _inproj(x2d, gain_pre, w_bf16, 0, SEG_BATCHES[0] * tiles_per_seq)
    yh = None
    batch0 = 0
    for prev_batches, seg_batches in zip(SEG_BATCHES[:-1], SEG_BATCHES[1:]):
        proj, hg, yh = _inproj_hgrn(
            x2d, gain_pre, w_bf16, (batch0 + prev_batches) * tiles_per_seq,
            seg_batches * tiles_per_seq, hg, batch0, proj, yh, lb_logits, hgrn_norm[layer], seq)
        batch0 += prev_batches
    yh = _hgrn(hg, lb_logits, hgrn_norm[layer], yh, batch0, batch * seq, seq)
    out = _swa_merge(proj, yh, x2d, rel_bias, attn_sinks[layer][None, :],
                     w_branch_attn[layer].astype(BF16), w_branch_hgrn[layer].astype(BF16),
                     w_out[layer].astype(BF16), norm_post[layer][None, :], seq)
    return out.reshape(batch, seq, d_model)
```

```python
import functools
import math

import jax
import jax.numpy as jnp
from jax import lax
from jax.experimental import pallas as pl
from jax.experimental.pallas import tpu as pltpu

D_MODEL = 2048
ATTN_HEADS = 16
ATTN_KV_HEADS = 4
ATTN_HEAD_DIM = 64
WINDOW = 128
ATTN_WIDTH = ATTN_HEADS * ATTN_HEAD_DIM
KV_WIDTH = ATTN_KV_HEADS * ATTN_HEAD_DIM
HGRN_HEADS = 8
HGRN_DIM = 128
HGRN_WIDTH = HGRN_HEADS * HGRN_DIM
CHUNK = 64
CHUNK_LEVELS = 6
HGRN_UNROLL = 16
HGRN_SLOTS = 16
HGRN_HEADS_PER_STEP = 2
SEG_BATCHES = (2,) * 8
REL_BUCKETS = 32
REL_MAX_DIST = 128
NORM_EPS = 1e-6
IN_PROJ_WIDTH = 10752

OFF_AQ = 0
OFF_AK = OFF_AQ + ATTN_WIDTH
OFF_AV = OFF_AK + KV_WIDTH
OFF_AG = OFF_AV + KV_WIDTH
OFF_HQ = OFF_AG + ATTN_WIDTH
OFF_HF = OFF_HQ + HGRN_WIDTH
OFF_HI = OFF_HF + HGRN_WIDTH
OFF_HG = OFF_HI + HGRN_WIDTH
OFF_GA = OFF_HG + HGRN_WIDTH
OFF_GH = OFF_GA + D_MODEL

LANES = 128
SUBLANES = 8
MASK_VALUE = -1e30

F32 = jnp.float32
BF16 = jnp.bfloat16

_NT = (((1,), (1,)), ((), ()))
_TN = (((0,), (0,)), ((), ()))


def _sigmoid(x):
    return 1.0 / (1.0 + jnp.exp(-x))


def _silu(x):
    return x * _sigmoid(x)


IN_TM = 1024
IN_TN = 1536
IN_NORM_ROWS = 256
IN_PIECE = 256
IN_PIECES = IN_TN // IN_PIECE
MIB = 1024 * 1024
IN_VMEM_BYTES = 56 * MIB
IN_HGRN_VMEM_BYTES = 58 * MIB


def _prenorm(x_ref, gain_ref, h_ref):
    gain = gain_ref[...]
    for r in range(0, IN_TM, IN_NORM_ROWS):
        xf = x_ref[r:r + IN_NORM_ROWS, :]
        ms = jnp.mean(xf * xf, axis=-1, keepdims=True)
        h_ref[r:r + IN_NORM_ROWS, :] = (xf * lax.rsqrt(ms + NORM_EPS) * gain).astype(BF16)


HG_TILE0 = OFF_HQ // IN_TN
HG_TILES = (OFF_GA - 1) // IN_TN - HG_TILE0 + 1
HG_COL0 = HG_TILE0 * IN_TN


def _copy_hgrn_tile(o_ref, hg_ref):
    col_step = pl.program_id(1)

    @pl.when((col_step >= HG_TILE0) & (col_step < HG_TILE0 + HG_TILES))
    def _():
        hg_ref[...] = o_ref[...]


def _inproj_kernel(x_ref, gain_ref, w_ref, o_ref, hg_ref, h_ref):
    @pl.when(pl.program_id(1) == 0)
    def _():
        _prenorm(x_ref, gain_ref, h_ref)

    o_ref[...] = jnp.dot(h_ref[...], w_ref[...], preferred_element_type=F32).astype(BF16)
    _copy_hgrn_tile(o_ref, hg_ref)


def _inproj_specs(row_tile0):
    return [
        pl.BlockSpec((IN_TM, D_MODEL), lambda i, j: (row_tile0 + i, 0)),
        pl.BlockSpec((1, D_MODEL), lambda i, j: (0, 0)),
        pl.BlockSpec((D_MODEL, IN_TN), lambda i, j: (0, j)),
    ]


def _inproj_out_specs(row_tile0):
    hg_tile = lambda j: jnp.clip(j - HG_TILE0, 0, HG_TILES - 1)
    return [pl.BlockSpec((IN_TM, IN_TN), lambda i, j: (row_tile0 + i, j)),
            pl.BlockSpec((IN_TM, IN_TN), lambda i, j: (i, hg_tile(j)))]


def _inproj_out_shapes(tokens, n_row_tiles):
    return [jax.ShapeDtypeStruct((tokens, IN_PROJ_WIDTH), BF16),
            jax.ShapeDtypeStruct((n_row_tiles * IN_TM, HG_TILES * IN_TN), BF16)]


def _inproj(x2d, gain, w_bf16, row_tile0, n_row_tiles):
    return pl.pallas_call(
        _inproj_kernel,
        grid=(n_row_tiles, IN_PROJ_WIDTH // IN_TN),
        in_specs=_inproj_specs(row_tile0),
        out_specs=_inproj_out_specs(row_tile0),
        out_shape=_inproj_out_shapes(x2d.shape[0], n_row_tiles),
        scratch_shapes=[pltpu.VMEM((IN_TM, D_MODEL), BF16)],
        compiler_params=pltpu.CompilerParams(
            dimension_semantics=("parallel", "arbitrary"),
            vmem_limit_bytes=IN_VMEM_BYTES),
        name="inproj",
    )(x2d, gain, w_bf16)


def _level_reference(g_ref, level, sub):
    half = 1 << level
    blk = 2 * half

    def row_tile(r):
        return g_ref[pl.ds(r, SUBLANES, stride=0), :]

    if blk >= SUBLANES:
        return jnp.concatenate(
            [row_tile(b * blk + half) for b in range(CHUNK // blk) for _ in range(blk // SUBLANES)],
            axis=0)
    tiles = []
    for t in range(CHUNK // SUBLANES):
        acc = row_tile(t * SUBLANES + half)
        for b in range(1, SUBLANES // blk):
            acc = jnp.where(sub >= b * blk, row_tile(t * SUBLANES + b * blk + half), acc)
        tiles.append(acc)
    return jnp.concatenate(tiles, axis=0)


def _neg_abs(x):
    bits = lax.bitcast_convert_type(x, jnp.uint32) | jnp.uint32(0x80000000)
    return lax.bitcast_convert_type(bits, F32)


def _hgrn_kernel(n_carried, lb_ref, hn_ref, hq_ref, hf_ref, hi_ref, hg_ref, *rest):
    o_ref, *scratch = rest[n_carried:]
    for j in range(HGRN_HEADS_PER_STEP):
        cols = slice(j * HGRN_DIM, (j + 1) * HGRN_DIM)
        head = pl.program_id(1) * HGRN_HEADS_PER_STEP + j
        _hgrn_head(lb_ref.at[:, cols], hn_ref.at[pl.ds(head, 1), :], hq_ref.at[:, cols],
                   hf_ref.at[:, cols], hi_ref.at[:, cols], hg_ref.at[:, cols], o_ref.at[:, cols],
                   *scratch)


def _hgrn_head(lb_ref, hn_ref, hq_ref, hf_ref, hi_ref, hg_ref, o_ref,
               g_ref, qg_ref, attn_ref, upd_ref, keep_ref, state_ref, piece_fn=None, n_pieces=0):
    seq = hq_ref.shape[0]
    lg = lb_ref[...]
    ex = jnp.exp(lg - jnp.max(lg, axis=0, keepdims=True))
    lb = ex[0:1, :] / jnp.sum(ex, axis=0, keepdims=True)
    hn = hn_ref[...]

    rowi = lax.broadcasted_iota(jnp.int32, (CHUNK, HGRN_DIM), 0)
    sub = lax.broadcasted_iota(jnp.int32, (SUBLANES, HGRN_DIM), 0)
    ti = lax.broadcasted_iota(jnp.int32, (CHUNK, CHUNK), 0)
    si = lax.broadcasted_iota(jnp.int32, (CHUNK, CHUNK), 1)
    xor = ti ^ si
    lvl = jnp.full((CHUNK, CHUNK), -2, jnp.int32)
    for level in range(CHUNK_LEVELS):
        lvl = jnp.where(xor >= (1 << level), level, lvl)
    lvl = jnp.where(ti > si, lvl, jnp.where(ti == si, -1, -2))

    def chunk_rows(c):
        start = c * CHUNK
        return pl.ds(start if isinstance(c, int) else pl.multiple_of(start, CHUNK), CHUNK)

    def local_step(c, slot):
        rows = chunk_rows(c)
        q = _silu(hq_ref[rows, :].astype(F32))
        f = lb + (1.0 - lb) * _sigmoid(hf_ref[rows, :].astype(F32))
        k = 1.0 - f
        v = hi_ref[rows, :]
        g = jnp.log2(f)
        for shift in (1, 2, 4):
            g = g + jnp.where(rowi >= shift, pltpu.roll(g, shift, 0), 0.0)
        for shift in (8, 16, 32):
            g = g + jnp.concatenate(
                [jnp.zeros((shift, HGRN_DIM), F32), g[:CHUNK - shift]], axis=0)
        g_slot = g_ref.at[slot]
        g_slot[...] = g
        g_last = g_slot[pl.ds(CHUNK - 1, SUBLANES, stride=0), :][0:1]
        qb = q.astype(BF16)
        kb = k.astype(BF16)
        qg_ref[rows, :] = qb * jnp.exp2(g).astype(BF16)

        attn = jnp.where(lvl == -1, jnp.sum(q * k, axis=-1, keepdims=True), 0.0)
        for level in range(CHUNK_LEVELS):
            dec = jnp.exp2(_neg_abs(g - _level_reference(g_slot, level, sub))).astype(BF16)
            a = lax.dot_general(qb * dec, kb * dec, _NT, preferred_element_type=F32)
            attn = jnp.where(lvl == level, a, attn)
        attn_ref[c] = attn.astype(BF16)

        k_dec = kb * jnp.exp2(g_last - g).astype(BF16)
        upd_ref[c] = lax.dot_general(v, k_dec, _TN, preferred_element_type=F32)
        keep_ref[c] = jnp.broadcast_to(jnp.exp2(g_last), (SUBLANES, HGRN_DIM))

    def local_group(i, carry):
        for slot in range(HGRN_UNROLL):
            local_step(i * HGRN_UNROLL + slot, slot)
        return carry

    def state_step(c, state_t):
        state_ref[c] = state_t.astype(BF16)
        return keep_ref[c][0:1] * state_t + upd_ref[c]

    def out_step(c, carry):
        rows = chunk_rows(c)
        o = lax.dot_general(qg_ref[rows, :], state_ref[c], _NT, preferred_element_type=F32)
        o = o + jnp.dot(attn_ref[c], hi_ref[rows, :], preferred_element_type=F32)
        ms = jnp.mean(o * o, axis=-1, keepdims=True)
        y = o * lax.rsqrt(ms + NORM_EPS) * hn * _silu(hg_ref[rows, :].astype(F32))
        o_ref[rows, :] = y.astype(BF16)
        return carry

    n_chunks = seq // CHUNK
    state0 = jnp.zeros((HGRN_DIM, HGRN_DIM), F32)
    if n_pieces:
        bounds = [(p * n_chunks) // n_pieces for p in range(n_pieces + 1)]
        state_t = state0
        for p in range(n_pieces):
            piece_fn(p)
            for c in range(bounds[p], bounds[p + 1]):
                local_step(c, c - bounds[p])
                if c >= 1:
                    state_t = state_step(c - 1, state_t)
                if c >= 2:
                    out_step(c - 2, 0)
        state_step(n_chunks - 1, state_t)
        out_step(n_chunks - 2, 0)
        out_step(n_chunks - 1, 0)
    else:
        def finish_group(i, state_t):
            for slot in range(HGRN_UNROLL):
                c = i * HGRN_UNROLL + slot
                state_t = state_step(c, state_t)
                out_step(c, 0)
            return state_t

        def lagged_group(i, state_t):
            state_t = finish_group(i - 1, state_t)
            local_group(i, 0)
            return state_t

        n_groups = n_chunks // HGRN_UNROLL
        local_group(0, 0)
        state_t = lax.fori_loop(1, n_groups, lagged_group, state0)
        finish_group(n_groups - 1, state_t)


def _hgrn_scratch(seq):
    n_chunks = seq // CHUNK
    return [
        pltpu.VMEM((HGRN_SLOTS, CHUNK, HGRN_DIM), F32),
        pltpu.VMEM((seq, HGRN_DIM), BF16),
        pltpu.VMEM((n_chunks, CHUNK, CHUNK), BF16),
        pltpu.VMEM((n_chunks, HGRN_DIM, HGRN_DIM), F32),
        pltpu.VMEM((n_chunks, SUBLANES, HGRN_DIM), F32),
        pltpu.VMEM((n_chunks, HGRN_DIM, HGRN_DIM), BF16),
    ]


def _hgrn(hg_seg, lb_logits, hgrn_norm, yh_prev, batch0, tokens, seq):
    width = HGRN_HEADS_PER_STEP * HGRN_DIM
    seg_batch = hg_seg.shape[0] // seq
    blk = lambda off: pl.BlockSpec((seq, width), lambda b, h: (b, (off - HG_COL0) // width + h))
    carried = [] if yh_prev is None else [yh_prev]
    return pl.pallas_call(
        functools.partial(_hgrn_kernel, len(carried)),
        grid=(seg_batch, HGRN_HEADS // HGRN_HEADS_PER_STEP),
        in_specs=[
            pl.BlockSpec((lb_logits.shape[0], width), lambda b, h: (0, h)),
            pl.BlockSpec((HGRN_HEADS, HGRN_DIM), lambda b, h: (0, 0)),
            blk(OFF_HQ), blk(OFF_HF), blk(OFF_HI), blk(OFF_HG),
        ] + [pl.BlockSpec(memory_space=pl.ANY) for _ in carried],
        out_specs=pl.BlockSpec((seq, width), lambda b, h: (batch0 + b, h)),
        out_shape=jax.ShapeDtypeStruct((tokens, HGRN_WIDTH), BF16),
        input_output_aliases={6: 0} if carried else {},
        scratch_shapes=_hgrn_scratch(seq),
        compiler_params=pltpu.CompilerParams(dimension_semantics=("parallel", "parallel")),
        name="hgrn2",
    )(lb_logits, hgrn_norm, hg_seg, hg_seg, hg_seg, hg_seg, *carried)


def _inproj_hgrn_kernel(heads_per_tile, n_carried, x_ref, gain_ref, w_ref, lb_ref, hn_ref, hq_ref,
                        hf_ref, hi_ref, hg_ref, *rest):
    o_ref, hg_out_ref, yh_ref, h_ref, *hgrn_scratch = rest[n_carried:]
    col_step = pl.program_id(1)

    @pl.when(col_step == 0)
    def _():
        _prenorm(x_ref, gain_ref, h_ref)

    def piece(p):
        cs = slice(p * IN_PIECE, (p + 1) * IN_PIECE)
        tile = jnp.dot(h_ref[...], w_ref[:, cs], preferred_element_type=F32).astype(BF16)
        o_ref[:, cs] = tile
        hg_out_ref[:, cs] = tile

    @pl.when(col_step < heads_per_tile)
    def _():
        head = (pl.program_id(0) * heads_per_tile + col_step) % HGRN_HEADS
        _hgrn_head(lb_ref, hn_ref.at[pl.ds(head, 1), :], hq_ref, hf_ref, hi_ref, hg_ref, yh_ref,
                   *hgrn_scratch, piece_fn=piece, n_pieces=IN_PIECES)

    @pl.when(col_step >= heads_per_tile)
    def _():
        o_ref[...] = jnp.dot(h_ref[...], w_ref[...], preferred_element_type=F32).astype(BF16)
        _copy_hgrn_tile(o_ref, hg_out_ref)


def _inproj_hgrn(x2d, gain, w_bf16, row_tile0, n_row_tiles, hg_prev, batch0_prev, proj_prev, yh_prev,
                 lb_logits, hgrn_norm, seq):
    tokens = x2d.shape[0]
    n_heads = (hg_prev.shape[0] // seq) * HGRN_HEADS
    heads_per_tile = n_heads // n_row_tiles
    assert heads_per_tile * n_row_tiles == n_heads
    assert heads_per_tile <= HG_TILE0 + HG_TILES
    assert pl.cdiv(seq // CHUNK, IN_PIECES) <= HGRN_SLOTS

    def flat_head(i, j):
        return i * heads_per_tile + jnp.minimum(j, heads_per_tile - 1)

    def blk(off):
        return pl.BlockSpec((seq, HGRN_DIM), lambda i, j: (
            flat_head(i, j) // HGRN_HEADS, (off - HG_COL0) // HGRN_DIM + flat_head(i, j) % HGRN_HEADS))

    carried = [proj_prev] + ([] if yh_prev is None else [yh_prev])
    n_read = 9
    aliases = {n_read: 0} if yh_prev is None else {n_read: 0, n_read + 1: 2}
    return pl.pallas_call(
        functools.partial(_inproj_hgrn_kernel, heads_per_tile, len(carried)),
        grid=(n_row_tiles, IN_PROJ_WIDTH // IN_TN),
        in_specs=_inproj_specs(row_tile0) + [
            pl.BlockSpec((lb_logits.shape[0], HGRN_DIM),
                         lambda i, j: (0, flat_head(i, j) % HGRN_HEADS)),
            pl.BlockSpec((HGRN_HEADS, HGRN_DIM), lambda i, j: (0, 0)),
            blk(OFF_HQ), blk(OFF_HF), blk(OFF_HI), blk(OFF_HG),
        ] + [pl.BlockSpec(memory_space=pl.ANY) for _ in carried],
        out_specs=_inproj_out_specs(row_tile0) + [
            pl.BlockSpec((seq, HGRN_DIM), lambda i, j: (
                batch0_prev + flat_head(i, j) // HGRN_HEADS, flat_head(i, j) % HGRN_HEADS))],
        out_shape=_inproj_out_shapes(tokens, n_row_tiles) + [
            jax.ShapeDtypeStruct((tokens, HGRN_WIDTH), BF16)],
        input_output_aliases=aliases,
        scratch_shapes=[pltpu.VMEM((IN_TM, D_MODEL), BF16)] + _hgrn_scratch(seq),
        compiler_params=pltpu.CompilerParams(
            dimension_semantics=("arbitrary", "arbitrary"),
            vmem_limit_bytes=IN_HGRN_VMEM_BYTES),
        name="inproj_hgrn2",
    )(x2d, gain, w_bf16, lb_logits, hgrn_norm, hg_prev, hg_prev, hg_prev, hg_prev, *carried)


SM_BLOCKS = 2
SM_TM = SM_BLOCKS * WINDOW
OUT_TN = 512
OUT_NCHUNK = D_MODEL // OUT_TN
SM_VMEM_BYTES = 52 * MIB


def _t5_bucket(dist):
    max_exact = REL_BUCKETS // 2
    d = jnp.maximum(dist, 0)
    df = jnp.maximum(d, 1).astype(F32)
    large = max_exact + (jnp.log(df / max_exact) / math.log(REL_MAX_DIST / max_exact)
                         * (REL_BUCKETS - max_exact)).astype(jnp.int32)
    large = jnp.minimum(large, REL_BUCKETS - 1)
    return jnp.where(d < max_exact, d, large)


def _swap_halves(x):
    half = LANES // 2
    return jnp.concatenate([x[:, half:], x[:, :half]], axis=1)


def _swa_tables(relb_ref, bias_ref, cur_ref):
    qi = lax.broadcasted_iota(jnp.int32, (WINDOW, WINDOW), 0)
    sj = lax.broadcasted_iota(jnp.int32, (WINDOW, WINDOW), 1)
    own = sj <= qi
    bucket = _t5_bucket(jnp.where(own, qi - sj, qi + WINDOW - sj))
    for h in range(ATTN_HEADS):
        def pick(b, acc, h=h):
            return jnp.where(bucket == b, relb_ref[b, h], acc)
        tbl = lax.fori_loop(0, REL_BUCKETS, pick, jnp.zeros((WINDOW, WINDOW), F32))
        bias_ref[0, h] = tbl
        bias_ref[1, h] = jnp.where(own, tbl, MASK_VALUE)
    own_bf = jnp.where(own, 1.0, 0.0).astype(BF16)
    cur_ref[...] = jnp.concatenate([own_bf, own_bf], axis=0)


def _swa_merge_kernel(blocks_per_seq, n_tiles, relb_ref, sink_ref, q_ref, kc_ref, vc_ref, kp_ref,
                      vp_ref, ga_ref, gb_ref, yh_ref, *rest):
    gate_a_refs = rest[:OUT_NCHUNK]
    gate_h_refs = rest[OUT_NCHUNK:2 * OUT_NCHUNK]
    (x_ref, wa_ref, wh_ref, wo_ref, gain_ref, o_ref,
     bias_ref, cur_ref, ya_ref, m_ref) = rest[2 * OUT_NCHUNK:]
    step = pl.program_id(0)

    @pl.when(step == 0)
    def _():
        _swa_tables(relb_ref, bias_ref, cur_ref)
        ya_ref[...] = jnp.zeros_like(ya_ref)

    lane = lax.broadcasted_iota(jnp.int32, (1, LANES), 1)
    is_lo = lane < LANES // 2
    scale = ATTN_HEAD_DIM ** -0.5
    keep_lo = jnp.where(is_lo, 1.0, 0.0).astype(BF16)
    keep_hi = jnp.where(is_lo, 0.0, 1.0).astype(BF16)
    row = lax.broadcasted_iota(jnp.int32, (2 * WINDOW, 1), 0)
    top = row < WINDOW
    own2 = (lax.broadcasted_iota(jnp.int32, (2 * WINDOW, WINDOW), 1)
            <= (lax.broadcasted_iota(jnp.int32, (2 * WINDOW, WINDOW), 0) & (WINDOW - 1)))
    tile = jnp.minimum(step, n_tiles - 1)
    tile_starts_seq = jnp.where(tile % (blocks_per_seq // SM_BLOCKS) == 0, 1, 0)

    def scores(qm, km, first, heads):
        s2 = lax.dot_general(qm, km, _NT, preferred_element_type=F32)
        bias = jnp.concatenate([bias_ref[first, heads[0]], bias_ref[first, heads[1]]], axis=0)
        return jnp.where(own2, s2[:, WINDOW:], s2[:, :WINDOW]) + bias

    def probs(s, heads):
        sink = jnp.where(top, sink_ref[0, heads[0]], sink_ref[0, heads[1]])
        m = jnp.maximum(jnp.max(s, axis=-1, keepdims=True), sink)
        p = jnp.exp(s - m)
        denom = jnp.sum(p, axis=-1, keepdims=True) + jnp.exp(sink - m)
        pb = p.astype(BF16)
        p_own = pb * cur_ref[...]
        return jnp.concatenate([pb - p_own, p_own], axis=1), 1.0 / denom

    def heads_of(hk):
        e = hk % 2
        return (4 * hk + e, 4 * hk + 2 + e), (4 * hk + 1 - e, 4 * hk + 3 - e)

    ya_prev = ya_ref[...]
    yh = yh_ref[...]
    for j in range(OUT_NCHUNK):
        cs = slice(j * OUT_TN, (j + 1) * OUT_TN)
        ua = jnp.dot(ya_prev, wa_ref[:, cs], preferred_element_type=F32)
        uh = jnp.dot(yh, wh_ref[:, cs], preferred_element_type=F32)
        merged = (_sigmoid(gate_a_refs[j][...].astype(F32)) * ua
                  + _sigmoid(gate_h_refs[j][...].astype(F32)) * uh)
        m_ref[:, cs] = merged.astype(BF16)

    s_same, s_diff, vals = [], [], []
    for blk in range(SM_BLOCKS):
        rows = slice(blk * WINDOW, (blk + 1) * WINDOW)
        first = tile_starts_seq if blk == 0 else 0
        for hk in range(ATTN_KV_HEADS):
            c, e = divmod(hk, 2)
            cs = slice(c * LANES, (c + 1) * LANES)
            if blk == 0:
                k_prev, v_prev = kp_ref[:, cs], vp_ref[:, cs]
            else:
                prev_rows = slice((blk - 1) * WINDOW, blk * WINDOW)
                k_prev, v_prev = kc_ref[prev_rows, cs], vc_ref[prev_rows, cs]
            k2 = jnp.concatenate([k_prev, kc_ref[rows, cs]], axis=0)
            v2 = jnp.concatenate([v_prev, vc_ref[rows, cs]], axis=0)
            k2s, v2s = _swap_halves(k2), _swap_halves(v2)
            keep_e, keep_o = (keep_lo, keep_hi) if e == 0 else (keep_hi, keep_lo)
            p0 = q_ref[rows, hk * 2 * LANES:hk * 2 * LANES + LANES]
            p1 = q_ref[rows, hk * 2 * LANES + LANES:(hk + 1) * 2 * LANES]
            q_same = jnp.concatenate([p0 * (keep_e * scale), p1 * (keep_e * scale)], axis=0)
            q_diff = jnp.concatenate([p0 * (keep_o * scale), p1 * (keep_o * scale)], axis=0)
            h_same, h_diff = heads_of(hk)
            s_same.append(scores(q_same, k2, first, h_same))
            s_diff.append(scores(q_diff, k2s, first, h_diff))
            vals.append((v2 * keep_lo, v2s * keep_hi) if e == 0 else (v2s * keep_lo, v2 * keep_hi))

    y = jnp.dot(m_ref[...], wo_ref[...], preferred_element_type=F32)

    pr_same = [probs(s_same[i], heads_of(i % ATTN_KV_HEADS)[0]) for i in range(len(s_same))]
    pr_diff = [probs(s_diff[i], heads_of(i % ATTN_KV_HEADS)[1]) for i in range(len(s_diff))]
    for blk in range(SM_BLOCKS):
        rows = slice(blk * WINDOW, (blk + 1) * WINDOW)
        for hk in range(ATTN_KV_HEADS):
            i = blk * ATTN_KV_HEADS + hk
            (p_lo, r_lo), (p_hi, r_hi) = ((pr_same[i], pr_diff[i]) if hk % 2 == 0
                                          else (pr_diff[i], pr_same[i]))
            v_lo, v_hi = vals[i]
            pv = jnp.dot(jnp.concatenate([p_lo, p_hi], axis=1),
                         jnp.concatenate([v_lo, v_hi], axis=0),
                         preferred_element_type=F32)
            pv = pv * jnp.where(is_lo, r_lo, r_hi)
            g_ref = ga_ref if hk < 2 else gb_ref
            for pair in range(2):
                gs = slice((hk % 2) * 2 * LANES + pair * LANES,
                           (hk % 2) * 2 * LANES + (pair + 1) * LANES)
                gate = _silu(g_ref[rows, gs].astype(F32))
                ya_ref[rows, (2 * hk + pair) * LANES:(2 * hk + pair + 1) * LANES] = (
                    pv[pair * WINDOW:(pair + 1) * WINDOW] * gate).astype(BF16)

    ms = jnp.mean(y * y, axis=-1, keepdims=True)
    o_ref[...] = x_ref[...] + y * lax.rsqrt(ms + NORM_EPS) * gain_ref[...]


def _swa_merge(proj, yh, x2d, rel_bias, sinks, wa, wh, wo, gain, seq):
    tokens = x2d.shape[0]
    n_tiles = tokens // SM_TM
    blocks_per_seq = seq // WINDOW
    attn_tile = lambda t: jnp.minimum(t, n_tiles - 1)
    merge_tile = lambda t: jnp.maximum(t - 1, 0)
    prev_block = lambda t: jnp.maximum(attn_tile(t) * SM_BLOCKS - 1, 0)
    kcol, vcol = OFF_AK // KV_WIDTH, OFF_AV // KV_WIDTH
    gcol = OFF_AG // (ATTN_WIDTH // 2)
    const = lambda t: (0, 0)
    smem = pl.BlockSpec(memory_space=pltpu.SMEM)
    gate = lambda off, j: pl.BlockSpec((SM_TM, OUT_TN), lambda t: (merge_tile(t), off // OUT_TN + j))
    resident = functools.partial(pl.BlockSpec, index_map=const, pipeline_mode=pl.Buffered(1))
    return pl.pallas_call(
        functools.partial(_swa_merge_kernel, blocks_per_seq, n_tiles),
        grid=(n_tiles + 1,),
        in_specs=(
            [smem, smem,
             pl.BlockSpec((SM_TM, ATTN_WIDTH), lambda t: (attn_tile(t), 0)),
             pl.BlockSpec((SM_TM, KV_WIDTH), lambda t: (attn_tile(t), kcol)),
             pl.BlockSpec((SM_TM, KV_WIDTH), lambda t: (attn_tile(t), vcol)),
             pl.BlockSpec((WINDOW, KV_WIDTH), lambda t: (prev_block(t), kcol)),
             pl.BlockSpec((WINDOW, KV_WIDTH), lambda t: (prev_block(t), vcol)),
             pl.BlockSpec((SM_TM, ATTN_WIDTH // 2), lambda t: (attn_tile(t), gcol)),
             pl.BlockSpec((SM_TM, ATTN_WIDTH // 2), lambda t: (attn_tile(t), gcol + 1)),
             pl.BlockSpec((SM_TM, HGRN_WIDTH), lambda t: (merge_tile(t), 0))]
            + [gate(OFF_GA, j) for j in range(OUT_NCHUNK)]
            + [gate(OFF_GH, j) for j in range(OUT_NCHUNK)]
            + [pl.BlockSpec((SM_TM, D_MODEL), lambda t: (merge_tile(t), 0)),
               resident((ATTN_WIDTH, D_MODEL)), resident((HGRN_WIDTH, D_MODEL)),
               resident((D_MODEL, D_MODEL)), pl.BlockSpec((1, D_MODEL), const)]),
        out_specs=pl.BlockSpec((SM_TM, D_MODEL), lambda t: (merge_tile(t), 0)),
        out_shape=jax.ShapeDtypeStruct((tokens, D_MODEL), F32),
        scratch_shapes=[
            pltpu.VMEM((2, ATTN_HEADS, WINDOW, WINDOW), F32),
            pltpu.VMEM((2 * WINDOW, WINDOW), BF16),
            pltpu.VMEM((SM_TM, ATTN_WIDTH), BF16),
            pltpu.VMEM((SM_TM, D_MODEL), BF16),
        ],
        compiler_params=pltpu.CompilerParams(
            dimension_semantics=("arbitrary",), vmem_limit_bytes=SM_VMEM_BYTES),
        name="swa_merge",
    )(rel_bias, sinks, proj, proj, proj, proj, proj, proj, proj, yh,
      *([proj] * (2 * OUT_NCHUNK)), x2d, wa, wh, wo, gain)


def kernel(x, norm_pre, w_in, rel_bias, attn_sinks, lb_logits, hgrn_norm, w_branch_attn,
           w_branch_hgrn, w_out, norm_post):
    batch, seq, d_model = x.shape
    depth = w_in.shape[0]
    assert depth == 1 and d_model == D_MODEL and seq % IN_TM == 0
    assert lb_logits.shape == (depth + 1, HGRN_WIDTH)
    assert sum(SEG_BATCHES) == batch
    x2d = x.reshape(batch * seq, d_model)
    layer = 0
    gain_pre = norm_pre[layer][None, :]
    w_bf16 = w_in[layer].astype(BF16)
    tiles_per_seq = seq // IN_TM
    proj, hg = _inproj(x2d, gain_pre, w_bf16, 0, SEG_BATCHES[0] * tiles_per_seq)
    yh = None
    batch0 = 0
    for prev_batches, seg_batches in zip(SEG_BATCHES[:-1], SEG_BATCHES[1:]):
        proj, hg, yh = _inproj_hgrn(
            x2d, gain_pre, w_bf16, (batch0 + prev_batches) * tiles_per_seq,
            seg_batches * tiles_per_seq, hg, batch0, proj, yh, lb_logits, hgrn_norm[layer], seq)
        batch0 += prev_batches
    yh = _hgrn(hg, lb_logits, hgrn_norm[layer], yh, batch0, batch * seq, seq)
    out = _swa_merge(proj, yh, x2d, rel_bias, attn_sinks[layer][None, :],
                     w_branch_attn[layer].astype(BF16), w_branch_hgrn[layer].astype(BF16),
                     w_out[layer].astype(BF16), norm_post[layer][None, :], seq)
    return out.reshape(batch, seq, d_model)
```

```python
import functools
import math

import jax
import jax.numpy as jnp
from jax import lax
from jax.experimental import pallas as pl
from jax.experimental.pallas import tpu as pltpu

D_MODEL = 2048
ATTN_HEADS = 16
ATTN_KV_HEADS = 4
ATTN_HEAD_DIM = 64
WINDOW = 128
ATTN_WIDTH = ATTN_HEADS * ATTN_HEAD_DIM
KV_WIDTH = ATTN_KV_HEADS * ATTN_HEAD_DIM
HGRN_HEADS = 8
HGRN_DIM = 128
HGRN_WIDTH = HGRN_HEADS * HGRN_DIM
CHUNK = 64
CHUNK_LEVELS = 6
HGRN_UNROLL = 16
HGRN_SLOTS = 16
HGRN_HEADS_PER_STEP = 2
SEG_BATCHES = (2,) * 8
REL_BUCKETS = 32
REL_MAX_DIST = 128
NORM_EPS = 1e-6
IN_PROJ_WIDTH = 10752

OFF_AQ = 0
OFF_AK = OFF_AQ + ATTN_WIDTH
OFF_AV = OFF_AK + KV_WIDTH
OFF_AG = OFF_AV + KV_WIDTH
OFF_HQ = OFF_AG + ATTN_WIDTH
OFF_HF = OFF_HQ + HGRN_WIDTH
OFF_HI = OFF_HF + HGRN_WIDTH
OFF_HG = OFF_HI + HGRN_WIDTH
OFF_GA = OFF_HG + HGRN_WIDTH
OFF_GH = OFF_GA + D_MODEL

LANES = 128
SUBLANES = 8
MASK_VALUE = -1e30

F32 = jnp.float32
BF16 = jnp.bfloat16

_NT = (((1,), (1,)), ((), ()))
_TN = (((0,), (0,)), ((), ()))


def _sigmoid(x):
    return 1.0 / (1.0 + jnp.exp(-x))


def _silu(x):
    return x * _sigmoid(x)


IN_TM = 1024
IN_TN = 1536
IN_NORM_ROWS = 256
IN_PIECE = 256
IN_PIECES = IN_TN // IN_PIECE
MIB = 1024 * 1024
IN_VMEM_BYTES = 56 * MIB
IN_HGRN_VMEM_BYTES = 58 * MIB


def _prenorm(x_ref, gain_ref, h_ref):
    gain = gain_ref[...]
    for r in range(0, IN_TM, IN_NORM_ROWS):
        xf = x_ref[r:r + IN_NORM_ROWS, :]
        ms = jnp.mean(xf * xf, axis=-1, keepdims=True)
        h_ref[r:r + IN_NORM_ROWS, :] = (xf * lax.rsqrt(ms + NORM_EPS) * gain).astype(BF16)


HG_TILE0 = OFF_HQ // IN_TN
HG_TILES = (OFF_GA - 1) // IN_TN - HG_TILE0 + 1
HG_COL0 = HG_TILE0 * IN_TN


def _copy_hgrn_tile(o_ref, hg_ref):
    col_step = pl.program_id(1)

    @pl.when((col_step >= HG_TILE0) & (col_step < HG_TILE0 + HG_TILES))
    def _():
        hg_ref[...] = o_ref[...]


def _inproj_kernel(x_ref, gain_ref, w_ref, o_ref, hg_ref, h_ref):
    @pl.when(pl.program_id(1) == 0)
    def _():
        _prenorm(x_ref, gain_ref, h_ref)

    o_ref[...] = jnp.dot(h_ref[...], w_ref[...], preferred_element_type=F32).astype(BF16)
    _copy_hgrn_tile(o_ref, hg_ref)


def _inproj_specs(row_tile0):
    return [
        pl.BlockSpec((IN_TM, D_MODEL), lambda i, j: (row_tile0 + i, 0)),
        pl.BlockSpec((1, D_MODEL), lambda i, j: (0, 0)),
        pl.BlockSpec((D_MODEL, IN_TN), lambda i, j: (0, j)),
    ]


def _inproj_out_specs(row_tile0):
    hg_tile = lambda j: jnp.clip(j - HG_TILE0, 0, HG_TILES - 1)
    return [pl.BlockSpec((IN_TM, IN_TN), lambda i, j: (row_tile0 + i, j)),
            pl.BlockSpec((IN_TM, IN_TN), lambda i, j: (i, hg_tile(j)))]


def _inproj_out_shapes(tokens, n_row_tiles):
    return [jax.ShapeDtypeStruct((tokens, IN_PROJ_WIDTH), BF16),
            jax.ShapeDtypeStruct((n_row_tiles * IN_TM, HG_TILES * IN_TN), BF16)]


def _inproj(x2d, gain, w_bf16, row_tile0, n_row_tiles):
    return pl.pallas_call(
        _inproj_kernel,
        grid=(n_row_tiles, IN_PROJ_WIDTH // IN_TN),
        in_specs=_inproj_specs(row_tile0),
        out_specs=_inproj_out_specs(row_tile0),
        out_shape=_inproj_out_shapes(x2d.shape[0], n_row_tiles),
        scratch_shapes=[pltpu.VMEM((IN_TM, D_MODEL), BF16)],
        compiler_params=pltpu.CompilerParams(
            dimension_semantics=("parallel", "arbitrary"),
            vmem_limit_bytes=IN_VMEM_BYTES),
        name="inproj",
    )(x2d, gain, w_bf16)


def _level_reference(g_ref, level, sub):
    half = 1 << level
    blk = 2 * half

    def row_tile(r):
        return g_ref[pl.ds(r, SUBLANES, stride=0), :]

    if blk >= SUBLANES:
        return jnp.concatenate(
            [row_tile(b * blk + half) for b in range(CHUNK // blk) for _ in range(blk // SUBLANES)],
            axis=0)
    tiles = []
    for t in range(CHUNK // SUBLANES):
        acc = row_tile(t * SUBLANES + half)
        for b in range(1, SUBLANES // blk):
            acc = jnp.where(sub >= b * blk, row_tile(t * SUBLANES + b * blk + half), acc)
        tiles.append(acc)
    return jnp.concatenate(tiles, axis=0)


def _neg_abs(x):
    bits = lax.bitcast_convert_type(x, jnp.uint32) | jnp.uint32(0x80000000)
    return lax.bitcast_convert_type(bits, F32)


def _hgrn_kernel(n_carried, lb_ref, hn_ref, hq_ref, hf_ref, hi_ref, hg_ref, *rest):
    o_ref, *scratch = rest[n_carried:]
    for j in range(HGRN_HEADS_PER_STEP):
        cols = slice(j * HGRN_DIM, (j + 1) * HGRN_DIM)
        head = pl.program_id(1) * HGRN_HEADS_PER_STEP + j
        _hgrn_head(lb_ref.at[:, cols], hn_ref.at[pl.ds(head, 1), :], hq_ref.at[:, cols],
                   hf_ref.at[:, cols], hi_ref.at[:, cols], hg_ref.at[:, cols], o_ref.at[:, cols],
                   *scratch)


def _hgrn_head(lb_ref, hn_ref, hq_ref, hf_ref, hi_ref, hg_ref, o_ref,
               g_ref, qg_ref, attn_ref, upd_ref, keep_ref, state_ref, piece_fn=None, n_pieces=0):
    seq = hq_ref.shape[0]
    lg = lb_ref[...]
    ex = jnp.exp(lg - jnp.max(lg, axis=0, keepdims=True))
    lb = ex[0:1, :] / jnp.sum(ex, axis=0, keepdims=True)
    hn = hn_ref[...]

    rowi = lax.broadcasted_iota(jnp.int32, (CHUNK, HGRN_DIM), 0)
    sub = lax.broadcasted_iota(jnp.int32, (SUBLANES, HGRN_DIM), 0)
    ti = lax.broadcasted_iota(jnp.int32, (CHUNK, CHUNK), 0)
    si = lax.broadcasted_iota(jnp.int32, (CHUNK, CHUNK), 1)
    xor = ti ^ si
    lvl = jnp.full((CHUNK, CHUNK), -2, jnp.int32)
    for level in range(CHUNK_LEVELS):
        lvl = jnp.where(xor >= (1 << level), level, lvl)
    lvl = jnp.where(ti > si, lvl, jnp.where(ti == si, -1, -2))

    def chunk_rows(c):
        start = c * CHUNK
        return pl.ds(start if isinstance(c, int) else pl.multiple_of(start, CHUNK), CHUNK)

    def local_step(c, slot):
        rows = chunk_rows(c)
        q = _silu(hq_ref[rows, :].astype(F32))
        f = lb + (1.0 - lb) * _sigmoid(hf_ref[rows, :].astype(F32))
        k = 1.0 - f
        v = hi_ref[rows, :]
        g = jnp.log2(f)
        for shift in (1, 2, 4):
            g = g + jnp.where(rowi >= shift, pltpu.roll(g, shift, 0), 0.0)
        for shift in (8, 16, 32):
            g = g + jnp.concatenate(
                [jnp.zeros((shift, HGRN_DIM), F32), g[:CHUNK - shift]], axis=0)
        g_slot = g_ref.at[slot]
        g_slot[...] = g
        g_last = g_slot[pl.ds(CHUNK - 1, SUBLANES, stride=0), :][0:1]
        qb = q.astype(BF16)
        kb = k.astype(BF16)
        qg_ref[rows, :] = qb * jnp.exp2(g).astype(BF16)

        attn = jnp.where(lvl == -1, jnp.sum(q * k, axis=-1, keepdims=True), 0.0)
        for level in range(CHUNK_LEVELS):
            dec = jnp.exp2(_neg_abs(g - _level_reference(g_slot, level, sub))).astype(BF16)
            a = lax.dot_general(qb * dec, kb * dec, _NT, preferred_element_type=F32)
            attn = jnp.where(lvl == level, a, attn)
        attn_ref[c] = attn.astype(BF16)

        k_dec = kb * jnp.exp2(g_last - g).astype(BF16)
        upd_ref[c] = lax.dot_general(v, k_dec, _TN, preferred_element_type=F32)
        keep_ref[c] = jnp.broadcast_to(jnp.exp2(g_last), (SUBLANES, HGRN_DIM))

    def local_group(i, carry):
        for slot in range(HGRN_UNROLL):
            local_step(i * HGRN_UNROLL + slot, slot)
        return carry

    def state_step(c, state_t):
        state_ref[c] = state_t.astype(BF16)
        return keep_ref[c][0:1] * state_t + upd_ref[c]

    def out_step(c, carry):
        rows = chunk_rows(c)
        o = lax.dot_general(qg_ref[rows, :], state_ref[c], _NT, preferred_element_type=F32)
        o = o + jnp.dot(attn_ref[c], hi_ref[rows, :], preferred_element_type=F32)
        ms = jnp.mean(o * o, axis=-1, keepdims=True)
        y = o * lax.rsqrt(ms + NORM_EPS) * hn * _silu(hg_ref[rows, :].astype(F32))
        o_ref[rows, :] = y.astype(BF16)
        return carry

    n_chunks = seq // CHUNK
    state0 = jnp.zeros((HGRN_DIM, HGRN_DIM), F32)
    if n_pieces:
        bounds = [(p * n_chunks) // n_pieces for p in range(n_pieces + 1)]
        state_t = state0
        for p in range(n_pieces):
            piece_fn(p)
            for c in range(bounds[p], bounds[p + 1]):
                local_step(c, c - bounds[p])
                if c >= 1:
                    state_t = state_step(c - 1, state_t)
                if c >= 2:
                    out_step(c - 2, 0)
        state_step(n_chunks - 1, state_t)
        out_step(n_chunks - 2, 0)
        out_step(n_chunks - 1, 0)
    else:
        def finish_group(i, state_t):
            for slot in range(HGRN_UNROLL):
                c = i * HGRN_UNROLL + slot
                state_t = state_step(c, state_t)
                out_step(c, 0)
            return state_t

        def lagged_group(i, state_t):
            state_t = finish_group(i - 1, state_t)
            local_group(i, 0)
            return state_t

        n_groups = n_chunks // HGRN_UNROLL
        local_group(0, 0)
        state_t = lax.fori_loop(1, n_groups, lagged_group, state0)
        finish_group(n_groups - 1, state_t)


def _hgrn_scratch(seq):
    n_chunks = seq // CHUNK
    return [
        pltpu.VMEM((HGRN_SLOTS, CHUNK, HGRN_DIM), F32),
        pltpu.VMEM((seq, HGRN_DIM), BF16),
        pltpu.VMEM((n_chunks, CHUNK, CHUNK), BF16),
        pltpu.VMEM((n_chunks, HGRN_DIM, HGRN_DIM), F32),
        pltpu.VMEM((n_chunks, SUBLANES, HGRN_DIM), F32),
        pltpu.VMEM((n_chunks, HGRN_DIM, HGRN_DIM), BF16),
    ]


def _hgrn(hg_seg, lb_logits, hgrn_norm, yh_prev, batch0, tokens, seq):
    width = HGRN_HEADS_PER_STEP * HGRN_DIM
    seg_batch = hg_seg.shape[0] // seq
    blk = lambda off: pl.BlockSpec((seq, width), lambda b, h: (b, (off - HG_COL0) // width + h))
    carried = [] if yh_prev is None else [yh_prev]
    return pl.pallas_call(
        functools.partial(_hgrn_kernel, len(carried)),
        grid=(seg_batch, HGRN_HEADS // HGRN_HEADS_PER_STEP),
        in_specs=[
            pl.BlockSpec((lb_logits.shape[0], width), lambda b, h: (0, h)),
            pl.BlockSpec((HGRN_HEADS, HGRN_DIM), lambda b, h: (0, 0)),
            blk(OFF_HQ), blk(OFF_HF), blk(OFF_HI), blk(OFF_HG),
        ] + [pl.BlockSpec(memory_space=pl.ANY) for _ in carried],
        out_specs=pl.BlockSpec((seq, width), lambda b, h: (batch0 + b, h)),
        out_shape=jax.ShapeDtypeStruct((tokens, HGRN_WIDTH), BF16),
        input_output_aliases={6: 0} if carried else {},
        scratch_shapes=_hgrn_scratch(seq),
        compiler_params=pltpu.CompilerParams(dimension_semantics=("parallel", "parallel")),
        name="hgrn2",
    )(lb_logits, hgrn_norm, hg_seg, hg_seg, hg_seg, hg_seg, *carried)


def _inproj_hgrn_kernel(heads_per_tile, n_carried, x_ref, gain_ref, w_ref, lb_ref, hn_ref, hq_ref,
                        hf_ref, hi_ref, hg_ref, *rest):
    o_ref, hg_out_ref, yh_ref, h_ref, *hgrn_scratch = rest[n_carried:]
    col_step = pl.program_id(1)

    @pl.when(col_step == 0)
    def _():
        _prenorm(x_ref, gain_ref, h_ref)

    def piece(p):
        cs = slice(p * IN_PIECE, (p + 1) * IN_PIECE)
        tile = jnp.dot(h_ref[...], w_ref[:, cs], preferred_element_type=F32).astype(BF16)
        o_ref[:, cs] = tile
        hg_out_ref[:, cs] = tile

    @pl.when(col_step < heads_per_tile)
    def _():
        head = (pl.program_id(0) * heads_per_tile + col_step) % HGRN_HEADS
        _hgrn_head(lb_ref, hn_ref.at[pl.ds(head, 1), :], hq_ref, hf_ref, hi_ref, hg_ref, yh_ref,
                   *hgrn_scratch, piece_fn=piece, n_pieces=IN_PIECES)

    @pl.when(col_step >= heads_per_tile)
    def _():
        o_ref[...] = jnp.dot(h_ref[...], w_ref[...], preferred_element_type=F32).astype(BF16)
        _copy_hgrn_tile(o_ref, hg_out_ref)


def _inproj_hgrn(x2d, gain, w_bf16, row_tile0, n_row_tiles, hg_prev, batch0_prev, proj_prev, yh_prev,
                 lb_logits, hgrn_norm, seq):
    tokens = x2d.shape[0]
    n_heads = (hg_prev.shape[0] // seq) * HGRN_HEADS
    heads_per_tile = n_heads // n_row_tiles
    assert heads_per_tile * n_row_tiles == n_heads
    assert heads_per_tile <= HG_TILE0 + HG_TILES
    assert pl.cdiv(seq // CHUNK, IN_PIECES) <= HGRN_SLOTS

    def flat_head(i, j):
        return i * heads_per_tile + jnp.minimum(j, heads_per_tile - 1)

    def blk(off):
        return pl.BlockSpec((seq, HGRN_DIM), lambda i, j: (
            flat_head(i, j) // HGRN_HEADS, (off - HG_COL0) // HGRN_DIM + flat_head(i, j) % HGRN_HEADS))

    carried = [proj_prev] + ([] if yh_prev is None else [yh_prev])
    n_read = 9
    aliases = {n_read: 0} if yh_prev is None else {n_read: 0, n_read + 1: 2}
    return pl.pallas_call(
        functools.partial(_inproj_hgrn_kernel, heads_per_tile, len(carried)),
        grid=(n_row_tiles, IN_PROJ_WIDTH // IN_TN),
        in_specs=_inproj_specs(row_tile0) + [
            pl.BlockSpec((lb_logits.shape[0], HGRN_DIM),
                         lambda i, j: (0, flat_head(i, j) % HGRN_HEADS)),
            pl.BlockSpec((HGRN_HEADS, HGRN_DIM), lambda i, j: (0, 0)),
            blk(OFF_HQ), blk(OFF_HF), blk(OFF_HI), blk(OFF_HG),
        ] + [pl.BlockSpec(memory_space=pl.ANY) for _ in carried],
        out_specs=_inproj_out_specs(row_tile0) + [
            pl.BlockSpec((seq, HGRN_DIM), lambda i, j: (
                batch0_prev + flat_head(i, j) // HGRN_HEADS, flat_head(i, j) % HGRN_HEADS))],
        out_shape=_inproj_out_shapes(tokens, n_row_tiles) + [
            jax.ShapeDtypeStruct((tokens, HGRN_WIDTH), BF16)],
        input_output_aliases=aliases,
        scratch_shapes=[pltpu.VMEM((IN_TM, D_MODEL), BF16)] + _hgrn_scratch(seq),
        compiler_params=pltpu.CompilerParams(
            dimension_semantics=("arbitrary", "arbitrary"),
            vmem_limit_bytes=IN_HGRN_VMEM_BYTES),
        name="inproj_hgrn2",
    )(x2d, gain, w_bf16, lb_logits, hgrn_norm, hg_prev, hg_prev, hg_prev, hg_prev, *carried)


SM_BLOCKS = 2
SM_TM = SM_BLOCKS * WINDOW
OUT_TN = 256
OUT_NCHUNK = D_MODEL // OUT_TN
SM_VMEM_BYTES = 52 * MIB


def _t5_bucket(dist):
    max_exact = REL_BUCKETS // 2
    d = jnp.maximum(dist, 0)
    df = jnp.maximum(d, 1).astype(F32)
    large = max_exact + (jnp.log(df / max_exact) / math.log(REL_MAX_DIST / max_exact)
                         * (REL_BUCKETS - max_exact)).astype(jnp.int32)
    large = jnp.minimum(large, REL_BUCKETS - 1)
    return jnp.where(d < max_exact, d, large)


def _swap_halves(x):
    half = LANES // 2
    return jnp.concatenate([x[:, half:], x[:, :half]], axis=1)


def _swa_tables(relb_ref, bias_ref, cur_ref):
    qi = lax.broadcasted_iota(jnp.int32, (WINDOW, WINDOW), 0)
    sj = lax.broadcasted_iota(jnp.int32, (WINDOW, WINDOW), 1)
    own = sj <= qi
    bucket = _t5_bucket(jnp.where(own, qi - sj, qi + WINDOW - sj))
    for h in range(ATTN_HEADS):
        def pick(b, acc, h=h):
            return jnp.where(bucket == b, relb_ref[b, h], acc)
        tbl = lax.fori_loop(0, REL_BUCKETS, pick, jnp.zeros((WINDOW, WINDOW), F32))
        bias_ref[0, h] = tbl
        bias_ref[1, h] = jnp.where(own, tbl, MASK_VALUE)
    own_bf = jnp.where(own, 1.0, 0.0).astype(BF16)
    cur_ref[...] = jnp.concatenate([own_bf, own_bf], axis=0)


def _swa_merge_kernel(blocks_per_seq, n_tiles, relb_ref, sink_ref, q_ref, kc_ref, vc_ref, kp_ref,
                      vp_ref, ga_ref, gb_ref, yh_ref, *rest):
    gate_a_refs = rest[:OUT_NCHUNK]
    gate_h_refs = rest[OUT_NCHUNK:2 * OUT_NCHUNK]
    (x_ref, wa_ref, wh_ref, wo_ref, gain_ref, o_ref,
     bias_ref, cur_ref, ya_ref, m_ref) = rest[2 * OUT_NCHUNK:]
    step = pl.program_id(0)

    @pl.when(step == 0)
    def _():
        _swa_tables(relb_ref, bias_ref, cur_ref)
        ya_ref[...] = jnp.zeros_like(ya_ref)

    lane = lax.broadcasted_iota(jnp.int32, (1, LANES), 1)
    is_lo = lane < LANES // 2
    scale = ATTN_HEAD_DIM ** -0.5
    keep_lo = jnp.where(is_lo, 1.0, 0.0).astype(BF16)
    keep_hi = jnp.where(is_lo, 0.0, 1.0).astype(BF16)
    row = lax.broadcasted_iota(jnp.int32, (2 * WINDOW, 1), 0)
    top = row < WINDOW
    own2 = (lax.broadcasted_iota(jnp.int32, (2 * WINDOW, WINDOW), 1)
            <= (lax.broadcasted_iota(jnp.int32, (2 * WINDOW, WINDOW), 0) & (WINDOW - 1)))
    tile = jnp.minimum(step, n_tiles - 1)
    tile_starts_seq = jnp.where(tile % (blocks_per_seq // SM_BLOCKS) == 0, 1, 0)

    def scores(qm, km, first, heads):
        s2 = lax.dot_general(qm, km, _NT, preferred_element_type=F32)
        bias = jnp.concatenate([bias_ref[first, heads[0]], bias_ref[first, heads[1]]], axis=0)
        return jnp.where(own2, s2[:, WINDOW:], s2[:, :WINDOW]) + bias

    def probs(s, heads):
        sink = jnp.where(top, sink_ref[0, heads[0]], sink_ref[0, heads[1]])
        m = jnp.maximum(jnp.max(s, axis=-1, keepdims=True), sink)
        p = jnp.exp(s - m)
        denom = jnp.sum(p, axis=-1, keepdims=True) + jnp.exp(sink - m)
        pb = p.astype(BF16)
        p_own = pb * cur_ref[...]
        return jnp.concatenate([pb - p_own, p_own], axis=1), 1.0 / denom

    def heads_of(hk):
        e = hk % 2
        return (4 * hk + e, 4 * hk + 2 + e), (4 * hk + 1 - e, 4 * hk + 3 - e)

    ya_prev = ya_ref[...]
    yh = yh_ref[...]
    for j in range(OUT_NCHUNK):
        cs = slice(j * OUT_TN, (j + 1) * OUT_TN)
        ua = jnp.dot(ya_prev, wa_ref[:, cs], preferred_element_type=F32)
        uh = jnp.dot(yh, wh_ref[:, cs], preferred_element_type=F32)
        merged = (_sigmoid(gate_a_refs[j][...].astype(F32)) * ua
                  + _sigmoid(gate_h_refs[j][...].astype(F32)) * uh)
        m_ref[:, cs] = merged.astype(BF16)

    s_same, s_diff, vals = [], [], []
    for blk in range(SM_BLOCKS):
        rows = slice(blk * WINDOW, (blk + 1) * WINDOW)
        first = tile_starts_seq if blk == 0 else 0
        for hk in range(ATTN_KV_HEADS):
            c, e = divmod(hk, 2)
            cs = slice(c * LANES, (c + 1) * LANES)
            if blk == 0:
                k_prev, v_prev = kp_ref[:, cs], vp_ref[:, cs]
            else:
                prev_rows = slice((blk - 1) * WINDOW, blk * WINDOW)
                k_prev, v_prev = kc_ref[prev_rows, cs], vc_ref[prev_rows, cs]
            k2 = jnp.concatenate([k_prev, kc_ref[rows, cs]], axis=0)
            v2 = jnp.concatenate([v_prev, vc_ref[rows, cs]], axis=0)
            k2s, v2s = _swap_halves(k2), _swap_halves(v2)
            keep_e, keep_o = (keep_lo, keep_hi) if e == 0 else (keep_hi, keep_lo)
            p0 = q_ref[rows, hk * 2 * LANES:hk * 2 * LANES + LANES]
            p1 = q_ref[rows, hk * 2 * LANES + LANES:(hk + 1) * 2 * LANES]
            q_same = jnp.concatenate([p0 * (keep_e * scale), p1 * (keep_e * scale)], axis=0)
            q_diff = jnp.concatenate([p0 * (keep_o * scale), p1 * (keep_o * scale)], axis=0)
            h_same, h_diff = heads_of(hk)
            s_same.append(scores(q_same, k2, first, h_same))
            s_diff.append(scores(q_diff, k2s, first, h_diff))
            vals.append((v2 * keep_lo, v2s * keep_hi) if e == 0 else (v2s * keep_lo, v2 * keep_hi))

    y = jnp.dot(m_ref[...], wo_ref[...], preferred_element_type=F32)

    pr_same = [probs(s_same[i], heads_of(i % ATTN_KV_HEADS)[0]) for i in range(len(s_same))]
    pr_diff = [probs(s_diff[i], heads_of(i % ATTN_KV_HEADS)[1]) for i in range(len(s_diff))]
    for blk in range(SM_BLOCKS):
        rows = slice(blk * WINDOW, (blk + 1) * WINDOW)
        for hk in range(ATTN_KV_HEADS):
            i = blk * ATTN_KV_HEADS + hk
            (p_lo, r_lo), (p_hi, r_hi) = ((pr_same[i], pr_diff[i]) if hk % 2 == 0
                                          else (pr_diff[i], pr_same[i]))
            v_lo, v_hi = vals[i]
            pv = jnp.dot(jnp.concatenate([p_lo, p_hi], axis=1),
                         jnp.concatenate([v_lo, v_hi], axis=0),
                         preferred_element_type=F32)
            pv = pv * jnp.where(is_lo, r_lo, r_hi)
            g_ref = ga_ref if hk < 2 else gb_ref
            for pair in range(2):
                gs = slice((hk % 2) * 2 * LANES + pair * LANES,
                           (hk % 2) * 2 * LANES + (pair + 1) * LANES)
                gate = _silu(g_ref[rows, gs].astype(F32))
                ya_ref[rows, (2 * hk + pair) * LANES:(2 * hk + pair + 1) * LANES] = (
                    pv[pair * WINDOW:(pair + 1) * WINDOW] * gate).astype(BF16)

    ms = jnp.mean(y * y, axis=-1, keepdims=True)
    o_ref[...] = x_ref[...] + y * lax.rsqrt(ms + NORM_EPS) * gain_ref[...]


def _swa_merge(proj, yh, x2d, rel_bias, sinks, wa, wh, wo, gain, seq):
    tokens = x2d.shape[0]
    n_tiles = tokens // SM_TM
    blocks_per_seq = seq // WINDOW
    attn_tile = lambda t: jnp.minimum(t, n_tiles - 1)
    merge_tile = lambda t: jnp.maximum(t - 1, 0)
    prev_block = lambda t: jnp.maximum(attn_tile(t) * SM_BLOCKS - 1, 0)
    kcol, vcol = OFF_AK // KV_WIDTH, OFF_AV // KV_WIDTH
    gcol = OFF_AG // (ATTN_WIDTH // 2)
    const = lambda t: (0, 0)
    smem = pl.BlockSpec(memory_space=pltpu.SMEM)
    gate = lambda off, j: pl.BlockSpec((SM_TM, OUT_TN), lambda t: (merge_tile(t), off // OUT_TN + j))
    resident = functools.partial(pl.BlockSpec, index_map=const, pipeline_mode=pl.Buffered(1))
    return pl.pallas_call(
        functools.partial(_swa_merge_kernel, blocks_per_seq, n_tiles),
        grid=(n_tiles + 1,),
        in_specs=(
            [smem, smem,
             pl.BlockSpec((SM_TM, ATTN_WIDTH), lambda t: (attn_tile(t), 0)),
             pl.BlockSpec((SM_TM, KV_WIDTH), lambda t: (attn_tile(t), kcol)),
             pl.BlockSpec((SM_TM, KV_WIDTH), lambda t: (attn_tile(t), vcol)),
             pl.BlockSpec((WINDOW, KV_WIDTH), lambda t: (prev_block(t), kcol)),
             pl.BlockSpec((WINDOW, KV_WIDTH), lambda t: (prev_block(t), vcol)),
             pl.BlockSpec((SM_TM, ATTN_WIDTH // 2), lambda t: (attn_tile(t), gcol)),
             pl.BlockSpec((SM_TM, ATTN_WIDTH // 2), lambda t: (attn_tile(t), gcol + 1)),
             pl.BlockSpec((SM_TM, HGRN_WIDTH), lambda t: (merge_tile(t), 0))]
            + [gate(OFF_GA, j) for j in range(OUT_NCHUNK)]
            + [gate(OFF_GH, j) for j in range(OUT_NCHUNK)]
            + [pl.BlockSpec((SM_TM, D_MODEL), lambda t: (merge_tile(t), 0)),
               resident((ATTN_WIDTH, D_MODEL)), resident((HGRN_WIDTH, D_MODEL)),
               resident((D_MODEL, D_MODEL)), pl.BlockSpec((1, D_MODEL), const)]),
        out_specs=pl.BlockSpec((SM_TM, D_MODEL), lambda t: (merge_tile(t), 0)),
        out_shape=jax.ShapeDtypeStruct((tokens, D_MODEL), F32),
        scratch_shapes=[
            pltpu.VMEM((2, ATTN_HEADS, WINDOW, WINDOW), F32),
            pltpu.VMEM((2 * WINDOW, WINDOW), BF16),
            pltpu.VMEM((SM_TM, ATTN_WIDTH), BF16),
            pltpu.VMEM((SM_TM, D_MODEL), BF16),
        ],
        compiler_params=pltpu.CompilerParams(
            dimension_semantics=("arbitrary",), vmem_limit_bytes=SM_VMEM_BYTES),
        name="swa_merge",
    )(rel_bias, sinks, proj, proj, proj, proj, proj, proj, proj, yh,
      *([proj] * (2 * OUT_NCHUNK)), x2d, wa, wh, wo, gain)


def kernel(x, norm_pre, w_in, rel_bias, attn_sinks, lb_logits, hgrn_norm, w_branch_attn,
           w_branch_hgrn, w_out, norm_post):
    batch, seq, d_model = x.shape
    depth = w_in.shape[0]
    assert depth == 1 and d_model == D_MODEL and seq % IN_TM == 0
    assert lb_logits.shape == (depth + 1, HGRN_WIDTH)
    assert sum(SEG_BATCHES) == batch
    x2d = x.reshape(batch * seq, d_model)
    layer = 0
    gain_pre = norm_pre[layer][None, :]
    w_bf16 = w_in[layer].astype(BF16)
    tiles_per_seq = seq // IN_TM
    proj, hg = _inproj(x2d, gain_pre, w_bf16, 0, SEG_BATCHES[0] * tiles_per_seq)
    yh = None
    batch0 = 0
    for prev_batches, seg_batches in zip(SEG_BATCHES[:-1], SEG_BATCHES[1:]):
        proj, hg, yh = _inproj_hgrn(
            x2d, gain_pre, w_bf16, (batch0 + prev_batches) * tiles_per_seq,
            seg_batches * tiles_per_seq, hg, batch0, proj, yh, lb_logits, hgrn_norm[layer], seq)
        batch0 += prev_batches
    yh = _hgrn(hg, lb_logits, hgrn_norm[layer], yh, batch0, batch * seq, seq)
    out = _swa_merge(proj, yh, x2d, rel_bias, attn_sinks[layer][None, :],
                     w_branch_attn[layer].astype(BF16), w_branch_hgrn[layer].astype(BF16),
                     w_out[layer].astype(BF16), norm_post[layer][None, :], seq)
    return out.reshape(batch, seq, d_model)
```
